```python
import math
import jax, jax.numpy as jnp
from jax import lax
import numpy as np

D_MODEL = 4096
BATCH = 32
SEQ = 256
DEPTH = 2
DEC_BATCH = 2
DEC_SEQ = 4096
PAST_LEN = 512

GRID_W = 64
CHUNK = 128
BRANCH_W = 2048
RET_HEADS = 8
RET_DK = 128
RET_DV = BRANCH_W // RET_HEADS
MLP_GROUPS = 8
MLP_WIDTH = BRANCH_W
MLP_GW = MLP_WIDTH // MLP_GROUPS
DIFF_HEADS = 8
DIFF_DH = 128
DIFF_DV = BRANCH_W // DIFF_HEADS
FFN_HIDDEN = -(-8 * D_MODEL // (3 * 256)) * 256
ROPE_BASE = 10000.0
EPS = 1e-6
IN_SPLITS = (RET_HEADS * RET_DK, RET_HEADS * RET_DK, RET_HEADS * RET_DV, RET_HEADS * RET_DV,
             MLP_WIDTH, MLP_WIDTH,
             DIFF_HEADS * 2 * DIFF_DH, DIFF_HEADS * 2 * DIFF_DH, DIFF_HEADS * DIFF_DV,
             3 * D_MODEL)
IN_WIDTH = sum(IN_SPLITS)

kernel_name = 'hybrid_retention_gmlp_diffattn_dit_step'


def rmsnorm(x, g):
    xf = x.astype(jnp.float32)
    y = xf * lax.rsqrt(jnp.mean(xf * xf, axis=-1, keepdims=True) + EPS)
    return (y * g.astype(jnp.float32)).astype(x.dtype)


def head_layernorm(x):
    xf = x.astype(jnp.float32)
    mu = jnp.mean(xf, axis=-1, keepdims=True)
    xc = xf - mu
    return xc * lax.rsqrt(jnp.mean(xc * xc, axis=-1, keepdims=True) + EPS)


def split_cols(z):
    idx = np.cumsum(IN_SPLITS)[:-1].tolist()
    return jnp.split(z, idx, axis=-1)


def axial_rope_tables(n_tok, head_dim):
    n_rows = n_tok // GRID_W
    rows = jnp.repeat(jnp.arange(n_rows, dtype=jnp.float32), GRID_W)
    cols = jnp.tile(jnp.arange(GRID_W, dtype=jnp.float32), n_rows)
    axis_dim = head_dim // 2
    inv = ROPE_BASE ** (-jnp.arange(0, axis_dim, 2, dtype=jnp.float32) / axis_dim)
    ang = jnp.stack([rows[:, None] * inv, cols[:, None] * inv], axis=1)
    return jnp.cos(ang), jnp.sin(ang)


def apply_axial_rope(x, cos, sin):
    shp = x.shape
    axis_dim = shp[-1] // 2
    xr = x.astype(jnp.float32).reshape(shp[:-1] + (2, axis_dim))
    x1, x2 = xr[..., : axis_dim // 2], xr[..., axis_dim // 2:]
    bshape = (1, shp[1]) + (1,) * (len(shp) - 3) + cos.shape[1:]
    cb, sb = cos.reshape(bshape), sin.reshape(bshape)
    out = jnp.concatenate([x1 * cb - x2 * sb, x2 * cb + x1 * sb], axis=-1)
    return out.reshape(shp).astype(x.dtype)


def retention_scan(q, k, v, log_g, s0):
    B, S, H, _ = q.shape
    DV = v.shape[-1]
    n = S // CHUNK

    def chunks(t):
        return t.astype(jnp.float32).reshape(B, n, CHUNK, H, t.shape[-1]).transpose(1, 0, 3, 2, 4)

    qc, kc, vc = chunks(q), chunks(k), chunks(v)
    pos = jnp.arange(CHUNK, dtype=jnp.float32)
    rel = pos[:, None] - pos[None, :]
    intra_decay = jnp.where(rel >= 0, jnp.exp(log_g[:, None, None] * jnp.maximum(rel, 0.0)), 0.0)
    read_decay = jnp.exp(log_g[:, None] * (pos + 1.0))[None, :, :, None]
    write_decay = jnp.exp(log_g[:, None] * (CHUNK - 1.0 - pos))[None, :, :, None]
    chunk_decay = jnp.exp(log_g * CHUNK)[None, :, None, None]

    def step(state, inp):
        qi, ki, vi = inp
        att = jnp.einsum('bhqd,bhkd->bhqk', qi, ki) * intra_decay
        o = jnp.einsum('bhqk,bhkv->bhqv', att, vi) + jnp.einsum('bhqd,bhdv->bhqv', qi, state) * read_decay
        state = state * chunk_decay + jnp.einsum('bhkd,bhkv->bhdv', ki * write_decay, vi)
        return state, o

    state, o = lax.scan(step, s0.astype(jnp.float32), (qc, kc, vc))
    o = o.transpose(1, 0, 3, 2, 4).reshape(B, S, H, DV)
    return o, state


def retention_branch(q, k, v, g, log_g, s0_f, s0_b):
    B, S, H, _ = q.shape
    o_f, s_f = retention_scan(q, k, v, log_g[0], s0_f)
    o_b, s_b = retention_scan(jnp.flip(q, 1), jnp.flip(k, 1), jnp.flip(v, 1), log_g[1], s0_b)
    o = head_layernorm(o_f + jnp.flip(o_b, 1)).reshape(B, S, H * v.shape[-1])
    out = (jax.nn.silu(g.astype(jnp.float32)) * o).astype(q.dtype)
    return out, s_f, s_b


def chunk_mlp_branch(u, v, norm_g, ws, bs):
    B, S, _ = v.shape
    n = S // CHUNK
    u = jax.nn.gelu(u)
    v = rmsnorm(jax.nn.gelu(v), norm_g)
    vc = v.reshape(B, n, CHUNK, MLP_GROUPS, MLP_GW)
    mixed = jnp.einsum('gpq,bnqgc->bnpgc', ws, vc) + bs.T[None, None, :, :, None]
    return u * mixed.reshape(B, S, MLP_WIDTH)


def diff_attention(q, k, v, lam_vec, subln_g, lam_init):
    B, Sq, H = q.shape[0], q.shape[1], q.shape[2]
    n = Sq // CHUNK
    lv = lam_vec.astype(jnp.float32)
    lam = jnp.exp(jnp.sum(lv[0] * lv[1])) - jnp.exp(jnp.sum(lv[2] * lv[3])) + lam_init
    scale = DIFF_DH ** -0.5
    qb = q.reshape(B, n, CHUNK, H, 2, DIFF_DH).swapaxes(0, 1)

    def block(qi):
        s = jnp.einsum('bqhcd,bkhcd->bhcqk', qi, k).astype(jnp.float32) * scale
        p = jax.nn.softmax(s, axis=-1)
        a = p[:, :, 0] - lam * p[:, :, 1]
        return jnp.einsum('bhqk,bkhv->bqhv', a.astype(v.dtype), v)

    o = lax.map(block, qb).swapaxes(0, 1).reshape(B, Sq, H, DIFF_DV)
    o = rmsnorm(o, subln_g) * (1.0 - lam_init)
    return o.reshape(B, Sq, H * DIFF_DV)


def trunk_layer(x, cond, layer_idx, p, ctx):
    B, S, _ = x.shape
    mod = (jax.nn.silu(cond) @ p['w_mod'] + p['b_mod']).reshape(cond.shape[0], 1, 6, D_MODEL)
    shift1, scale1, gate1 = mod[:, :, 0], mod[:, :, 1], mod[:, :, 2]
    shift2, scale2, gate2 = mod[:, :, 3], mod[:, :, 4], mod[:, :, 5]

    h = rmsnorm(x, p['norm_g'][0]) * (1.0 + scale1) + shift1
    rq, rk, rv, rg, mu, mv, dq, dk, dv, gates = split_cols(h @ p['w_in'])
    rq = rq.reshape(B, S, RET_HEADS, RET_DK)
    rk = rk.reshape(B, S, RET_HEADS, RET_DK)
    rv = rv.reshape(B, S, RET_HEADS, RET_DV)
    dq = dq.reshape(B, S, DIFF_HEADS, 2, DIFF_DH)
    dk = dk.reshape(B, S, DIFF_HEADS, 2, DIFF_DH)
    dv = dv.reshape(B, S, DIFF_HEADS, DIFF_DV)
    log_g = jax.nn.log_sigmoid(p['ret_decay_logit'].astype(jnp.float32))
    lam_init = 0.8 - 0.6 * math.exp(-0.3 * layer_idx)

    if ctx is None:
        zero = jnp.zeros((B, RET_HEADS, RET_DK, RET_DV), jnp.float32)
        ret_o, s_f, s_b = retention_branch(rq, rk * RET_DK ** -0.5, rv, rg, log_g, zero, zero)
        diff_o = diff_attention(dq, dk, dv, p['diff_lambda'], p['diff_subln_g'], lam_init)
        ctx_out = (dk, dv, jnp.stack([s_f, s_b], axis=1).astype(x.dtype))
    else:
        cache_k_l, cache_v_l, state_l = ctx
        cos_r, sin_r = axial_rope_tables(S, RET_DK)
        cos_d, sin_d = axial_rope_tables(S, DIFF_DH)
        rq = apply_axial_rope(rq, cos_r, sin_r)
        rk = apply_axial_rope(rk, cos_r, sin_r)
        dq = apply_axial_rope(dq, cos_d, sin_d)
        dk = apply_axial_rope(dk, cos_d, sin_d)
        ret_o, _, _ = retention_branch(rq, rk * RET_DK ** -0.5, rv, rg, log_g, state_l[:, 0], state_l[:, 1])
        k_all = jnp.concatenate([cache_k_l.astype(dk.dtype), dk], axis=1)
        v_all = jnp.concatenate([cache_v_l.astype(dv.dtype), dv], axis=1)
        diff_o = diff_attention(dq, k_all, v_all, p['diff_lambda'], p['diff_subln_g'], lam_init)
        ctx_out = None

    mlp_o = chunk_mlp_branch(mu, mv, p['mlp_norm_g'], p['mlp_ws'], p['mlp_bs'])
    g_ret, g_mlp, g_diff = jnp.split(jax.nn.sigmoid(gates), 3, axis=-1)
    merged = (g_ret * (ret_o @ p['w_branch'][0]) + g_mlp * (mlp_o @ p['w_branch'][1])
              + g_diff * (diff_o @ p['w_branch'][2]))
    x = x + gate1 * rmsnorm(merged @ p['w_o'], p['norm_g'][1])

    h = rmsnorm(x, p['norm_g'][2]) * (1.0 + scale2) + shift2
    a, b = jnp.split(h @ p['w_up'], 2, axis=-1)
    f = (jax.nn.silu(a) * b) @ p['w_down']
    x = x + gate2 * rmsnorm(f, p['norm_g'][3])
    return x, ctx_out


def setup_inputs(seed: int = 0) -> dict:
    key = jax.random.key(seed)
    ks = jax.random.split(key, 21)
    f32 = jnp.float32

    def nrm(k, shape, s):
        return jax.random.normal(k, shape, f32) * s

    gam = 1.0 - 2.0 ** (-5.0 - jnp.arange(RET_HEADS, dtype=f32))
    base_logit = jnp.log(gam) - jnp.log1p(-gam)
    return {
        'x_prompt': nrm(ks[0], (BATCH, SEQ, D_MODEL), 1.0),
        'x_sample': nrm(ks[1], (DEC_BATCH, DEC_SEQ, D_MODEL), 1.0),
        'cache_k': nrm(ks[2], (DEC_BATCH, DEPTH, PAST_LEN, DIFF_HEADS, 2, DIFF_DH), 1.0),
        'cache_v': nrm(ks[3], (DEC_BATCH, DEPTH, PAST_LEN, DIFF_HEADS, DIFF_DV), 1.0),
        'state_ret': nrm(ks[4], (DEC_BATCH, DEPTH, 2, RET_HEADS, RET_DK, RET_DV), RET_DK ** -0.5),
        'c': nrm(ks[5], (DEC_BATCH, D_MODEL), 1.0),
        'c_ctx': nrm(ks[6], (D_MODEL,), 1.0),
        'w_mod': nrm(ks[7], (DEPTH, D_MODEL, 6 * D_MODEL), 0.5 * D_MODEL ** -0.5),
        'b_mod': nrm(ks[8], (DEPTH, 6 * D_MODEL), 0.01),
        'norm_g': 1.0 + nrm(ks[9], (DEPTH, 4, D_MODEL), 0.05),
        'w_in': nrm(ks[10], (DEPTH, D_MODEL, IN_WIDTH), D_MODEL ** -0.5),
        'ret_decay_logit': base_logit + nrm(ks[11], (DEPTH, 2, RET_HEADS), 0.1),
        'mlp_norm_g': 1.0 + nrm(ks[12], (DEPTH, MLP_WIDTH), 0.05),
        'mlp_ws': nrm(ks[13], (DEPTH, MLP_GROUPS, CHUNK, CHUNK), CHUNK ** -0.5),
        'mlp_bs': 1.0 + nrm(ks[14], (DEPTH, MLP_GROUPS, CHUNK), 0.05),
        'diff_lambda': nrm(ks[15], (DEPTH, 4, DIFF_DH), 0.1),
        'diff_subln_g': 1.0 + nrm(ks[16], (DEPTH, DIFF_DV), 0.05),
        'w_branch': nrm(ks[17], (DEPTH, 3, BRANCH_W, D_MODEL), BRANCH_W ** -0.5),
        'w_o': nrm(ks[18], (DEPTH, D_MODEL, D_MODEL), D_MODEL ** -0.5),
        'w_up': nrm(ks[19], (DEPTH, D_MODEL, 2 * FFN_HIDDEN), D_MODEL ** -0.5),
        'w_down': nrm(ks[20], (DEPTH, FFN_HIDDEN, D_MODEL), FFN_HIDDEN ** -0.5),
    }


def reference(x_prompt, x_sample, cache_k, cache_v, state_ret, c, c_ctx, w_mod, b_mod, norm_g, w_in,
              ret_decay_logit, mlp_norm_g, mlp_ws, mlp_bs, diff_lambda, diff_subln_g, w_branch, w_o,
              w_up, w_down):
    params = [dict(w_mod=w_mod[l], b_mod=b_mod[l], norm_g=norm_g[l], w_in=w_in[l],
                   ret_decay_logit=ret_decay_logit[l], mlp_norm_g=mlp_norm_g[l], mlp_ws=mlp_ws[l],
                   mlp_bs=mlp_bs[l], diff_lambda=diff_lambda[l], diff_subln_g=diff_subln_g[l],
                   w_branch=w_branch[l], w_o=w_o[l], w_up=w_up[l], w_down=w_down[l])
              for l in range(DEPTH)]

    h = x_prompt
    ks_out, vs_out, ss_out = [], [], []
    for l in range(DEPTH):
        h, (k_l, v_l, s_l) = trunk_layer(h, c_ctx[None, :], l, params[l], None)
        ks_out.append(k_l)
        vs_out.append(v_l)
        ss_out.append(s_l)
    y_prompt = h
    new_cache_k = jnp.stack(ks_out, axis=1)
    new_cache_v = jnp.stack(vs_out, axis=1)
    new_state_ret = jnp.stack(ss_out, axis=1)

    h = x_sample
    for l in range(DEPTH):
        h, _ = trunk_layer(h, c, l, params[l], (cache_k[:, l], cache_v[:, l], state_ret[:, l]))
    y_sample = h

    return (y_prompt, y_sample, new_cache_k, new_cache_v, new_state_ret)
```

```python
import functools
import math

import jax
import jax.numpy as jnp
from jax import lax
from jax.experimental import pallas as pl
from jax.experimental.pallas import tpu as pltpu

F32 = jnp.float32
BF16 = jnp.bfloat16

HEADS = 8
DK = 128
DV = 256
CHUNK = 128
BRANCH_W = HEADS * DV
GRID_W = 64
ROPE_BASE = 10000.0
EPS = 1e-6
N_COND = 16
MIB = 1 << 20


def _pick(n, prefs):
    for p in prefs:
        if n % p == 0:
            return p
    raise ValueError(f"no block size in {prefs} divides {n}")


def _cparams(dims, vmem_mib):
    return pltpu.CompilerParams(dimension_semantics=dims, vmem_limit_bytes=vmem_mib * MIB)


def _silu(x):
    return x * jax.nn.sigmoid(x)


def _gelu_tanh(x):
    return 0.5 * x * (1.0 + jnp.tanh(math.sqrt(2.0 / math.pi) * (x + 0.044715 * (x * x * x))))


def _rms(x):
    return x * lax.rsqrt(jnp.mean(x * x, axis=-1, keepdims=True) + EPS)


def _mod_kernel(c_ref, w_ref, b_ref, o_ref):
    s = _silu(c_ref[...]).astype(BF16)
    o_ref[...] = jnp.dot(s, w_ref[...].astype(BF16), preferred_element_type=F32) + b_ref[...]


def _modulation(cond, w_mod, b_mod3, layer):
    n_cond, d = cond.shape
    bn = _pick(d, (512, 256, 128))
    per_seg = d // bn
    return pl.pallas_call(
        _mod_kernel,
        grid=(6 * per_seg,),
        in_specs=[
            pl.BlockSpec((n_cond, d), lambda j: (0, 0)),
            pl.BlockSpec((None, d, bn), lambda j: (layer, 0, j)),
            pl.BlockSpec((None, 1, bn), lambda j: (layer, 0, j)),
        ],
        out_specs=pl.BlockSpec((None, n_cond, bn), lambda j: (j // per_seg, 0, j % per_seg)),
        out_shape=jax.ShapeDtypeStruct((6, n_cond, d), F32),
        compiler_params=_cparams(("arbitrary",), 40),
        name="modulation",
    )(cond, w_mod, b_mod3)


def _norm_mod_kernel(x_ref, ng_ref, mod_ref, h_ref, *, gn, shift, scale):
    hn = _rms(x_ref[...]) * ng_ref[gn:gn + 1, :]
    h_ref[...] = (hn * (1.0 + mod_ref[scale:scale + 1, :]) + mod_ref[shift:shift + 1, :]).astype(BF16)


def _resid_norm_kernel(x_ref, y_ref, nga_ref, moda_ref, ngb_ref, modb_ref, xo_ref, h_ref, *,
                       gy, gate, gn, shift, scale):
    yn = _rms(y_ref[...]) * nga_ref[gy:gy + 1, :]
    xn = x_ref[...] + moda_ref[gate:gate + 1, :] * yn
    xo_ref[...] = xn
    hn = _rms(xn) * ngb_ref[gn:gn + 1, :]
    h_ref[...] = (hn * (1.0 + modb_ref[scale:scale + 1, :]) + modb_ref[shift:shift + 1, :]).astype(BF16)


def _resid_kernel(x_ref, y_ref, nga_ref, moda_ref, xo_ref, *, gy, gate):
    yn = _rms(y_ref[...]) * nga_ref[gy:gy + 1, :]
    xo_ref[...] = x_ref[...] + moda_ref[gate:gate + 1, :] * yn


class _Tokens:
    def __init__(self, n_ctx, n_dec_seq, dec_seq):
        self.n_ctx = n_ctx
        self.n_dec_seq = n_dec_seq
        self.dec_seq = dec_seq
        self.total = n_ctx + n_dec_seq * dec_seq
        self.br = _pick(math.gcd(n_ctx, dec_seq), (256, 128))

    def cond_index(self, i):
        nc = self.n_ctx // self.br
        per = self.dec_seq // self.br
        return jnp.where(i < nc, 0, 1 + (i - nc) // per)


def _row_spec(tok, d):
    return pl.BlockSpec((tok.br, d), lambda i: (i, 0))


def _ng_spec(layer, d):
    return pl.BlockSpec((None, 4, d), lambda i: (layer, 0, 0))


def _mod_spec(tok, d):
    return pl.BlockSpec((None, 6, d), lambda i: (tok.cond_index(i), 0, 0))


def _norm_mod(tok, x, norm_g, layer, mod):
    d = x.shape[1]
    return pl.pallas_call(
        functools.partial(_norm_mod_kernel, gn=0, shift=0, scale=1),
        grid=(tok.total // tok.br,),
        in_specs=[_row_spec(tok, d), _ng_spec(layer, d), _mod_spec(tok, d)],
        out_specs=_row_spec(tok, d),
        out_shape=jax.ShapeDtypeStruct(x.shape, BF16),
        compiler_params=_cparams(("arbitrary",), 32),
        name="norm_mod",
    )(x, norm_g, mod)


def _resid_norm(tok, x, y, norm_g, layer_a, mod_a, layer_b, mod_b, *, gy, gate, gn, shift, scale):
    d = x.shape[1]
    return pl.pallas_call(
        functools.partial(_resid_norm_kernel, gy=gy, gate=gate, gn=gn, shift=shift, scale=scale),
        grid=(tok.total // tok.br,),
        in_specs=[_row_spec(tok, d), _row_spec(tok, d), _ng_spec(layer_a, d), _mod_spec(tok, d),
                  _ng_spec(layer_b, d), _mod_spec(tok, d)],
        out_specs=[_row_spec(tok, d), _row_spec(tok, d)],
        out_shape=[jax.ShapeDtypeStruct(x.shape, F32), jax.ShapeDtypeStruct(x.shape, BF16)],
        compiler_params=_cparams(("arbitrary",), 40),
        name="resid_norm",
    )(x, y, norm_g, mod_a, norm_g, mod_b)


def _resid(tok, x, y, norm_g, layer, mod, *, gy, gate):
    d = x.shape[1]
    return pl.pallas_call(
        functools.partial(_resid_kernel, gy=gy, gate=gate),
        grid=(tok.total // tok.br,),
        in_specs=[_row_spec(tok, d), _row_spec(tok, d), _ng_spec(layer, d), _mod_spec(tok, d)],
        out_specs=_row_spec(tok, d),
        out_shape=jax.ShapeDtypeStruct(x.shape, F32),
        compiler_params=_cparams(("arbitrary",), 40),
        name="resid",
    )(x, y, norm_g, mod)


def _mm_kernel(a_ref, w_ref, o_ref, *, nk):
    p = jnp.dot(a_ref[...], w_ref[...], preferred_element_type=F32)
    if nk == 1:
        o_ref[...] = p.astype(o_ref.dtype)
    else:
        k = pl.program_id(2)

        @pl.when(k == 0)
        def _():
            o_ref[...] = p

        @pl.when(k > 0)
        def _():
            o_ref[...] += p


def _matmul(a, w, layer, *, row0, nrows, wcol, ncols, out_dtype, bm, bn, nk=1, name):
    kdim = a.shape[1]
    bk = kdim // nk
    assert nk == 1 or out_dtype == F32
    r0 = row0 // bm
    return pl.pallas_call(
        functools.partial(_mm_kernel, nk=nk),
        grid=(ncols // bn, nrows // bm, nk),
        in_specs=[
            pl.BlockSpec((bm, bk), lambda j, i, k: (r0 + i, k)),
            pl.BlockSpec((None, bk, bn), lambda j, i, k: (layer, k, wcol(j))),
        ],
        out_specs=pl.BlockSpec((bm, bn), lambda j, i, k: (i, j)),
        out_shape=jax.ShapeDtypeStruct((nrows, ncols), out_dtype),
        compiler_params=_cparams(("arbitrary", "arbitrary", "arbitrary"), 56),
        name=name,
    )(a, w)


def _swiglu_kernel(a_ref, wa_ref, wb_ref, o_ref):
    x = a_ref[...]
    ga = jnp.dot(x, wa_ref[...], preferred_element_type=F32)
    gb = jnp.dot(x, wb_ref[...], preferred_element_type=F32)
    o_ref[...] = (_silu(ga) * gb).astype(BF16)


def _swiglu_up(h, wa, wb, layer, *, bm, bn):
    m, d = h.shape
    hid = wa.shape[2]
    return pl.pallas_call(
        _swiglu_kernel,
        grid=(hid // bn, m // bm),
        in_specs=[
            pl.BlockSpec((bm, d), lambda j, i: (i, 0)),
            pl.BlockSpec((None, d, bn), lambda j, i: (layer, 0, j)),
            pl.BlockSpec((None, d, bn), lambda j, i: (layer, 0, j)),
        ],
        out_specs=pl.BlockSpec((bm, bn), lambda j, i: (i, j)),
        out_shape=jax.ShapeDtypeStruct((m, hid), BF16),
        compiler_params=_cparams(("arbitrary", "arbitrary"), 56),
        name="swiglu_up",
    )(h, wa, wb)


def _merge_kernel(r_ref, m_ref, d_ref, w_ref, g0_ref, g1_ref, g2_ref, o_ref):
    acc = jax.nn.sigmoid(g0_ref[...].astype(F32)) * jnp.dot(r_ref[...], w_ref[0], preferred_element_type=F32)
    acc += jax.nn.sigmoid(g1_ref[...].astype(F32)) * jnp.dot(m_ref[...], w_ref[1], preferred_element_type=F32)
    acc += jax.nn.sigmoid(g2_ref[...].astype(F32)) * jnp.dot(d_ref[...], w_ref[2], preferred_element_type=F32)
    o_ref[...] = acc.astype(BF16)


def _merge(ret_o, mlp_o, diff_o, w_branch, layer, z, gate_col0, d, *, bm, bn):
    m = ret_o.shape[0]
    branch = pl.BlockSpec((bm, BRANCH_W), lambda j, i: (i, 0))

    def gate_spec(t):
        off = (gate_col0 + t * d) // bn
        return pl.BlockSpec((bm, bn), lambda j, i: (i, off + j))

    return pl.pallas_call(
        _merge_kernel,
        grid=(d // bn, m // bm),
        in_specs=[branch, branch, branch,
                  pl.BlockSpec((None, 3, BRANCH_W, bn), lambda j, i: (layer, 0, 0, j)),
                  gate_spec(0), gate_spec(1), gate_spec(2)],
        out_specs=pl.BlockSpec((bm, bn), lambda j, i: (i, j)),
        out_shape=jax.ShapeDtypeStruct((m, d), BF16),
        compiler_params=_cparams(("arbitrary", "arbitrary"), 56),
        name="merge",
    )(ret_o, mlp_o, diff_o, w_branch, z, z, z)


def _rope_tables(n_tok):
    n_rows = n_tok // GRID_W
    rows = jnp.repeat(jnp.arange(n_rows, dtype=F32), GRID_W)
    cols = jnp.tile(jnp.arange(GRID_W, dtype=F32), n_rows)
    axis_dim = DK // 2
    inv = ROPE_BASE ** (-jnp.arange(0, axis_dim, 2, dtype=F32) / axis_dim)
    ang = jnp.stack([rows[:, None] * inv, cols[:, None] * inv], axis=1)
    cos, sin = jnp.cos(ang), jnp.sin(ang)
    zero = jnp.zeros_like(sin)
    cos_t = jnp.concatenate([cos, cos], axis=-1).reshape(n_tok, DK)
    sin_lo = jnp.concatenate([-sin, zero], axis=-1).reshape(n_tok, DK)
    sin_hi = jnp.concatenate([zero, sin], axis=-1).reshape(n_tok, DK)
    return jnp.stack([cos_t, sin_lo, sin_hi], axis=0)


def _rope(x, cos_t, sin_lo, sin_hi):
    return x * cos_t + pltpu.roll(x, DK - 32, 1) * sin_lo + pltpu.roll(x, 32, 1) * sin_hi


def _log_sigmoid(x):
    return jnp.minimum(x, 0.0) - jnp.log(1.0 + jnp.exp(-jnp.abs(x)))


def _retention_kernel(*refs, n_chunks, hg, rope, has_s0, emit_state):
    it = iter(refs)
    q_ref, k_ref, v_ref, g_ref, logit_ref = next(it), next(it), next(it), next(it), next(it)
    rope_ref = next(it) if rope else None
    s0_ref = next(it) if has_s0 else None
    o_ref = next(it)
    st_ref = next(it) if emit_state else None
    ks_scr, sb_scr = next(it), next(it)

    row = lax.broadcasted_iota(jnp.int32, (CHUNK, CHUNK), 0).astype(F32)
    col = lax.broadcasted_iota(jnp.int32, (CHUNK, CHUNK), 1).astype(F32)
    rel = row - col
    row_v = lax.broadcasted_iota(jnp.int32, (CHUNK, DV), 0).astype(F32)
    k_scale = DK ** -0.5

    def rows(c):
        if isinstance(c, int):
            return pl.ds(c * CHUNK, CHUNK)
        return pl.ds(pl.multiple_of(c * CHUNK, CHUNK), CHUNK)

    def maybe_rope(x, r):
        if not rope:
            return x
        return _rope(x, rope_ref[0, r, :], rope_ref[1, r, :], rope_ref[2, r, :])

    def loop(lo, hi, body, init, reverse=False):
        if n_chunks <= 4:
            carry = init
            for c in (range(hi - 1, lo - 1, -1) if reverse else range(lo, hi)):
                carry = body(c, carry)
            return carry
        if reverse:
            return lax.fori_loop(lo, hi, lambda t, cr: body(hi - 1 - t + lo, cr), init)
        return lax.fori_loop(lo, hi, body, init)

    for hh in range(hg):
        kq = slice(hh * DK, (hh + 1) * DK)
        vq = slice(hh * DV, (hh + 1) * DV)
        lg_f = _log_sigmoid(logit_ref[0, hh])[0:1, :]
        lg_b = _log_sigmoid(logit_ref[1, hh])[0:1, :]
        lg_fv = jnp.concatenate([lg_f, lg_f], axis=1)
        lg_bv = jnp.concatenate([lg_b, lg_b], axis=1)
        decay = (jnp.where(rel >= 0, jnp.exp(lg_f * jnp.maximum(rel, 0.0)), 0.0)
                 + jnp.where(rel <= 0, jnp.exp(lg_b * jnp.maximum(-rel, 0.0)), 0.0))
        read_f = jnp.exp(lg_fv * (row_v + 1.0))
        read_b = jnp.exp(lg_bv * (CHUNK - row_v))
        write_f = jnp.exp(lg_f * (CHUNK - 1.0 - row))
        write_b = jnp.exp(lg_b * row)
        cd_f = jnp.exp(lg_fv * float(CHUNK))
        cd_b = jnp.exp(lg_bv * float(CHUNK))

        if has_s0:
            s0_f, s0_b = s0_ref[0, hh], s0_ref[1, hh]
        else:
            s0_f = s0_b = jnp.zeros((DK, DV), F32)

        def bwd_body(c, s_b):
            r = rows(c)
            ks = maybe_rope(k_ref[r, kq].astype(F32), r) * k_scale
            ks_scr[r, :] = ks.astype(BF16)
            sb_scr[c] = s_b.astype(BF16)
            kw = (ks * write_b).astype(BF16)
            return s_b * cd_b + lax.dot_general(kw, v_ref[r, vq], (((0,), (0,)), ((), ())),
                                                preferred_element_type=F32)

        s_b = loop(0, n_chunks, bwd_body, s0_b, reverse=True)

        def fwd_body(c, s_f):
            r = rows(c)
            q = maybe_rope(q_ref[r, kq].astype(F32), r).astype(BF16)
            ks = ks_scr[r, :]
            v = v_ref[r, vq]
            att = lax.dot_general(q, ks, (((1,), (1,)), ((), ())), preferred_element_type=F32) * decay
            o = jnp.dot(att.astype(BF16), v, preferred_element_type=F32)
            o += jnp.dot(q, s_f.astype(BF16), preferred_element_type=F32) * read_f
            o += jnp.dot(q, sb_scr[c], preferred_element_type=F32) * read_b
            oc = o - jnp.mean(o, axis=-1, keepdims=True)
            on = oc * lax.rsqrt(jnp.mean(oc * oc, axis=-1, keepdims=True) + EPS)
            o_ref[r, vq] = (_silu(g_ref[r, vq].astype(F32)) * on).astype(BF16)
            kw = (ks.astype(F32) * write_f).astype(BF16)
            return s_f * cd_f + lax.dot_general(kw, v, (((0,), (0,)), ((), ())), preferred_element_type=F32)

        s_f = loop(0, n_chunks, fwd_body, s0_f)
        if emit_state:
            st_ref[0, hh] = s_f
            st_ref[1, hh] = s_b


def _retention(z, logit_b, layer, *, row0, n_seq, seq, hg, rope_tab=None, state_in=None, emit_state=False):
    n_chunks = seq // CHUNK
    sb = row0 // seq
    kw, vw = hg * DK, hg * DV
    in_specs = [
        pl.BlockSpec((seq, kw), lambda b, h: (sb + b, h)),
        pl.BlockSpec((seq, kw), lambda b, h: (sb + b, HEADS * DK // kw + h)),
        pl.BlockSpec((seq, vw), lambda b, h: (sb + b, 2 * HEADS * DK // vw + h)),
        pl.BlockSpec((seq, vw), lambda b, h: (sb + b, (2 * HEADS * DK + BRANCH_W) // vw + h)),
        pl.BlockSpec((None, 2, hg, 8, 128), lambda b, h: (layer, 0, h, 0, 0)),
    ]
    args = [z, z, z, z, logit_b]
    if rope_tab is not None:
        in_specs.append(pl.BlockSpec((3, seq, DK), lambda b, h: (0, 0, 0)))
        args.append(rope_tab)
    if state_in is not None:
        in_specs.append(pl.BlockSpec((None, None, 2, hg, DK, DV), lambda b, h: (b, layer, 0, h, 0, 0)))
        args.append(state_in)
    out_specs = [pl.BlockSpec((seq, vw), lambda b, h: (b, h))]
    out_shape = [jax.ShapeDtypeStruct((n_seq * seq, BRANCH_W), BF16)]
    if emit_state:
        out_specs.append(pl.BlockSpec((None, 2, hg, DK, DV), lambda b, h: (b, 0, h, 0, 0)))
        out_shape.append(jax.ShapeDtypeStruct((n_seq, 2, HEADS, DK, DV), F32))
    return pl.pallas_call(
        functools.partial(_retention_kernel, n_chunks=n_chunks, hg=hg, rope=rope_tab is not None,
                          has_s0=state_in is not None, emit_state=emit_state),
        grid=(n_seq, HEADS // hg),
        in_specs=in_specs,
        out_specs=out_specs,
        out_shape=out_shape,
        scratch_shapes=[pltpu.VMEM((seq, DK), BF16), pltpu.VMEM((n_chunks, DK, DV), BF16)],
        compiler_params=_cparams(("arbitrary", "arbitrary"), 48),
        name="retention",
    )(*args)


def _gmlp_kernel(u_ref, v_ref, ng_ref, ws_ref, bs_ref, o_ref, *, n_chunks):
    vn = (_rms(_gelu_tanh(v_ref[...].astype(F32))) * ng_ref[...]).astype(BF16)
    for t in range(n_chunks):
        r = slice(t * CHUNK, (t + 1) * CHUNK)
        for g in range(HEADS):
            cs = slice(g * DV, (g + 1) * DV)
            mixed = jnp.dot(ws_ref[g].astype(BF16), vn[r, cs], preferred_element_type=F32) + bs_ref[g]
            o_ref[r, cs] = (_gelu_tanh(u_ref[r, cs].astype(F32)) * mixed).astype(BF16)


def _gmlp(z, mlp_norm_g3, mlp_ws, bs_b, layer, *, u_col0):
    m = z.shape[0]
    bt = _pick(m, (512, 256, 128))
    ub = u_col0 // BRANCH_W
    return pl.pallas_call(
        functools.partial(_gmlp_kernel, n_chunks=bt // CHUNK),
        grid=(m // bt,),
        in_specs=[
            pl.BlockSpec((bt, BRANCH_W), lambda i: (i, ub)),
            pl.BlockSpec((bt, BRANCH_W), lambda i: (i, ub + 1)),
            pl.BlockSpec((None, 1, BRANCH_W), lambda i: (layer, 0, 0)),
            pl.BlockSpec((None, HEADS, CHUNK, CHUNK), lambda i: (layer, 0, 0, 0)),
            pl.BlockSpec((None, HEADS, CHUNK, DV), lambda i: (layer, 0, 0, 0)),
        ],
        out_specs=pl.BlockSpec((bt, BRANCH_W), lambda i: (i, 0)),
        out_shape=jax.ShapeDtypeStruct((m, BRANCH_W), BF16),
        compiler_params=_cparams(("arbitrary",), 40),
        name="gmlp",
    )(z, z, mlp_norm_g3, mlp_ws, bs_b)


def _diff_attn_kernel(*refs, tq, n_new, n_cache, rope, lam_init):
    it = iter(refs)
    q_ref, kn_ref, vn_ref = next(it), next(it), next(it)
    kc_ref, vc_ref = (next(it), next(it)) if n_cache else (None, None)
    rope_ref = next(it) if rope else None
    lam_ref, sg_ref = next(it), next(it)
    o_ref = next(it)
    k_scr, v_scr = next(it), next(it)
    qi = pl.program_id(2)

    def maybe_rope(x, r):
        if not rope:
            return x
        return _rope(x, rope_ref[0, r, :], rope_ref[1, r, :], rope_ref[2, r, :])

    @pl.when(qi == 0)
    def _():
        if n_cache:
            k_scr[0:n_cache, :] = kc_ref[...].astype(BF16)
            v_scr[0:n_cache, :] = vc_ref[...].astype(BF16)
        for c in range(2):
            cs = slice(c * DK, (c + 1) * DK)
            k_scr[n_cache:n_cache + n_new, cs] = maybe_rope(
                kn_ref[:, cs].astype(F32), slice(0, n_new)).astype(BF16)
        v_scr[n_cache:n_cache + n_new, :] = vn_ref[...].astype(BF16)

    lv = lam_ref[...]
    lam = (jnp.exp(jnp.sum(lv[0:1] * lv[1:2], axis=-1, keepdims=True))
           - jnp.exp(jnp.sum(lv[2:3] * lv[3:4], axis=-1, keepdims=True)) + lam_init)
    scale = DK ** -0.5
    qrows = pl.ds(pl.multiple_of(qi * tq, tq), tq)
    es, invs = [], []
    for c in range(2):
        cs = slice(c * DK, (c + 1) * DK)
        q = maybe_rope(q_ref[:, cs].astype(F32), qrows).astype(BF16)
        s = lax.dot_general(q, k_scr[:, cs], (((1,), (1,)), ((), ())), preferred_element_type=F32) * scale
        e = jnp.exp(s - jnp.max(s, axis=-1, keepdims=True))
        es.append(e)
        invs.append(1.0 / jnp.sum(e, axis=-1, keepdims=True))
    a = (es[0] * invs[0] - es[1] * (lam * invs[1])).astype(BF16)
    o = jnp.dot(a, v_scr[...], preferred_element_type=F32)
    o_ref[...] = (_rms(o) * sg_ref[...] * (1.0 - lam_init)).astype(BF16)


def _diff_attn(q_arr, q_col0, q_row0, kv_arr, layer, diff_lambda, subln_g3, lam_init, *, n_seq, seq,
               cache_k=None, cache_v=None, rope_tab=None):
    n_cache = 0 if cache_k is None else cache_k.shape[2]
    tq = _pick(seq, (256, 128))
    nq = seq // tq
    qb0, qc0 = q_row0 // tq, q_col0 // DV
    in_specs = [
        pl.BlockSpec((tq, DV), lambda b, h, i: (qb0 + b * nq + i, qc0 + h)),
        pl.BlockSpec((seq, DV), lambda b, h, i: (b, h)),
        pl.BlockSpec((seq, DV), lambda b, h, i: (b, HEADS + h)),
    ]
    args = [q_arr, kv_arr, kv_arr]
    if n_cache:
        in_specs += [pl.BlockSpec((None, None, n_cache, DV), lambda b, h, i: (b, layer, 0, h))] * 2
        args += [cache_k, cache_v]
    if rope_tab is not None:
        in_specs.append(pl.BlockSpec((3, seq, DK), lambda b, h, i: (0, 0, 0)))
        args.append(rope_tab)
    in_specs += [pl.BlockSpec((None, 4, DK), lambda b, h, i: (layer, 0, 0)),
                 pl.BlockSpec((None, 1, DV), lambda b, h, i: (layer, 0, 0))]
    args += [diff_lambda, subln_g3]
    return pl.pallas_call(
        functools.partial(_diff_attn_kernel, tq=tq, n_new=seq, n_cache=n_cache, rope=rope_tab is not None,
                          lam_init=lam_init),
        grid=(n_seq, HEADS, nq),
        in_specs=in_specs,
        out_specs=pl.BlockSpec((tq, DV), lambda b, h, i: (b * nq + i, h)),
        out_shape=jax.ShapeDtypeStruct((n_seq * seq, BRANCH_W), BF16),
        scratch_shapes=[pltpu.VMEM((n_cache + seq, 2 * DK), BF16), pltpu.VMEM((n_cache + seq, DV), BF16)],
        compiler_params=_cparams(("arbitrary", "arbitrary", "arbitrary"), 56),
        name="diff_attn",
    )(*args)


def kernel(x_prompt, x_sample, cache_k, cache_v, state_ret, c, c_ctx, w_mod, b_mod, norm_g, w_in,
           ret_decay_logit, mlp_norm_g, mlp_ws, mlp_bs, diff_lambda, diff_subln_g, w_branch, w_o,
           w_up, w_down):
    batch, seq, d = x_prompt.shape
    dec_batch, dec_seq, _ = x_sample.shape
    depth = w_in.shape[0]
    hidden = w_down.shape[1]
    past = cache_k.shape[2]
    n_ctx = batch * seq
    n_lat = dec_batch * dec_seq
    tok = _Tokens(n_ctx, dec_batch, dec_seq)
    m = tok.total

    c_rq, c_rk, c_rv, c_rg = 0, HEADS * DK, 2 * HEADS * DK, 2 * HEADS * DK + BRANCH_W
    c_mu = c_rg + BRANCH_W
    c_dq = c_mu + 2 * BRANCH_W
    c_dk = c_dq + BRANCH_W
    c_gates = c_dk + 2 * BRANCH_W
    n_main = c_dk + 3 * d

    bm = _pick(math.gcd(n_ctx, n_lat), (1024, 512, 256, 128))
    bn = _pick(math.gcd(d, BRANCH_W), (1024, 512, 256, 128))
    hid_pad = -(-hidden // 1024) * 1024

    w_in_b = w_in.astype(BF16)
    w_branch_b = w_branch.astype(BF16)
    w_o_b = w_o.astype(BF16)
    pad_h = ((0, 0), (0, 0), (0, hid_pad - hidden))
    w_up_a = jnp.pad(w_up[:, :, :hidden].astype(BF16), pad_h)
    w_up_b = jnp.pad(w_up[:, :, hidden:].astype(BF16), pad_h)
    w_down_b = jnp.pad(w_down.astype(BF16), ((0, 0), (0, hid_pad - hidden), (0, 0)))

    cond = jnp.zeros((N_COND, d), F32).at[0].set(c_ctx).at[1:1 + dec_batch].set(c)
    b_mod3 = b_mod.reshape(depth, 1, 6 * d)
    logit_b = jnp.broadcast_to(ret_decay_logit[:, :, :, None, None], (depth, 2, HEADS, 8, 128))
    bs_b = jnp.broadcast_to(mlp_bs[:, :, :, None], (depth, HEADS, CHUNK, DV))
    mlp_norm_g3 = mlp_norm_g.reshape(depth, 1, BRANCH_W)
    subln_g3 = diff_subln_g.reshape(depth, 1, DV)
    cache_k4 = cache_k.reshape(dec_batch, depth, past, BRANCH_W)
    cache_v4 = cache_v.reshape(dec_batch, depth, past, BRANCH_W)
    rope_tab = _rope_tables(dec_seq)

    mods = [jnp.transpose(_modulation(cond, w_mod, b_mod3, l), (1, 0, 2)) for l in range(depth)]
    x = jnp.concatenate([x_prompt.reshape(n_ctx, d), x_sample.reshape(n_lat, d)], axis=0)
    h = _norm_mod(tok, x, norm_g, 0, mods[0])

    ks_out, vs_out, ss_out = [], [], []
    main_split = c_dk // bn
    for l in range(depth):
        lam_init = 0.8 - 0.6 * math.exp(-0.3 * l)
        z = _matmul(h, w_in_b, l, row0=0, nrows=m, ncols=n_main, out_dtype=BF16, bm=bm, bn=bn,
                    wcol=lambda j: jnp.where(j < main_split, j, j + 2 * BRANCH_W // bn), name="w_in_main")
        kv_ctx = _matmul(h, w_in_b, l, row0=0, nrows=n_ctx, ncols=2 * BRANCH_W, out_dtype=F32, bm=bm, bn=bn,
                         wcol=lambda j: main_split + j, name="w_in_kv_ctx")
        kv_lat = _matmul(h, w_in_b, l, row0=n_ctx, nrows=n_lat, ncols=2 * BRANCH_W, out_dtype=BF16, bm=bm,
                         bn=bn, wcol=lambda j: main_split + j, name="w_in_kv_lat")
        ks_out.append(kv_ctx[:, :BRANCH_W].reshape(batch, seq, HEADS, 2, DK))
        vs_out.append(kv_ctx[:, BRANCH_W:].reshape(batch, seq, HEADS, DV))

        ret_ctx, st = _retention(z, logit_b, l, row0=0, n_seq=batch, seq=seq, hg=HEADS, emit_state=True)
        ss_out.append(st)
        ret_lat, = _retention(z, logit_b, l, row0=n_ctx, n_seq=dec_batch, seq=dec_seq, hg=1,
                              rope_tab=rope_tab, state_in=state_ret)
        ret_o = jnp.concatenate([ret_ctx, ret_lat], axis=0)

        mlp_o = _gmlp(z, mlp_norm_g3, mlp_ws, bs_b, l, u_col0=c_mu)

        diff_ctx = _diff_attn(z, c_dq, 0, kv_ctx, l, diff_lambda, subln_g3, lam_init, n_seq=batch, seq=seq)
        diff_lat = _diff_attn(z, c_dq, n_ctx, kv_lat, l, diff_lambda, subln_g3, lam_init, n_seq=dec_batch,
                              seq=dec_seq, cache_k=cache_k4, cache_v=cache_v4, rope_tab=rope_tab)
        diff_o = jnp.concatenate([diff_ctx, diff_lat], axis=0)

        merged = _merge(ret_o, mlp_o, diff_o, w_branch_b, l, z, c_dk, d, bm=_pick(m, (512, 256, 128)), bn=bn)
        y = _matmul(merged, w_o_b, l, row0=0, nrows=m, ncols=d, out_dtype=F32, bm=bm, bn=bn,
                    wcol=lambda j: j, name="w_o")
        x, h2 = _resid_norm(tok, x, y, norm_g, l, mods[l], l, mods[l], gy=1, gate=2, gn=2, shift=3, scale=4)

        act = _swiglu_up(h2, w_up_a, w_up_b, l, bm=bm, bn=_pick(hid_pad, (512, 256)))
        f = _matmul(act, w_down_b, l, row0=0, nrows=m, ncols=d, out_dtype=F32, bm=bm, bn=bn,
                    nk=4, wcol=lambda j: j, name="w_down")
        if l + 1 < depth:
            x, h = _resid_norm(tok, x, f, norm_g, l, mods[l], l + 1, mods[l + 1],
                               gy=3, gate=5, gn=0, shift=0, scale=1)
        else:
            x = _resid(tok, x, f, norm_g, l, mods[l], gy=3, gate=5)

    y_prompt = x[:n_ctx].reshape(batch, seq, d)
    y_sample = x[n_ctx:].reshape(dec_batch, dec_seq, d)
    return (y_prompt, y_sample, jnp.stack(ks_out, axis=1), jnp.stack(vs_out, axis=1),
            jnp.stack(ss_out, axis=1))
```

```python
import functools
import math

import jax
import jax.numpy as jnp
from jax import lax
from jax.experimental import pallas as pl
from jax.experimental.pallas import tpu as pltpu

F32 = jnp.float32
BF16 = jnp.bfloat16

HEADS = 8
DK = 128
DV = 256
CHUNK = 128
BRANCH_W = HEADS * DV
GRID_W = 64
ROPE_BASE = 10000.0
EPS = 1e-6
N_COND = 16
MIB = 1 << 20
LOG2E = 1.4426950408889634


def _pick(n, prefs):
    for p in prefs:
        if n % p == 0:
            return p
    raise ValueError(f"no block size in {prefs} divides {n}")


def _cparams(dims, vmem_mib):
    return pltpu.CompilerParams(dimension_semantics=dims, vmem_limit_bytes=vmem_mib * MIB)


def _silu(x):
    return x * jax.nn.sigmoid(x)


def _gelu_tanh(x):
    return 0.5 * x * (1.0 + jnp.tanh(math.sqrt(2.0 / math.pi) * (x + 0.044715 * (x * x * x))))


def _rms(x):
    return x * lax.rsqrt(jnp.mean(x * x, axis=-1, keepdims=True) + EPS)


def _mod_kernel(c_ref, w_ref, b_ref, o_ref):
    s = _silu(c_ref[...]).astype(BF16)
    o_ref[...] = jnp.dot(s, w_ref[...].astype(BF16), preferred_element_type=F32) + b_ref[...]


def _modulation(cond, w_mod, b_mod3, layer):
    n_cond, d = cond.shape
    bn = _pick(d, (512, 256, 128))
    per_seg = d // bn
    return pl.pallas_call(
        _mod_kernel,
        grid=(6 * per_seg,),
        in_specs=[
            pl.BlockSpec((n_cond, d), lambda j: (0, 0)),
            pl.BlockSpec((None, d, bn), lambda j: (layer, 0, j)),
            pl.BlockSpec((None, 1, bn), lambda j: (layer, 0, j)),
        ],
        out_specs=pl.BlockSpec((None, n_cond, bn), lambda j: (j // per_seg, 0, j % per_seg)),
        out_shape=jax.ShapeDtypeStruct((6, n_cond, d), F32),
        compiler_params=_cparams(("arbitrary",), 40),
        name="modulation",
    )(cond, w_mod, b_mod3)


class _Tokens:
    def __init__(self, n_ctx, n_dec_seq, dec_seq):
        self.n_ctx = n_ctx
        self.n_dec_seq = n_dec_seq
        self.dec_seq = dec_seq
        self.total = n_ctx + n_dec_seq * dec_seq
        self.br = _pick(math.gcd(n_ctx, dec_seq), (256, 128))
        self.ctx_blocks = n_ctx // self.br

    def cond_index(self, i):
        per = self.dec_seq // self.br
        return jnp.where(i < self.ctx_blocks, 0, 1 + (i - self.ctx_blocks) // per)

    def split_specs(self, d):
        nc = self.ctx_blocks
        return [pl.BlockSpec((self.br, d), lambda i: (jnp.minimum(i, nc - 1), 0)),
                pl.BlockSpec((self.br, d), lambda i: (jnp.maximum(i - nc, 0), 0))]


def _row_spec(tok, d):
    return pl.BlockSpec((tok.br, d), lambda i: (i, 0))


def _ng_spec(layer, d):
    return pl.BlockSpec((None, 4, d), lambda i: (layer, 0, 0))


def _mod_spec(tok, d):
    return pl.BlockSpec((None, 6, d), lambda i: (tok.cond_index(i), 0, 0))


def _token_kernel(*refs, ctx_blocks, split_in, split_out, has_y, has_h, gy, gate, gn, shift, scale):
    it = iter(refs)
    x_refs = [next(it), next(it)] if split_in else [next(it)]
    if has_y:
        y_ref, nga_ref, moda_ref = next(it), next(it), next(it)
    if has_h:
        ngb_ref, modb_ref = next(it), next(it)
    if has_y:
        xo_refs = [next(it), next(it)] if split_out else [next(it)]
    if has_h:
        h_ref = next(it)
    in_ctx = pl.program_id(0) < ctx_blocks

    x = jnp.where(in_ctx, x_refs[0][...], x_refs[1][...]) if split_in else x_refs[0][...]
    if has_y:
        x = x + moda_ref[gate:gate + 1, :] * (_rms(y_ref[...]) * nga_ref[gy:gy + 1, :])
        if split_out:
            @pl.when(in_ctx)
            def _():
                xo_refs[0][...] = x

            @pl.when(jnp.logical_not(in_ctx))
            def _():
                xo_refs[1][...] = x
        else:
            xo_refs[0][...] = x
    if has_h:
        hn = _rms(x) * ngb_ref[gn:gn + 1, :]
        h_ref[...] = (hn * (1.0 + modb_ref[scale:scale + 1, :]) + modb_ref[shift:shift + 1, :]).astype(BF16)


def _token_call(tok, x_parts, d, *, y=None, norm_g=None, layer_a=None, mod_a=None, layer_b=None, mod_b=None,
                split_out=False, gy=0, gate=0, gn=0, shift=0, scale=0):
    split_in = len(x_parts) == 2
    has_y, has_h = y is not None, mod_b is not None
    in_specs = tok.split_specs(d) if split_in else [_row_spec(tok, d)]
    args = list(x_parts)
    if has_y:
        in_specs += [_row_spec(tok, d), _ng_spec(layer_a, d), _mod_spec(tok, d)]
        args += [y, norm_g, mod_a]
    if has_h:
        in_specs += [_ng_spec(layer_b, d), _mod_spec(tok, d)]
        args += [norm_g, mod_b]
    out_specs, out_shape = [], []
    if has_y:
        if split_out:
            out_specs += tok.split_specs(d)
            out_shape += [jax.ShapeDtypeStruct((tok.n_ctx, d), F32),
                          jax.ShapeDtypeStruct((tok.total - tok.n_ctx, d), F32)]
        else:
            out_specs.append(_row_spec(tok, d))
            out_shape.append(jax.ShapeDtypeStruct((tok.total, d), F32))
    if has_h:
        out_specs.append(_row_spec(tok, d))
        out_shape.append(jax.ShapeDtypeStruct((tok.total, d), BF16))
    return pl.pallas_call(
        functools.partial(_token_kernel, ctx_blocks=tok.ctx_blocks, split_in=split_in, split_out=split_out,
                          has_y=has_y, has_h=has_h, gy=gy, gate=gate, gn=gn, shift=shift, scale=scale),
        grid=(tok.total // tok.br,),
        in_specs=in_specs,
        out_specs=out_specs,
        out_shape=out_shape,
        compiler_params=_cparams(("arbitrary",), 48),
        name="token_norm",
    )(*args)


def _mm_kernel(a_ref, w_ref, o_ref, *scratch, nk, cast):
    if cast:
        wb_ref, = scratch

        @pl.when(pl.program_id(1) == 0)
        def _():
            wb_ref[...] = w_ref[...].astype(BF16)
        w = wb_ref[...]
    else:
        w = w_ref[...]
    p = jnp.dot(a_ref[...], w, preferred_element_type=F32)
    if nk == 1:
        o_ref[...] = p.astype(o_ref.dtype)
    else:
        k = pl.program_id(2)

        @pl.when(k == 0)
        def _():
            o_ref[...] = p

        @pl.when(k > 0)
        def _():
            o_ref[...] += p


def _matmul(a, w, layer, *, row0, nrows, wcol, ncols, out_dtype, bm, bn, nk=1, name):
    kdim = a.shape[1]
    bk = kdim // nk
    cast = w.dtype != BF16
    assert nk == 1 or (out_dtype == F32 and not cast)
    r0 = row0 // bm
    return pl.pallas_call(
        functools.partial(_mm_kernel, nk=nk, cast=cast),
        grid=(ncols // bn, nrows // bm, nk),
        in_specs=[
            pl.BlockSpec((bm, bk), lambda j, i, k: (r0 + i, k)),
            pl.BlockSpec((None, bk, bn), lambda j, i, k: (layer, k, wcol(j))),
        ],
        out_specs=pl.BlockSpec((bm, bn), lambda j, i, k: (i, j)),
        out_shape=jax.ShapeDtypeStruct((nrows, ncols), out_dtype),
        scratch_shapes=[pltpu.VMEM((bk, bn), BF16)] if cast else [],
        compiler_params=_cparams(("arbitrary", "arbitrary", "arbitrary"), 56),
        name=name,
    )(a, w)


def _swiglu_kernel(a_ref, wa_ref, wb_ref, o_ref, w_scr, *, bn):
    @pl.when(pl.program_id(1) == 0)
    def _():
        w_scr[:, 0:bn] = wa_ref[...].astype(BF16)
        w_scr[:, bn:2 * bn] = wb_ref[...].astype(BF16)
    g = jnp.dot(a_ref[...], w_scr[...], preferred_element_type=F32)
    o_ref[...] = (_silu(g[:, 0:bn]) * g[:, bn:2 * bn]).astype(BF16)


def _swiglu_up(h, w_up, layer, hidden, *, bm, bn):
    m, d = h.shape
    half = hidden // bn
    return pl.pallas_call(
        functools.partial(_swiglu_kernel, bn=bn),
        grid=(half, m // bm),
        in_specs=[
            pl.BlockSpec((bm, d), lambda j, i: (i, 0)),
            pl.BlockSpec((None, d, bn), lambda j, i: (layer, 0, j)),
            pl.BlockSpec((None, d, bn), lambda j, i: (layer, 0, half + j)),
        ],
        out_specs=pl.BlockSpec((bm, bn), lambda j, i: (i, j)),
        out_shape=jax.ShapeDtypeStruct((m, hidden), BF16),
        scratch_shapes=[pltpu.VMEM((d, 2 * bn), BF16)],
        compiler_params=_cparams(("arbitrary", "arbitrary"), 56),
        name="swiglu_up",
    )(h, w_up, w_up)


def _merge_kernel(r_ref, m_ref, d_ref, w_ref, g0_ref, g1_ref, g2_ref, o_ref, w_scr):
    @pl.when(pl.program_id(1) == 0)
    def _():
        for t in range(3):
            w_scr[t] = w_ref[t].astype(BF16)
    acc = jax.nn.sigmoid(g0_ref[...].astype(F32)) * jnp.dot(r_ref[...], w_scr[0], preferred_element_type=F32)
    acc += jax.nn.sigmoid(g1_ref[...].astype(F32)) * jnp.dot(m_ref[...], w_scr[1], preferred_element_type=F32)
    acc += jax.nn.sigmoid(g2_ref[...].astype(F32)) * jnp.dot(d_ref[...], w_scr[2], preferred_element_type=F32)
    o_ref[...] = acc.astype(BF16)


def _merge(ret_o, mlp_o, diff_o, w_branch, layer, z, gate_col0, d, *, bm, bn):
    m = ret_o.shape[0]
    branch = pl.BlockSpec((bm, BRANCH_W), lambda j, i: (i, 0))

    def gate_spec(t):
        off = (gate_col0 + t * d) // bn
        return pl.BlockSpec((bm, bn), lambda j, i: (i, off + j))

    return pl.pallas_call(
        _merge_kernel,
        grid=(d // bn, m // bm),
        in_specs=[branch, branch, branch,
                  pl.BlockSpec((None, 3, BRANCH_W, bn), lambda j, i: (layer, 0, 0, j)),
                  gate_spec(0), gate_spec(1), gate_spec(2)],
        out_specs=pl.BlockSpec((bm, bn), lambda j, i: (i, j)),
        out_shape=jax.ShapeDtypeStruct((m, d), BF16),
        scratch_shapes=[pltpu.VMEM((3, BRANCH_W, bn), BF16)],
        compiler_params=_cparams(("arbitrary", "arbitrary"), 56),
        name="merge",
    )(ret_o, mlp_o, diff_o, w_branch, z, z, z)


def _rope_tables(n_tok):
    n_rows = n_tok // GRID_W
    rows = jnp.repeat(jnp.arange(n_rows, dtype=F32), GRID_W)
    cols = jnp.tile(jnp.arange(GRID_W, dtype=F32), n_rows)
    axis_dim = DK // 2
    inv = ROPE_BASE ** (-jnp.arange(0, axis_dim, 2, dtype=F32) / axis_dim)
    ang = jnp.stack([rows[:, None] * inv, cols[:, None] * inv], axis=1)
    cos, sin = jnp.cos(ang), jnp.sin(ang)
    zero = jnp.zeros_like(sin)
    cos_t = jnp.concatenate([cos, cos], axis=-1).reshape(n_tok, DK)
    sin_lo = jnp.concatenate([-sin, zero], axis=-1).reshape(n_tok, DK)
    sin_hi = jnp.concatenate([zero, sin], axis=-1).reshape(n_tok, DK)
    return jnp.stack([cos_t, sin_lo, sin_hi], axis=0)


def _rope(x, cos_t, sin_lo, sin_hi):
    return x * cos_t + pltpu.roll(x, DK - 32, 1) * sin_lo + pltpu.roll(x, 32, 1) * sin_hi


def _log_sigmoid(x):
    return jnp.minimum(x, 0.0) - jnp.log(1.0 + jnp.exp(-jnp.abs(x)))


def _retention_kernel(*refs, n_chunks, hg, rope, has_s0, emit_state, aliased):
    it = iter(refs)
    q_ref, k_ref, v_ref, g_ref, logit_ref = next(it), next(it), next(it), next(it), next(it)
    rope_ref = next(it) if rope else None
    s0_ref = next(it) if has_s0 else None
    if aliased:
        next(it)
    o_ref = next(it)
    st_ref = next(it) if emit_state else None
    ks_scr, sb_scr = next(it), next(it)

    row = lax.broadcasted_iota(jnp.int32, (CHUNK, CHUNK), 0).astype(F32)
    col = lax.broadcasted_iota(jnp.int32, (CHUNK, CHUNK), 1).astype(F32)
    rel = row - col
    row_v = lax.broadcasted_iota(jnp.int32, (CHUNK, DV), 0).astype(F32)
    k_scale = DK ** -0.5

    def rows(c):
        if isinstance(c, int):
            return pl.ds(c * CHUNK, CHUNK)
        return pl.ds(pl.multiple_of(c * CHUNK, CHUNK), CHUNK)

    def maybe_rope(x, r):
        if not rope:
            return x
        return _rope(x, rope_ref[0, r, :], rope_ref[1, r, :], rope_ref[2, r, :])

    def loop(lo, hi, body, init, reverse=False):
        if n_chunks <= 4:
            carry = init
            for c in (range(hi - 1, lo - 1, -1) if reverse else range(lo, hi)):
                carry = body(c, carry)
            return carry
        if reverse:
            return lax.fori_loop(lo, hi, lambda t, cr: body(hi - 1 - t + lo, cr), init)
        return lax.fori_loop(lo, hi, body, init)

    consts = []
    for hh in range(hg):
        lg_f = _log_sigmoid(logit_ref[0, hh])[0:1, :]
        lg_b = _log_sigmoid(logit_ref[1, hh])[0:1, :]
        lg_fv = jnp.concatenate([lg_f, lg_f], axis=1)
        lg_bv = jnp.concatenate([lg_b, lg_b], axis=1)
        consts.append(dict(
            decay=(jnp.where(rel >= 0, jnp.exp(lg_f * jnp.maximum(rel, 0.0)), 0.0)
                   + jnp.where(rel <= 0, jnp.exp(lg_b * jnp.maximum(-rel, 0.0)), 0.0)),
            read_f=jnp.exp(lg_fv * (row_v + 1.0)),
            read_b=jnp.exp(lg_bv * (CHUNK - row_v)),
            write_f=jnp.exp(lg_f * (CHUNK - 1.0 - row)),
            write_b=jnp.exp(lg_b * row),
            cd_f=jnp.exp(lg_fv * float(CHUNK)),
            cd_b=jnp.exp(lg_bv * float(CHUNK)),
        ))
    if has_s0:
        s0_f = tuple(s0_ref[0, hh] for hh in range(hg))
        s0_b = tuple(s0_ref[1, hh] for hh in range(hg))
    else:
        s0_f = s0_b = tuple(jnp.zeros((DK, DV), F32) for _ in range(hg))

    def bwd_body(c, s_b):
        r = rows(c)
        out = []
        for hh, cst in enumerate(consts):
            kq = slice(hh * DK, (hh + 1) * DK)
            vq = slice(hh * DV, (hh + 1) * DV)
            ks = maybe_rope(k_ref[r, kq].astype(F32), r) * k_scale
            ks_scr[r, kq] = ks.astype(BF16)
            sb_scr[hh, c] = s_b[hh].astype(BF16)
            kw = (ks * cst["write_b"]).astype(BF16)
            out.append(s_b[hh] * cst["cd_b"] + lax.dot_general(
                kw, v_ref[r, vq], (((0,), (0,)), ((), ())), preferred_element_type=F32))
        return tuple(out)

    s_b = loop(0, n_chunks, bwd_body, s0_b, reverse=True)

    def fwd_body(c, s_f):
        r = rows(c)
        out = []
        for hh, cst in enumerate(consts):
            kq = slice(hh * DK, (hh + 1) * DK)
            vq = slice(hh * DV, (hh + 1) * DV)
            q = maybe_rope(q_ref[r, kq].astype(F32), r).astype(BF16)
            ks = ks_scr[r, kq]
            v = v_ref[r, vq]
            att = lax.dot_general(q, ks, (((1,), (1,)), ((), ())), preferred_element_type=F32) * cst["decay"]
            o = jnp.dot(att.astype(BF16), v, preferred_element_type=F32)
            o += jnp.dot(q, s_f[hh].astype(BF16), preferred_element_type=F32) * cst["read_f"]
            o += jnp.dot(q, sb_scr[hh, c], preferred_element_type=F32) * cst["read_b"]
            oc = o - jnp.mean(o, axis=-1, keepdims=True)
            on = oc * lax.rsqrt(jnp.mean(oc * oc, axis=-1, keepdims=True) + EPS)
            o_ref[r, vq] = (_silu(g_ref[r, vq].astype(F32)) * on).astype(BF16)
            kw = (ks.astype(F32) * cst["write_f"]).astype(BF16)
            out.append(s_f[hh] * cst["cd_f"] + lax.dot_general(
                kw, v, (((0,), (0,)), ((), ())), preferred_element_type=F32))
        return tuple(out)

    s_f = loop(0, n_chunks, fwd_body, s0_f)
    if emit_state:
        for hh in range(hg):
            st_ref[0, hh] = s_f[hh]
            st_ref[1, hh] = s_b[hh]


def _retention(z, logit_b, layer, *, row0, n_seq, seq, hg, rope_tab=None, state_in=None, emit_state=False,
               out_into=None):
    n_chunks = seq // CHUNK
    sb = row0 // seq
    kw, vw = hg * DK, hg * DV
    in_specs = [
        pl.BlockSpec((seq, kw), lambda b, h: (sb + b, h)),
        pl.BlockSpec((seq, kw), lambda b, h: (sb + b, HEADS * DK // kw + h)),
        pl.BlockSpec((seq, vw), lambda b, h: (sb + b, 2 * HEADS * DK // vw + h)),
        pl.BlockSpec((seq, vw), lambda b, h: (sb + b, (2 * HEADS * DK + BRANCH_W) // vw + h)),
        pl.BlockSpec((None, 2, hg, 8, 128), lambda b, h: (layer, 0, h, 0, 0)),
    ]
    args = [z, z, z, z, logit_b]
    if rope_tab is not None:
        in_specs.append(pl.BlockSpec((3, seq, DK), lambda b, h: (0, 0, 0)))
        args.append(rope_tab)
    if state_in is not None:
        in_specs.append(pl.BlockSpec((None, None, 2, hg, DK, DV), lambda b, h: (b, layer, 0, h, 0, 0)))
        args.append(state_in)
    aliases = {}
    if out_into is not None:
        aliases = {len(args): 0}
        in_specs.append(pl.BlockSpec(memory_space=pl.ANY))
        args.append(out_into)
    out_specs = [pl.BlockSpec((seq, vw), lambda b, h: (sb + b, h))]
    out_shape = [jax.ShapeDtypeStruct((z.shape[0], BRANCH_W), BF16)]
    if emit_state:
        out_specs.append(pl.BlockSpec((None, 2, hg, DK, DV), lambda b, h: (b, 0, h, 0, 0)))
        out_shape.append(jax.ShapeDtypeStruct((n_seq, 2, HEADS, DK, DV), F32))
    return pl.pallas_call(
        functools.partial(_retention_kernel, n_chunks=n_chunks, hg=hg, rope=rope_tab is not None,
                          has_s0=state_in is not None, emit_state=emit_state, aliased=out_into is not None),
        grid=(n_seq, HEADS // hg),
        in_specs=in_specs,
        out_specs=out_specs,
        out_shape=out_shape,
        scratch_shapes=[pltpu.VMEM((seq, kw), BF16), pltpu.VMEM((hg, n_chunks, DK, DV), BF16)],
        input_output_aliases=aliases,
        compiler_params=_cparams(("arbitrary", "arbitrary"), 48),
        name="retention",
    )(*args)


def _gmlp_kernel(u_ref, v_ref, ng_ref, ws_ref, bs_ref, o_ref, *, n_chunks):
    vn = (_rms(_gelu_tanh(v_ref[...].astype(F32))) * ng_ref[...]).astype(BF16)
    for t in range(n_chunks):
        r = slice(t * CHUNK, (t + 1) * CHUNK)
        for g in range(HEADS):
            cs = slice(g * DV, (g + 1) * DV)
            mixed = jnp.dot(ws_ref[g].astype(BF16), vn[r, cs], preferred_element_type=F32) + bs_ref[g]
            o_ref[r, cs] = (_gelu_tanh(u_ref[r, cs].astype(F32)) * mixed).astype(BF16)


def _gmlp(z, mlp_norm_g3, mlp_ws, bs_b, layer, *, u_col0):
    m = z.shape[0]
    bt = _pick(m, (512, 256, 128))
    ub = u_col0 // BRANCH_W
    return pl.pallas_call(
        functools.partial(_gmlp_kernel, n_chunks=bt // CHUNK),
        grid=(m // bt,),
        in_specs=[
            pl.BlockSpec((bt, BRANCH_W), lambda i: (i, ub)),
            pl.BlockSpec((bt, BRANCH_W), lambda i: (i, ub + 1)),
            pl.BlockSpec((None, 1, BRANCH_W), lambda i: (layer, 0, 0)),
            pl.BlockSpec((None, HEADS, CHUNK, CHUNK), lambda i: (layer, 0, 0, 0)),
            pl.BlockSpec((None, HEADS, CHUNK, DV), lambda i: (layer, 0, 0, 0)),
        ],
        out_specs=pl.BlockSpec((bt, BRANCH_W), lambda i: (i, 0)),
        out_shape=jax.ShapeDtypeStruct((m, BRANCH_W), BF16),
        compiler_params=_cparams(("arbitrary",), 40),
        name="gmlp",
    )(z, z, mlp_norm_g3, mlp_ws, bs_b)


def _diff_attn_kernel(*refs, tq, n_new, n_cache, hg, rope, lam_init, aliased):
    it = iter(refs)
    q_ref, kn_ref, vn_ref = next(it), next(it), next(it)
    kc_ref, vc_ref = (next(it), next(it)) if n_cache else (None, None)
    rope_ref = next(it) if rope else None
    lam_ref, sg_ref = next(it), next(it)
    if aliased:
        next(it)
    o_ref = next(it)
    k_scr, v_scr = next(it), next(it)
    qi = pl.program_id(2)

    def maybe_rope(x, r):
        if not rope:
            return x
        return _rope(x, rope_ref[0, r, :], rope_ref[1, r, :], rope_ref[2, r, :])

    @pl.when(qi == 0)
    def _():
        for hh in range(hg):
            hs = slice(hh * DV, (hh + 1) * DV)
            if n_cache:
                k_scr[hh, 0:n_cache, :] = kc_ref[:, hs].astype(BF16)
                v_scr[hh, 0:n_cache, :] = vc_ref[:, hs].astype(BF16)
            for c in range(2):
                k_scr[hh, n_cache:n_cache + n_new, c * DK:(c + 1) * DK] = maybe_rope(
                    kn_ref[:, hh * DV + c * DK:hh * DV + (c + 1) * DK].astype(F32), slice(0, n_new)).astype(BF16)
            v_scr[hh, n_cache:n_cache + n_new, :] = vn_ref[:, hs].astype(BF16)

    lv = lam_ref[...]
    lam = (jnp.exp(jnp.sum(lv[0:1] * lv[1:2], axis=-1, keepdims=True))
           - jnp.exp(jnp.sum(lv[2:3] * lv[3:4], axis=-1, keepdims=True)) + lam_init)
    q_scale = DK ** -0.5 * LOG2E
    qrows = pl.ds(pl.multiple_of(qi * tq, tq), tq)
    for hh in range(hg):
        pvs, invs = [], []
        for c in range(2):
            cs = slice(hh * DV + c * DK, hh * DV + (c + 1) * DK)
            q = (maybe_rope(q_ref[:, cs].astype(F32), qrows) * q_scale).astype(BF16)
            s = lax.dot_general(q, k_scr[hh, :, c * DK:(c + 1) * DK], (((1,), (1,)), ((), ())),
                                preferred_element_type=F32)
            e = jnp.exp2(s - jnp.max(s, axis=-1, keepdims=True))
            invs.append(1.0 / jnp.sum(e, axis=-1, keepdims=True))
            pvs.append(jnp.dot(e.astype(BF16), v_scr[hh], preferred_element_type=F32))
        o = pvs[0] * invs[0] - pvs[1] * (lam * invs[1])
        o_ref[:, hh * DV:(hh + 1) * DV] = (_rms(o) * sg_ref[...] * (1.0 - lam_init)).astype(BF16)


def _diff_attn(q_arr, q_col0, row0, k_arr, k_col0, v_arr, v_col0, layer, diff_lambda, subln_g3, lam_init, *,
               n_seq, seq, hg, cache_k=None, cache_v=None, rope_tab=None, out_into=None):
    n_cache = 0 if cache_k is None else cache_k.shape[2]
    tq = _pick(seq, (256, 128))
    nq = seq // tq
    w = hg * DV
    qb0, qc0, kc0, vc0 = row0 // tq, q_col0 // w, k_col0 // w, v_col0 // w
    in_specs = [
        pl.BlockSpec((tq, w), lambda b, h, i: (qb0 + b * nq + i, qc0 + h)),
        pl.BlockSpec((seq, w), lambda b, h, i: (b, kc0 + h)),
        pl.BlockSpec((seq, w), lambda b, h, i: (b, vc0 + h)),
    ]
    args = [q_arr, k_arr, v_arr]
    if n_cache:
        in_specs += [pl.BlockSpec((None, None, n_cache, w), lambda b, h, i: (b, layer, 0, h))] * 2
        args += [cache_k, cache_v]
    if rope_tab is not None:
        in_specs.append(pl.BlockSpec((3, seq, DK), lambda b, h, i: (0, 0, 0)))
        args.append(rope_tab)
    in_specs += [pl.BlockSpec((None, 4, DK), lambda b, h, i: (layer, 0, 0)),
                 pl.BlockSpec((None, 1, DV), lambda b, h, i: (layer, 0, 0))]
    args += [diff_lambda, subln_g3]
    aliases = {}
    if out_into is not None:
        aliases = {len(args): 0}
        in_specs.append(pl.BlockSpec(memory_space=pl.ANY))
        args.append(out_into)
    return pl.pallas_call(
        functools.partial(_diff_attn_kernel, tq=tq, n_new=seq, n_cache=n_cache, hg=hg, rope=rope_tab is not None,
                          lam_init=lam_init, aliased=out_into is not None),
        grid=(n_seq, HEADS // hg, nq),
        in_specs=in_specs,
        out_specs=pl.BlockSpec((tq, w), lambda b, h, i: (qb0 + b * nq + i, h)),
        out_shape=jax.ShapeDtypeStruct((q_arr.shape[0], BRANCH_W), BF16),
        scratch_shapes=[pltpu.VMEM((hg, n_cache + seq, 2 * DK), BF16), pltpu.VMEM((hg, n_cache + seq, DV), BF16)],
        input_output_aliases=aliases,
        compiler_params=_cparams(("arbitrary", "arbitrary", "arbitrary"), 56),
        name="diff_attn",
    )(*args)


def kernel(x_prompt, x_sample, cache_k, cache_v, state_ret, c, c_ctx, w_mod, b_mod, norm_g, w_in,
           ret_decay_logit, mlp_norm_g, mlp_ws, mlp_bs, diff_lambda, diff_subln_g, w_branch, w_o,
           w_up, w_down):
    batch, seq, d = x_prompt.shape
    dec_batch, dec_seq, _ = x_sample.shape
    depth = w_in.shape[0]
    hidden = w_down.shape[1]
    past = cache_k.shape[2]
    n_ctx = batch * seq
    n_lat = dec_batch * dec_seq
    tok = _Tokens(n_ctx, dec_batch, dec_seq)
    m = tok.total

    c_rq, c_rk, c_rv, c_rg = 0, HEADS * DK, 2 * HEADS * DK, 2 * HEADS * DK + BRANCH_W
    c_mu = c_rg + BRANCH_W
    c_dq = c_mu + 2 * BRANCH_W
    c_dk = c_dq + BRANCH_W
    c_dv = c_dk + BRANCH_W
    n_main = c_dk + 3 * d

    bm = _pick(math.gcd(n_ctx, n_lat), (1024, 512, 256, 128))
    bn = _pick(math.gcd(d, BRANCH_W), (512, 256, 128))
    nk_down = 2 if (hidden // 2) % 128 == 0 else 1

    w_down_b = w_down.astype(BF16)
    cond = jnp.zeros((N_COND, d), F32).at[0].set(c_ctx).at[1:1 + dec_batch].set(c)
    b_mod3 = b_mod.reshape(depth, 1, 6 * d)
    logit_b = jnp.broadcast_to(ret_decay_logit[:, :, :, None, None], (depth, 2, HEADS, 8, 128))
    bs_b = jnp.broadcast_to(mlp_bs[:, :, :, None], (depth, HEADS, CHUNK, DV))
    mlp_norm_g3 = mlp_norm_g.reshape(depth, 1, BRANCH_W)
    subln_g3 = diff_subln_g.reshape(depth, 1, DV)
    cache_k4 = cache_k.reshape(dec_batch, depth, past, BRANCH_W)
    cache_v4 = cache_v.reshape(dec_batch, depth, past, BRANCH_W)
    rope_tab = _rope_tables(dec_seq)

    mods = [jnp.transpose(_modulation(cond, w_mod, b_mod3, l), (1, 0, 2)) for l in range(depth)]
    x_parts = [x_prompt.reshape(n_ctx, d), x_sample.reshape(n_lat, d)]
    h, = _token_call(tok, x_parts, d, norm_g=norm_g, layer_b=0, mod_b=mods[0], gn=0, shift=0, scale=1)

    ks_out, vs_out, ss_out = [], [], []
    main_split = c_dk // bn
    for l in range(depth):
        lam_init = 0.8 - 0.6 * math.exp(-0.3 * l)
        z = _matmul(h, w_in, l, row0=0, nrows=m, ncols=n_main, out_dtype=BF16, bm=bm, bn=bn,
                    wcol=lambda j: jnp.where(j < main_split, j, j + 2 * BRANCH_W // bn), name="w_in_main")
        dk_ctx = _matmul(h, w_in, l, row0=0, nrows=n_ctx, ncols=BRANCH_W, out_dtype=F32, bm=bm, bn=bn,
                         wcol=lambda j: c_dk // bn + j, name="w_in_dk_ctx")
        dv_ctx = _matmul(h, w_in, l, row0=0, nrows=n_ctx, ncols=BRANCH_W, out_dtype=F32, bm=bm, bn=bn,
                         wcol=lambda j: c_dv // bn + j, name="w_in_dv_ctx")
        kv_lat = _matmul(h, w_in, l, row0=n_ctx, nrows=n_lat, ncols=2 * BRANCH_W, out_dtype=BF16, bm=bm,
                         bn=bn, wcol=lambda j: c_dk // bn + j, name="w_in_kv_lat")
        ks_out.append(dk_ctx.reshape(batch, seq, HEADS, 2, DK))
        vs_out.append(dv_ctx.reshape(batch, seq, HEADS, DV))

        ret_o, st = _retention(z, logit_b, l, row0=0, n_seq=batch, seq=seq, hg=HEADS, emit_state=True)
        ss_out.append(st)
        ret_o, = _retention(z, logit_b, l, row0=n_ctx, n_seq=dec_batch, seq=dec_seq, hg=2,
                            rope_tab=rope_tab, state_in=state_ret, out_into=ret_o)

        mlp_o = _gmlp(z, mlp_norm_g3, mlp_ws, bs_b, l, u_col0=c_mu)

        diff_o = _diff_attn(z, c_dq, 0, dk_ctx, 0, dv_ctx, 0, l, diff_lambda, subln_g3, lam_init,
                            n_seq=batch, seq=seq, hg=4)
        diff_o = _diff_attn(z, c_dq, n_ctx, kv_lat, 0, kv_lat, BRANCH_W, l, diff_lambda, subln_g3, lam_init,
                            n_seq=dec_batch, seq=dec_seq, hg=1, cache_k=cache_k4, cache_v=cache_v4,
                            rope_tab=rope_tab, out_into=diff_o)

        merged = _merge(ret_o, mlp_o, diff_o, w_branch, l, z, c_dk, d, bm=_pick(m, (512, 256, 128)), bn=bn)
        y = _matmul(merged, w_o, l, row0=0, nrows=m, ncols=d, out_dtype=F32, bm=bm, bn=bn,
                    wcol=lambda j: j, name="w_o")
        x, h2 = _token_call(tok, x_parts, d, y=y, norm_g=norm_g, layer_a=l, mod_a=mods[l], layer_b=l,
                            mod_b=mods[l], gy=1, gate=2, gn=2, shift=3, scale=4)
        x_parts = [x]

        act = _swiglu_up(h2, w_up, l, hidden, bm=bm, bn=_pick(hidden, (256, 128)))
        f = _matmul(act, w_down_b, l, row0=0, nrows=m, ncols=d, out_dtype=F32,
                    bm=_pick(m, (512, 256, 128)), bn=_pick(d, (1024, 512, 256, 128)), nk=nk_down,
                    wcol=lambda j: j, name="w_down")
        if l + 1 < depth:
            x, h = _token_call(tok, x_parts, d, y=f, norm_g=norm_g, layer_a=l, mod_a=mods[l], layer_b=l + 1,
                               mod_b=mods[l + 1], gy=3, gate=5, gn=0, shift=0, scale=1)
            x_parts = [x]
        else:
            y_ctx, y_lat = _token_call(tok, x_parts, d, y=f, norm_g=norm_g, layer_a=l, mod_a=mods[l],
                                       split_out=True, gy=3, gate=5)

    return (y_ctx.reshape(batch, seq, d), y_lat.reshape(dec_batch, dec_seq, d),
            jnp.stack(ks_out, axis=1), jnp.stack(vs_out, axis=1), jnp.stack(ss_out, axis=1))
```

```python
import functools
import math

import jax
import jax.numpy as jnp
from jax import lax
from jax.experimental import pallas as pl
from jax.experimental.pallas import tpu as pltpu

F32 = jnp.float32
BF16 = jnp.bfloat16

HEADS = 8
DK = 128
DV = 256
CHUNK = 128
BRANCH_W = HEADS * DV
GRID_W = 64
ROPE_BASE = 10000.0
EPS = 1e-6
N_COND = 16
MIB = 1 << 20
LOG2E = 1.4426950408889634


def _pick(n, prefs):
    for p in prefs:
        if n % p == 0:
            return p
    raise ValueError(f"no block size in {prefs} divides {n}")


def _cparams(dims, vmem_mib):
    return pltpu.CompilerParams(dimension_semantics=dims, vmem_limit_bytes=vmem_mib * MIB)


def _silu(x):
    return x * jax.nn.sigmoid(x)


def _gelu_tanh(x):
    return 0.5 * x * (1.0 + jnp.tanh(math.sqrt(2.0 / math.pi) * (x + 0.044715 * (x * x * x))))


def _rms(x):
    return x * lax.rsqrt(jnp.mean(x * x, axis=-1, keepdims=True) + EPS)


def _mod_kernel(c_ref, w_ref, b_ref, o_ref):
    s = _silu(c_ref[...]).astype(BF16)
    o_ref[...] = jnp.dot(s, w_ref[...].astype(BF16), preferred_element_type=F32) + b_ref[...]


def _modulation(cond, w_mod, b_mod3, layer):
    n_cond, d = cond.shape
    bn = _pick(d, (512, 256, 128))
    per_seg = d // bn
    return pl.pallas_call(
        _mod_kernel,
        grid=(6 * per_seg,),
        in_specs=[
            pl.BlockSpec((n_cond, d), lambda j: (0, 0)),
            pl.BlockSpec((None, d, bn), lambda j: (layer, 0, j)),
            pl.BlockSpec((None, 1, bn), lambda j: (layer, 0, j)),
        ],
        out_specs=pl.BlockSpec((None, n_cond, bn), lambda j: (j // per_seg, 0, j % per_seg)),
        out_shape=jax.ShapeDtypeStruct((6, n_cond, d), F32),
        compiler_params=_cparams(("arbitrary",), 40),
        name="modulation",
    )(cond, w_mod, b_mod3)


class _Tokens:
    def __init__(self, n_ctx, n_dec_seq, dec_seq):
        self.n_ctx = n_ctx
        self.n_dec_seq = n_dec_seq
        self.dec_seq = dec_seq
        self.total = n_ctx + n_dec_seq * dec_seq
        self.br = _pick(math.gcd(n_ctx, dec_seq), (256, 128))
        self.ctx_blocks = n_ctx // self.br

    def cond_index(self, i):
        per = self.dec_seq // self.br
        return jnp.where(i < self.ctx_blocks, 0, 1 + (i - self.ctx_blocks) // per)

    def split_specs(self, d):
        nc = self.ctx_blocks
        return [pl.BlockSpec((self.br, d), lambda i: (jnp.minimum(i, nc - 1), 0)),
                pl.BlockSpec((self.br, d), lambda i: (jnp.maximum(i - nc, 0), 0))]


def _row_spec(tok, d):
    return pl.BlockSpec((tok.br, d), lambda i: (i, 0))


def _ng_spec(layer, d):
    return pl.BlockSpec((None, 4, d), lambda i: (layer, 0, 0))


def _mod_spec(tok, d):
    return pl.BlockSpec((None, 6, d), lambda i: (tok.cond_index(i), 0, 0))


def _token_kernel(*refs, ctx_blocks, split_in, split_out, has_y, has_h, gy, gate, gn, shift, scale):
    it = iter(refs)
    x_refs = [next(it), next(it)] if split_in else [next(it)]
    if has_y:
        y_ref, nga_ref, moda_ref = next(it), next(it), next(it)
    if has_h:
        ngb_ref, modb_ref = next(it), next(it)
    if has_y:
        xo_refs = [next(it), next(it)] if split_out else [next(it)]
    if has_h:
        h_ref = next(it)
    in_ctx = pl.program_id(0) < ctx_blocks

    x = jnp.where(in_ctx, x_refs[0][...], x_refs[1][...]) if split_in else x_refs[0][...]
    if has_y:
        x = x + moda_ref[gate:gate + 1, :] * (_rms(y_ref[...].astype(F32)) * nga_ref[gy:gy + 1, :])
        if split_out:
            @pl.when(in_ctx)
            def _():
                xo_refs[0][...] = x

            @pl.when(jnp.logical_not(in_ctx))
            def _():
                xo_refs[1][...] = x
        else:
            xo_refs[0][...] = x
    if has_h:
        hn = _rms(x) * ngb_ref[gn:gn + 1, :]
        h_ref[...] = (hn * (1.0 + modb_ref[scale:scale + 1, :]) + modb_ref[shift:shift + 1, :]).astype(BF16)


def _token_call(tok, x_parts, d, *, y=None, norm_g=None, layer_a=None, mod_a=None, layer_b=None, mod_b=None,
                split_out=False, gy=0, gate=0, gn=0, shift=0, scale=0):
    split_in = len(x_parts) == 2
    has_y, has_h = y is not None, mod_b is not None
    in_specs = tok.split_specs(d) if split_in else [_row_spec(tok, d)]
    args = list(x_parts)
    if has_y:
        in_specs += [_row_spec(tok, d), _ng_spec(layer_a, d), _mod_spec(tok, d)]
        args += [y, norm_g, mod_a]
    if has_h:
        in_specs += [_ng_spec(layer_b, d), _mod_spec(tok, d)]
        args += [norm_g, mod_b]
    out_specs, out_shape = [], []
    if has_y:
        if split_out:
            out_specs += tok.split_specs(d)
            out_shape += [jax.ShapeDtypeStruct((tok.n_ctx, d), F32),
                          jax.ShapeDtypeStruct((tok.total - tok.n_ctx, d), F32)]
        else:
            out_specs.append(_row_spec(tok, d))
            out_shape.append(jax.ShapeDtypeStruct((tok.total, d), F32))
    if has_h:
        out_specs.append(_row_spec(tok, d))
        out_shape.append(jax.ShapeDtypeStruct((tok.total, d), BF16))
    return pl.pallas_call(
        functools.partial(_token_kernel, ctx_blocks=tok.ctx_blocks, split_in=split_in, split_out=split_out,
                          has_y=has_y, has_h=has_h, gy=gy, gate=gate, gn=gn, shift=shift, scale=scale),
        grid=(tok.total // tok.br,),
        in_specs=in_specs,
        out_specs=out_specs,
        out_shape=out_shape,
        compiler_params=_cparams(("arbitrary",), 48),
        name="token_norm",
    )(*args)


def _mm_kernel(*refs, nk, cast, aliased, acc_scratch):
    it = iter(refs)
    a_ref, w_ref = next(it), next(it)
    if aliased:
        next(it)
    o_ref = next(it)
    wb_ref = next(it) if cast else None
    acc_ref = next(it) if acc_scratch else None
    if cast:
        @pl.when(pl.program_id(1) == 0)
        def _():
            wb_ref[...] = w_ref[...].astype(BF16)
        w = wb_ref[...]
    else:
        w = w_ref[...]
    p = jnp.dot(a_ref[...], w, preferred_element_type=F32)
    if nk == 1:
        o_ref[...] = p.astype(o_ref.dtype).reshape(o_ref.shape)
        return
    k = pl.program_id(2)
    tgt = acc_ref if acc_scratch else o_ref

    @pl.when(k == 0)
    def _():
        tgt[...] = p

    @pl.when(k > 0)
    def _():
        tgt[...] += p

    if acc_scratch:
        @pl.when(k == nk - 1)
        def _():
            o_ref[...] = acc_ref[...].astype(o_ref.dtype)


def _matmul(a, w, layer, *, row0, nrows, wcol, ncols, out_dtype, bm, bn, nk=1, name, stacked=None, out_into=None):
    kdim = a.shape[1]
    bk = kdim // nk
    cast = w.dtype != BF16
    assert nk == 1 or not cast
    acc_scratch = nk > 1 and out_dtype != F32
    r0 = row0 // bm
    in_specs = [
        pl.BlockSpec((bm, bk), lambda j, i, k: (r0 + i, k)),
        pl.BlockSpec((None, bk, bn), lambda j, i, k: (layer, k, wcol(j))),
    ]
    args = [a, w]
    aliases = {}
    if out_into is not None:
        aliases = {len(args): 0}
        in_specs.append(pl.BlockSpec(memory_space=pl.ANY))
        args.append(out_into)
    if stacked is None:
        out_spec = pl.BlockSpec((bm, bn), lambda j, i, k: (i, j))
        out_shape = jax.ShapeDtypeStruct((nrows, ncols), out_dtype)
    else:
        n_seq, depth, seq = stacked
        out_spec = pl.BlockSpec((bm // seq, None, seq, bn), lambda j, i, k: (i, layer, 0, j))
        out_shape = jax.ShapeDtypeStruct((n_seq, depth, seq, ncols), out_dtype)
    scratch = ([pltpu.VMEM((bk, bn), BF16)] if cast else []) + ([pltpu.VMEM((bm, bn), F32)] if acc_scratch else [])
    return pl.pallas_call(
        functools.partial(_mm_kernel, nk=nk, cast=cast, aliased=out_into is not None, acc_scratch=acc_scratch),
        grid=(ncols // bn, nrows // bm, nk),
        in_specs=in_specs,
        out_specs=out_spec,
        out_shape=out_shape,
        scratch_shapes=scratch,
        input_output_aliases=aliases,
        compiler_params=_cparams(("arbitrary", "arbitrary", "arbitrary"), 56),
        name=name,
    )(*args)


def _swiglu_kernel(a_ref, wa_ref, wb_ref, o_ref, w_scr, *, bn):
    @pl.when(pl.program_id(1) == 0)
    def _():
        w_scr[:, 0:bn] = wa_ref[...].astype(BF16)
        w_scr[:, bn:2 * bn] = wb_ref[...].astype(BF16)
    g = jnp.dot(a_ref[...], w_scr[...], preferred_element_type=F32)
    o_ref[...] = (_silu(g[:, 0:bn]) * g[:, bn:2 * bn]).astype(BF16)


def _swiglu_up(h, w_up, layer, hidden, *, bm, bn):
    m, d = h.shape
    half = hidden // bn
    return pl.pallas_call(
        functools.partial(_swiglu_kernel, bn=bn),
        grid=(half, m // bm),
        in_specs=[
            pl.BlockSpec((bm, d), lambda j, i: (i, 0)),
            pl.BlockSpec((None, d, bn), lambda j, i: (layer, 0, j)),
            pl.BlockSpec((None, d, bn), lambda j, i: (layer, 0, half + j)),
        ],
        out_specs=pl.BlockSpec((bm, bn), lambda j, i: (i, j)),
        out_shape=jax.ShapeDtypeStruct((m, hidden), BF16),
        scratch_shapes=[pltpu.VMEM((d, 2 * bn), BF16)],
        compiler_params=_cparams(("arbitrary", "arbitrary"), 56),
        name="swiglu_up",
    )(h, w_up, w_up)


def _merge_kernel(r_ref, m_ref, d_ref, w_ref, g0_ref, g1_ref, g2_ref, o_ref):
    acc = jax.nn.sigmoid(g0_ref[...].astype(F32)) * jnp.dot(r_ref[...], w_ref[0], preferred_element_type=F32)
    acc += jax.nn.sigmoid(g1_ref[...].astype(F32)) * jnp.dot(m_ref[...], w_ref[1], preferred_element_type=F32)
    acc += jax.nn.sigmoid(g2_ref[...].astype(F32)) * jnp.dot(d_ref[...], w_ref[2], preferred_element_type=F32)
    o_ref[...] = acc.astype(BF16)


def _merge(ret_o, mlp_o, diff_o, w_branch, layer, z, gate_col0, d, *, bm, bn):
    m = ret_o.shape[0]
    branch = pl.BlockSpec((bm, BRANCH_W), lambda j, i: (i, 0))

    def gate_spec(t):
        off = (gate_col0 + t * d) // bn
        return pl.BlockSpec((bm, bn), lambda j, i: (i, off + j))

    return pl.pallas_call(
        _merge_kernel,
        grid=(d // bn, m // bm),
        in_specs=[branch, branch, branch,
                  pl.BlockSpec((None, 3, BRANCH_W, bn), lambda j, i: (layer, 0, 0, j)),
                  gate_spec(0), gate_spec(1), gate_spec(2)],
        out_specs=pl.BlockSpec((bm, bn), lambda j, i: (i, j)),
        out_shape=jax.ShapeDtypeStruct((m, d), BF16),
        compiler_params=_cparams(("arbitrary", "arbitrary"), 56),
        name="merge",
    )(ret_o, mlp_o, diff_o, w_branch, z, z, z)


def _rope_tables(n_tok):
    n_rows = n_tok // GRID_W
    rows = jnp.repeat(jnp.arange(n_rows, dtype=F32), GRID_W)
    cols = jnp.tile(jnp.arange(GRID_W, dtype=F32), n_rows)
    axis_dim = DK // 2
    inv = ROPE_BASE ** (-jnp.arange(0, axis_dim, 2, dtype=F32) / axis_dim)
    ang = jnp.stack([rows[:, None] * inv, cols[:, None] * inv], axis=1)
    cos, sin = jnp.cos(ang), jnp.sin(ang)
    zero = jnp.zeros_like(sin)
    cos_t = jnp.concatenate([cos, cos], axis=-1).reshape(n_tok, DK)
    sin_lo = jnp.concatenate([-sin, zero], axis=-1).reshape(n_tok, DK)
    sin_hi = jnp.concatenate([zero, sin], axis=-1).reshape(n_tok, DK)
    return jnp.stack([cos_t, sin_lo, sin_hi], axis=0)


def _rope(x, cos_t, sin_lo, sin_hi):
    return x * cos_t + pltpu.roll(x, DK - 32, 1) * sin_lo + pltpu.roll(x, 32, 1) * sin_hi


def _log_sigmoid(x):
    return jnp.minimum(x, 0.0) - jnp.log(1.0 + jnp.exp(-jnp.abs(x)))


def _retention_kernel(*refs, n_chunks, hg, rope, has_s0, emit_state, n_aliased):
    it = iter(refs)
    q_ref, k_ref, v_ref, g_ref, logit_ref = next(it), next(it), next(it), next(it), next(it)
    rope_ref = next(it) if rope else None
    s0_ref = next(it) if has_s0 else None
    for _ in range(n_aliased):
        next(it)
    o_ref = next(it)
    st_ref = next(it) if emit_state else None
    ks_scr, sb_scr = next(it), next(it)

    row = lax.broadcasted_iota(jnp.int32, (CHUNK, CHUNK), 0).astype(F32)
    col = lax.broadcasted_iota(jnp.int32, (CHUNK, CHUNK), 1).astype(F32)
    rel = row - col
    row_v = lax.broadcasted_iota(jnp.int32, (CHUNK, DV), 0).astype(F32)
    k_scale = DK ** -0.5

    def rows(c):
        if isinstance(c, int):
            return pl.ds(c * CHUNK, CHUNK)
        return pl.ds(pl.multiple_of(c * CHUNK, CHUNK), CHUNK)

    def maybe_rope(x, r):
        if not rope:
            return x
        return _rope(x, rope_ref[0, r, :], rope_ref[1, r, :], rope_ref[2, r, :])

    def loop(lo, hi, body, init, reverse=False):
        if n_chunks <= 4:
            carry = init
            for c in (range(hi - 1, lo - 1, -1) if reverse else range(lo, hi)):
                carry = body(c, carry)
            return carry
        if reverse:
            return lax.fori_loop(lo, hi, lambda t, cr: body(hi - 1 - t + lo, cr), init)
        return lax.fori_loop(lo, hi, body, init)

    consts = []
    for hh in range(hg):
        lg_f = _log_sigmoid(logit_ref[0, hh])[0:1, :]
        lg_b = _log_sigmoid(logit_ref[1, hh])[0:1, :]
        lg_fv = jnp.concatenate([lg_f, lg_f], axis=1)
        lg_bv = jnp.concatenate([lg_b, lg_b], axis=1)
        consts.append(dict(
            decay=(jnp.where(rel >= 0, jnp.exp(lg_f * jnp.maximum(rel, 0.0)), 0.0)
                   + jnp.where(rel <= 0, jnp.exp(lg_b * jnp.maximum(-rel, 0.0)), 0.0)),
            read_f=jnp.exp(lg_fv * (row_v + 1.0)),
            read_b=jnp.exp(lg_bv * (CHUNK - row_v)),
            write_f=jnp.exp(lg_f * (CHUNK - 1.0 - row)),
            write_b=jnp.exp(lg_b * row),
            cd_f=jnp.exp(lg_fv * float(CHUNK)),
            cd_b=jnp.exp(lg_bv * float(CHUNK)),
        ))
    if has_s0:
        s0_f = tuple(s0_ref[0, hh] for hh in range(hg))
        s0_b = tuple(s0_ref[1, hh] for hh in range(hg))
    else:
        s0_f = s0_b = tuple(jnp.zeros((DK, DV), F32) for _ in range(hg))

    def bwd_body(c, s_b):
        r = rows(c)
        out = []
        for hh, cst in enumerate(consts):
            kq = slice(hh * DK, (hh + 1) * DK)
            vq = slice(hh * DV, (hh + 1) * DV)
            ks = maybe_rope(k_ref[r, kq].astype(F32), r) * k_scale
            ks_scr[r, kq] = ks.astype(BF16)
            sb_scr[hh, c] = s_b[hh].astype(BF16)
            kw = (ks * cst["write_b"]).astype(BF16)
            out.append(s_b[hh] * cst["cd_b"] + lax.dot_general(
                kw, v_ref[r, vq], (((0,), (0,)), ((), ())), preferred_element_type=F32))
        return tuple(out)

    s_b = loop(0, n_chunks, bwd_body, s0_b, reverse=True)

    def fwd_body(c, s_f):
        r = rows(c)
        out = []
        for hh, cst in enumerate(consts):
            kq = slice(hh * DK, (hh + 1) * DK)
            vq = slice(hh * DV, (hh + 1) * DV)
            q = maybe_rope(q_ref[r, kq].astype(F32), r).astype(BF16)
            ks = ks_scr[r, kq]
            v = v_ref[r, vq]
            att = lax.dot_general(q, ks, (((1,), (1,)), ((), ())), preferred_element_type=F32) * cst["decay"]
            o = jnp.dot(att.astype(BF16), v, preferred_element_type=F32)
            o += jnp.dot(q, s_f[hh].astype(BF16), preferred_element_type=F32) * cst["read_f"]
            o += jnp.dot(q, sb_scr[hh, c], preferred_element_type=F32) * cst["read_b"]
            oc = o - jnp.mean(o, axis=-1, keepdims=True)
            on = oc * lax.rsqrt(jnp.mean(oc * oc, axis=-1, keepdims=True) + EPS)
            o_ref[r, vq] = (_silu(g_ref[r, vq].astype(F32)) * on).astype(BF16)
            kw = (ks.astype(F32) * cst["write_f"]).astype(BF16)
            out.append(s_f[hh] * cst["cd_f"] + lax.dot_general(
                kw, v, (((0,), (0,)), ((), ())), preferred_element_type=F32))
        return tuple(out)

    s_f = loop(0, n_chunks, fwd_body, s0_f)
    if emit_state:
        for hh in range(hg):
            st_ref[0, hh] = s_f[hh]
            st_ref[1, hh] = s_b[hh]


def _retention(z, logit_b, layer, *, row0, n_seq, seq, hg, rope_tab=None, state_in=None, state_depth=None,
               out_into=None, state_into=None):
    emit_state = state_depth is not None
    n_chunks = seq // CHUNK
    sb = row0 // seq
    kw, vw = hg * DK, hg * DV
    in_specs = [
        pl.BlockSpec((seq, kw), lambda b, h: (sb + b, h)),
        pl.BlockSpec((seq, kw), lambda b, h: (sb + b, HEADS * DK // kw + h)),
        pl.BlockSpec((seq, vw), lambda b, h: (sb + b, 2 * HEADS * DK // vw + h)),
        pl.BlockSpec((seq, vw), lambda b, h: (sb + b, (2 * HEADS * DK + BRANCH_W) // vw + h)),
        pl.BlockSpec((None, 2, hg, 8, 128), lambda b, h: (layer, 0, h, 0, 0)),
    ]
    args = [z, z, z, z, logit_b]
    if rope_tab is not None:
        in_specs.append(pl.BlockSpec((3, seq, DK), lambda b, h: (0, 0, 0)))
        args.append(rope_tab)
    if state_in is not None:
        in_specs.append(pl.BlockSpec((None, None, 2, hg, DK, DV), lambda b, h: (b, layer, 0, h, 0, 0)))
        args.append(state_in)
    aliases = {}
    for out_idx, arr in ((0, out_into), (1, state_into)):
        if arr is not None:
            aliases[len(args)] = out_idx
            in_specs.append(pl.BlockSpec(memory_space=pl.ANY))
            args.append(arr)
    out_specs = [pl.BlockSpec((seq, vw), lambda b, h: (sb + b, h))]
    out_shape = [jax.ShapeDtypeStruct((z.shape[0], BRANCH_W), BF16)]
    if emit_state:
        out_specs.append(pl.BlockSpec((None, None, 2, hg, DK, DV), lambda b, h: (b, layer, 0, h, 0, 0)))
        out_shape.append(jax.ShapeDtypeStruct((n_seq, state_depth, 2, HEADS, DK, DV), F32))
    return pl.pallas_call(
        functools.partial(_retention_kernel, n_chunks=n_chunks, hg=hg, rope=rope_tab is not None,
                          has_s0=state_in is not None, emit_state=emit_state, n_aliased=len(aliases)),
        grid=(n_seq, HEADS // hg),
        in_specs=in_specs,
        out_specs=out_specs,
        out_shape=out_shape,
        scratch_shapes=[pltpu.VMEM((seq, kw), BF16), pltpu.VMEM((hg, n_chunks, DK, DV), BF16)],
        input_output_aliases=aliases,
        compiler_params=_cparams(("arbitrary", "arbitrary"), 48),
        name="retention",
    )(*args)


def _gmlp_kernel(u_ref, v_ref, ng_ref, ws_ref, bs_ref, o_ref, *, n_chunks):
    vn = (_rms(_gelu_tanh(v_ref[...].astype(F32))) * ng_ref[...]).astype(BF16)
    for t in range(n_chunks):
        r = slice(t * CHUNK, (t + 1) * CHUNK)
        for g in range(HEADS):
            cs = slice(g * DV, (g + 1) * DV)
            mixed = jnp.dot(ws_ref[g].astype(BF16), vn[r, cs], preferred_element_type=F32) + bs_ref[g]
            o_ref[r, cs] = (_gelu_tanh(u_ref[r, cs].astype(F32)) * mixed).astype(BF16)


def _gmlp(z, mlp_norm_g3, mlp_ws, bs_b, layer, *, u_col0):
    m = z.shape[0]
    bt = _pick(m, (512, 256, 128))
    ub = u_col0 // BRANCH_W
    return pl.pallas_call(
        functools.partial(_gmlp_kernel, n_chunks=bt // CHUNK),
        grid=(m // bt,),
        in_specs=[
            pl.BlockSpec((bt, BRANCH_W), lambda i: (i, ub)),
            pl.BlockSpec((bt, BRANCH_W), lambda i: (i, ub + 1)),
            pl.BlockSpec((None, 1, BRANCH_W), lambda i: (layer, 0, 0)),
            pl.BlockSpec((None, HEADS, CHUNK, CHUNK), lambda i: (layer, 0, 0, 0)),
            pl.BlockSpec((None, HEADS, CHUNK, DV), lambda i: (layer, 0, 0, 0)),
        ],
        out_specs=pl.BlockSpec((bt, BRANCH_W), lambda i: (i, 0)),
        out_shape=jax.ShapeDtypeStruct((m, BRANCH_W), BF16),
        compiler_params=_cparams(("arbitrary",), 40),
        name="gmlp",
    )(z, z, mlp_norm_g3, mlp_ws, bs_b)


Q_SCALE = DK ** -0.5 * LOG2E


def _attn_prep_kernel(q_ref, k_ref, kc_ref, rope_ref, qo_ref, ko_ref, *, cache_blocks):
    r = pl.program_id(1)

    @pl.when(r < cache_blocks)
    def _():
        ko_ref[...] = kc_ref[...].astype(BF16)

    @pl.when(r >= cache_blocks)
    def _():
        cos_t, sin_lo, sin_hi = rope_ref[0], rope_ref[1], rope_ref[2]
        for g in range(BRANCH_W // DK):
            gs = slice(g * DK, (g + 1) * DK)
            ko_ref[:, gs] = _rope(k_ref[:, gs].astype(F32), cos_t, sin_lo, sin_hi).astype(BF16)
            qo_ref[:, gs] = (_rope(q_ref[:, gs].astype(F32), cos_t, sin_lo, sin_hi) * Q_SCALE).astype(BF16)


def _attn_prep(z, q_col0, row0, kv_lat, cache_k4, layer, rope_tab, *, n_seq, seq):
    past = cache_k4.shape[2]
    tr = _pick(math.gcd(past, seq), (512, 256, 128))
    cb, nb = past // tr, seq // tr
    qb0, qc0 = row0 // tr, q_col0 // BRANCH_W

    def new_blk(r):
        return jnp.maximum(r - cb, 0)

    return pl.pallas_call(
        functools.partial(_attn_prep_kernel, cache_blocks=cb),
        grid=(n_seq, cb + nb),
        in_specs=[
            pl.BlockSpec((tr, BRANCH_W), lambda b, r: (qb0 + b * nb + new_blk(r), qc0)),
            pl.BlockSpec((tr, BRANCH_W), lambda b, r: (b * nb + new_blk(r), 0)),
            pl.BlockSpec((None, None, tr, BRANCH_W), lambda b, r: (b, layer, jnp.minimum(r, cb - 1), 0)),
            pl.BlockSpec((3, tr, DK), lambda b, r: (0, new_blk(r), 0)),
        ],
        out_specs=[
            pl.BlockSpec((tr, BRANCH_W), lambda b, r: (b * nb + new_blk(r), 0)),
            pl.BlockSpec((None, tr, BRANCH_W), lambda b, r: (b, r, 0)),
        ],
        out_shape=[jax.ShapeDtypeStruct((n_seq * seq, BRANCH_W), BF16),
                   jax.ShapeDtypeStruct((n_seq, past + seq, BRANCH_W), BF16)],
        compiler_params=_cparams(("arbitrary", "arbitrary"), 48),
        name="attn_prep",
    )(z, kv_lat, cache_k4, rope_tab)


def _diff_attn_kernel(*refs, tq, n_new, n_cache, hg, prepped, lam_init, aliased):
    it = iter(refs)
    q_ref, k_ref, v_ref = next(it), next(it), next(it)
    vc_ref = next(it) if n_cache else None
    lam_ref, sg_ref = next(it), next(it)
    if aliased:
        next(it)
    o_ref = next(it)
    k_scr = None if prepped else next(it)
    v_scr = next(it)
    sub = min(tq, 256)

    @pl.when(pl.program_id(2) == 0)
    def _():
        for hh in range(hg):
            hs = slice(hh * DV, (hh + 1) * DV)
            if not prepped:
                k_scr[hh] = k_ref[:, hs].astype(BF16)
            if n_cache:
                v_scr[hh, 0:n_cache, :] = vc_ref[:, hs].astype(BF16)
            v_scr[hh, n_cache:n_cache + n_new, :] = v_ref[:, hs].astype(BF16)

    lv = lam_ref[...]
    lam = (jnp.exp(jnp.sum(lv[0:1] * lv[1:2], axis=-1, keepdims=True))
           - jnp.exp(jnp.sum(lv[2:3] * lv[3:4], axis=-1, keepdims=True)) + lam_init)
    for hh, r0 in [(hh, r0) for hh in range(hg) for r0 in range(0, tq, sub)]:
        rs = slice(r0, r0 + sub)
        pvs, invs = [], []
        for c in range(2):
            cs = slice(hh * DV + c * DK, hh * DV + (c + 1) * DK)
            if prepped:
                q, k = q_ref[rs, cs], k_ref[:, cs]
            else:
                q = (q_ref[rs, cs].astype(F32) * Q_SCALE).astype(BF16)
                k = k_scr[hh, :, c * DK:(c + 1) * DK]
            s = lax.dot_general(q, k, (((1,), (1,)), ((), ())), preferred_element_type=F32)
            e = jnp.exp2(s - jnp.max(s, axis=-1, keepdims=True))
            invs.append(1.0 / jnp.sum(e, axis=-1, keepdims=True))
            pvs.append(jnp.dot(e.astype(BF16), v_scr[hh], preferred_element_type=F32))
        o = pvs[0] * invs[0] - pvs[1] * (lam * invs[1])
        o_ref[rs, hh * DV:(hh + 1) * DV] = (_rms(o) * sg_ref[...] * (1.0 - lam_init)).astype(BF16)


def _diff_attn(q_arr, q_spec, k_arr, k_spec, v_arr, v_spec, layer, diff_lambda, subln_g3, lam_init, *,
               out_rows, row0, n_seq, seq, n_keys, hg, tq, prepped, cache_v=None, out_into=None):
    n_cache = 0 if cache_v is None else cache_v.shape[2]
    nq = seq // tq
    w = hg * DV
    in_specs = [q_spec, k_spec, v_spec]
    args = [q_arr, k_arr, v_arr]
    if n_cache:
        in_specs.append(pl.BlockSpec((None, None, n_cache, w), lambda b, h, i: (b, layer, 0, h)))
        args.append(cache_v)
    in_specs += [pl.BlockSpec((None, 4, DK), lambda b, h, i: (layer, 0, 0)),
                 pl.BlockSpec((None, 1, DV), lambda b, h, i: (layer, 0, 0))]
    args += [diff_lambda, subln_g3]
    aliases = {}
    if out_into is not None:
        aliases = {len(args): 0}
        in_specs.append(pl.BlockSpec(memory_space=pl.ANY))
        args.append(out_into)
    ob0 = row0 // tq
    scratch = [] if prepped else [pltpu.VMEM((hg, n_keys, 2 * DK), BF16)]
    scratch.append(pltpu.VMEM((hg, n_keys, DV), BF16))
    return pl.pallas_call(
        functools.partial(_diff_attn_kernel, tq=tq, n_new=seq, n_cache=n_cache, hg=hg, prepped=prepped,
                          lam_init=lam_init, aliased=out_into is not None),
        grid=(n_seq, HEADS // hg, nq),
        in_specs=in_specs,
        out_specs=pl.BlockSpec((tq, w), lambda b, h, i: (ob0 + b * nq + i, h)),
        out_shape=jax.ShapeDtypeStruct((out_rows, BRANCH_W), BF16),
        scratch_shapes=scratch,
        input_output_aliases=aliases,
        compiler_params=_cparams(("arbitrary", "arbitrary", "arbitrary"), 56),
        name="diff_attn",
    )(*args)


def kernel(x_prompt, x_sample, cache_k, cache_v, state_ret, c, c_ctx, w_mod, b_mod, norm_g, w_in,
           ret_decay_logit, mlp_norm_g, mlp_ws, mlp_bs, diff_lambda, diff_subln_g, w_branch, w_o,
           w_up, w_down):
    batch, seq, d = x_prompt.shape
    dec_batch, dec_seq, _ = x_sample.shape
    depth = w_in.shape[0]
    hidden = w_down.shape[1]
    past = cache_k.shape[2]
    n_ctx = batch * seq
    n_lat = dec_batch * dec_seq
    tok = _Tokens(n_ctx, dec_batch, dec_seq)
    m = tok.total

    c_rq, c_rk, c_rv, c_rg = 0, HEADS * DK, 2 * HEADS * DK, 2 * HEADS * DK + BRANCH_W
    c_mu = c_rg + BRANCH_W
    c_dq = c_mu + 2 * BRANCH_W
    c_dk = c_dq + BRANCH_W
    c_dv = c_dk + BRANCH_W
    n_main = c_dk + 3 * d

    bm = _pick(math.gcd(n_ctx, n_lat), (1024, 512, 256, 128))
    assert bm % seq == 0
    bn = _pick(math.gcd(d, BRANCH_W), (1024, 512, 256, 128))
    bm_half = _pick(m, (512, 256, 128))
    nk_down = 2 if (hidden // 2) % 128 == 0 else 1
    n_keys = past + dec_seq
    tq_lat = _pick(dec_seq, (1024, 512, 256, 128))

    w_in_b = w_in.astype(BF16)
    w_branch_b = w_branch.astype(BF16)
    w_o_b = w_o.astype(BF16)
    w_down_b = w_down.astype(BF16)
    cond = jnp.zeros((N_COND, d), F32).at[0].set(c_ctx).at[1:1 + dec_batch].set(c)
    b_mod3 = b_mod.reshape(depth, 1, 6 * d)
    logit_b = jnp.broadcast_to(ret_decay_logit[:, :, :, None, None], (depth, 2, HEADS, 8, 128))
    bs_b = jnp.broadcast_to(mlp_bs[:, :, :, None], (depth, HEADS, CHUNK, DV))
    mlp_norm_g3 = mlp_norm_g.reshape(depth, 1, BRANCH_W)
    subln_g3 = diff_subln_g.reshape(depth, 1, DV)
    cache_k4 = cache_k.reshape(dec_batch, depth, past, BRANCH_W)
    cache_v4 = cache_v.reshape(dec_batch, depth, past, BRANCH_W)
    rope_tab = _rope_tables(dec_seq)

    mods = [jnp.transpose(_modulation(cond, w_mod, b_mod3, l), (1, 0, 2)) for l in range(depth)]
    x_parts = [x_prompt.reshape(n_ctx, d), x_sample.reshape(n_lat, d)]
    h, = _token_call(tok, x_parts, d, norm_g=norm_g, layer_b=0, mod_b=mods[0], gn=0, shift=0, scale=1)

    new_k = new_v = new_s = None
    main_split = c_dk // bn
    w_ctx = 4 * DV
    nq_lat = dec_seq // tq_lat
    for l in range(depth):
        lam_init = 0.8 - 0.6 * math.exp(-0.3 * l)
        z = _matmul(h, w_in_b, l, row0=0, nrows=m, ncols=n_main, out_dtype=BF16, bm=bm, bn=bn,
                    wcol=lambda j: jnp.where(j < main_split, j, j + 2 * BRANCH_W // bn), name="w_in_main")
        new_k = _matmul(h, w_in_b, l, row0=0, nrows=n_ctx, ncols=BRANCH_W, out_dtype=F32, bm=bm, bn=bn,
                        wcol=lambda j: c_dk // bn + j, name="w_in_dk_ctx", stacked=(batch, depth, seq),
                        out_into=new_k)
        new_v = _matmul(h, w_in_b, l, row0=0, nrows=n_ctx, ncols=BRANCH_W, out_dtype=F32, bm=bm, bn=bn,
                        wcol=lambda j: c_dv // bn + j, name="w_in_dv_ctx", stacked=(batch, depth, seq),
                        out_into=new_v)
        kv_lat = _matmul(h, w_in_b, l, row0=n_ctx, nrows=n_lat, ncols=2 * BRANCH_W, out_dtype=BF16, bm=bm,
                         bn=bn, wcol=lambda j: c_dk // bn + j, name="w_in_kv_lat")

        ret_o, new_s = _retention(z, logit_b, l, row0=0, n_seq=batch, seq=seq, hg=HEADS, state_depth=depth,
                                  state_into=new_s)
        ret_o, = _retention(z, logit_b, l, row0=n_ctx, n_seq=dec_batch, seq=dec_seq, hg=2,
                            rope_tab=rope_tab, state_in=state_ret, out_into=ret_o)

        mlp_o = _gmlp(z, mlp_norm_g3, mlp_ws, bs_b, l, u_col0=c_mu)

        stacked_spec = pl.BlockSpec((None, None, seq, w_ctx), lambda b, hq, i: (b, l, 0, hq))
        diff_o = _diff_attn(
            z, pl.BlockSpec((seq, w_ctx), lambda b, hq, i: (b, c_dq // w_ctx + hq)),
            new_k, stacked_spec, new_v, stacked_spec, l, diff_lambda, subln_g3, lam_init,
            out_rows=m, row0=0, n_seq=batch, seq=seq, n_keys=seq, hg=4, tq=seq, prepped=False)
        q_lat, k_all = _attn_prep(z, c_dq, n_ctx, kv_lat, cache_k4, l, rope_tab, n_seq=dec_batch, seq=dec_seq)
        diff_o = _diff_attn(
            q_lat, pl.BlockSpec((tq_lat, DV), lambda b, hq, i: (b * nq_lat + i, hq)),
            k_all, pl.BlockSpec((None, n_keys, DV), lambda b, hq, i: (b, 0, hq)),
            kv_lat, pl.BlockSpec((dec_seq, DV), lambda b, hq, i: (b, HEADS + hq)),
            l, diff_lambda, subln_g3, lam_init, out_rows=m, row0=n_ctx, n_seq=dec_batch, seq=dec_seq,
            n_keys=n_keys, hg=1, tq=tq_lat, prepped=True, cache_v=cache_v4, out_into=diff_o)

        merged = _merge(ret_o, mlp_o, diff_o, w_branch_b, l, z, c_dk, d, bm=bm_half, bn=bn)
        y = _matmul(merged, w_o_b, l, row0=0, nrows=m, ncols=d, out_dtype=BF16, bm=bm, bn=bn,
                    wcol=lambda j: j, name="w_o")
        x, h2 = _token_call(tok, x_parts, d, y=y, norm_g=norm_g, layer_a=l, mod_a=mods[l], layer_b=l,
                            mod_b=mods[l], gy=1, gate=2, gn=2, shift=3, scale=4)
        x_parts = [x]

        act = _swiglu_up(h2, w_up, l, hidden, bm=bm, bn=_pick(hidden, (256, 128)))
        f = _matmul(act, w_down_b, l, row0=0, nrows=m, ncols=d, out_dtype=BF16, bm=bm_half, bn=bn, nk=nk_down,
                    wcol=lambda j: j, name="w_down")
        if l + 1 < depth:
            x, h = _token_call(tok, x_parts, d, y=f, norm_g=norm_g, layer_a=l, mod_a=mods[l], layer_b=l + 1,
                               mod_b=mods[l + 1], gy=3, gate=5, gn=0, shift=0, scale=1)
            x_parts = [x]
        else:
            y_ctx, y_lat = _token_call(tok, x_parts, d, y=f, norm_g=norm_g, layer_a=l, mod_a=mods[l],
                                       split_out=True, gy=3, gate=5)

    return (y_ctx.reshape(batch, seq, d), y_lat.reshape(dec_batch, dec_seq, d),
            new_k.reshape(batch, depth, seq, HEADS, 2, DK), new_v.reshape(batch, depth, seq, HEADS, DV), new_s)
```

```python
import functools
import math

import jax
import jax.numpy as jnp
from jax import lax
from jax.experimental import pallas as pl
from jax.experimental.pallas import tpu as pltpu

F32 = jnp.float32
BF16 = jnp.bfloat16

HEADS = 8
DK = 128
DV = 256
CHUNK = 128
BRANCH_W = HEADS * DV
GRID_W = 64
ROPE_BASE = 10000.0
EPS = 1e-6
N_COND = 16
MIB = 1 << 20
LOG2E = 1.4426950408889634


def _pick(n, prefs):
    for p in prefs:
        if n % p == 0:
            return p
    raise ValueError(f"no block size in {prefs} divides {n}")


def _cparams(dims, vmem_mib):
    return pltpu.CompilerParams(dimension_semantics=dims, vmem_limit_bytes=vmem_mib * MIB)


def _silu(x):
    return x * jax.nn.sigmoid(x)


def _gelu_tanh(x):
    return 0.5 * x * (1.0 + jnp.tanh(math.sqrt(2.0 / math.pi) * (x + 0.044715 * (x * x * x))))


def _rms(x):
    return x * lax.rsqrt(jnp.mean(x * x, axis=-1, keepdims=True) + EPS)


def _mod_kernel(c_ref, w_ref, b_ref, o_ref):
    s = _silu(c_ref[...]).astype(BF16)
    o_ref[...] = jnp.dot(s, w_ref[...].astype(BF16), preferred_element_type=F32) + b_ref[...]


def _modulation(cond, w_mod, b_mod3, layer):
    n_cond, d = cond.shape
    bn = _pick(d, (512, 256, 128))
    per_seg = d // bn
    return pl.pallas_call(
        _mod_kernel,
        grid=(6 * per_seg,),
        in_specs=[
            pl.BlockSpec((n_cond, d), lambda j: (0, 0)),
            pl.BlockSpec((None, d, bn), lambda j: (layer, 0, j)),
            pl.BlockSpec((None, 1, bn), lambda j: (layer, 0, j)),
        ],
        out_specs=pl.BlockSpec((None, n_cond, bn), lambda j: (j // per_seg, 0, j % per_seg)),
        out_shape=jax.ShapeDtypeStruct((6, n_cond, d), F32),
        compiler_params=_cparams(("arbitrary",), 40),
        name="modulation",
    )(cond, w_mod, b_mod3)


class _Tokens:
    def __init__(self, n_ctx, n_dec_seq, dec_seq):
        self.n_ctx = n_ctx
        self.n_dec_seq = n_dec_seq
        self.dec_seq = dec_seq
        self.total = n_ctx + n_dec_seq * dec_seq
        self.br = _pick(math.gcd(n_ctx, dec_seq), (256, 128))
        self.ctx_blocks = n_ctx // self.br

    def cond_index(self, i):
        per = self.dec_seq // self.br
        return jnp.where(i < self.ctx_blocks, 0, 1 + (i - self.ctx_blocks) // per)

    def split_specs(self, d):
        nc = self.ctx_blocks
        return [pl.BlockSpec((self.br, d), lambda i: (jnp.minimum(i, nc - 1), 0)),
                pl.BlockSpec((self.br, d), lambda i: (jnp.maximum(i - nc, 0), 0))]


def _row_spec(tok, d):
    return pl.BlockSpec((tok.br, d), lambda i: (i, 0))


def _ng_spec(layer, d):
    return pl.BlockSpec((None, 4, d), lambda i: (layer, 0, 0))


def _mod_spec(tok, d):
    return pl.BlockSpec((None, 6, d), lambda i: (tok.cond_index(i), 0, 0))


def _token_kernel(*refs, ctx_blocks, split_in, split_out, has_y, has_h, gy, gate, gn, shift, scale):
    it = iter(refs)
    x_refs = [next(it), next(it)] if split_in else [next(it)]
    if has_y:
        y_ref, nga_ref, moda_ref = next(it), next(it), next(it)
    if has_h:
        ngb_ref, modb_ref = next(it), next(it)
    if has_y:
        xo_refs = [next(it), next(it)] if split_out else [next(it)]
    if has_h:
        h_ref = next(it)
    in_ctx = pl.program_id(0) < ctx_blocks

    x = jnp.where(in_ctx, x_refs[0][...], x_refs[1][...]) if split_in else x_refs[0][...]
    if has_y:
        x = x + moda_ref[gate:gate + 1, :] * (_rms(y_ref[...].astype(F32)) * nga_ref[gy:gy + 1, :])
        if split_out:
            @pl.when(in_ctx)
            def _():
                xo_refs[0][...] = x

            @pl.when(jnp.logical_not(in_ctx))
            def _():
                xo_refs[1][...] = x
        else:
            xo_refs[0][...] = x
    if has_h:
        hn = _rms(x) * ngb_ref[gn:gn + 1, :]
        h_ref[...] = (hn * (1.0 + modb_ref[scale:scale + 1, :]) + modb_ref[shift:shift + 1, :]).astype(BF16)


def _token_call(tok, x_parts, d, *, y=None, norm_g=None, layer_a=None, mod_a=None, layer_b=None, mod_b=None,
                split_out=False, gy=0, gate=0, gn=0, shift=0, scale=0):
    split_in = len(x_parts) == 2
    has_y, has_h = y is not None, mod_b is not None
    in_specs = tok.split_specs(d) if split_in else [_row_spec(tok, d)]
    args = list(x_parts)
    if has_y:
        in_specs += [_row_spec(tok, d), _ng_spec(layer_a, d), _mod_spec(tok, d)]
        args += [y, norm_g, mod_a]
    if has_h:
        in_specs += [_ng_spec(layer_b, d), _mod_spec(tok, d)]
        args += [norm_g, mod_b]
    out_specs, out_shape = [], []
    if has_y:
        if split_out:
            out_specs += tok.split_specs(d)
            out_shape += [jax.ShapeDtypeStruct((tok.n_ctx, d), F32),
                          jax.ShapeDtypeStruct((tok.total - tok.n_ctx, d), F32)]
        else:
            out_specs.append(_row_spec(tok, d))
            out_shape.append(jax.ShapeDtypeStruct((tok.total, d), F32))
    if has_h:
        out_specs.append(_row_spec(tok, d))
        out_shape.append(jax.ShapeDtypeStruct((tok.total, d), BF16))
    return pl.pallas_call(
        functools.partial(_token_kernel, ctx_blocks=tok.ctx_blocks, split_in=split_in, split_out=split_out,
                          has_y=has_y, has_h=has_h, gy=gy, gate=gate, gn=gn, shift=shift, scale=scale),
        grid=(tok.total // tok.br,),
        in_specs=in_specs,
        out_specs=out_specs,
        out_shape=out_shape,
        compiler_params=_cparams(("arbitrary",), 48),
        name="token_norm",
    )(*args)


class _SideCast:
    def __init__(self, src, steps, first_row=0, n_rows=None):
        cols = src.shape[-1]
        flat = src.reshape(-1, cols)
        n_rows = flat.shape[0] - first_row if n_rows is None else n_rows
        self.rows = next(r for r in (16, 32, 64, 96, 128, 192, 256, 384, 512, 768, 1024)
                         if n_rows % r == 0 and first_row % r == 0 and n_rows // r <= steps)
        self.cols = cols
        self.count = n_rows // self.rows
        self.first = first_row // self.rows
        self.src3 = flat.reshape(-1, self.rows, cols)

    def specs(self, step_of):
        blk = (None, self.rows, self.cols)

        def chunk(*ids):
            return jnp.minimum(step_of(*ids), self.count - 1)

        return (pl.BlockSpec(blk, lambda *ids: (self.first + chunk(*ids), 0, 0)),
                pl.BlockSpec(blk, lambda *ids: (chunk(*ids), 0, 0)),
                jax.ShapeDtypeStruct((self.count, self.rows, self.cols), BF16))


def _side_cast_step(step, sides):
    for in_ref, out_ref, count in sides:
        @pl.when(step < count)
        def _():
            out_ref[...] = in_ref[...].astype(BF16)


def _mm_kernel(*refs, nk, nm, cast, aliased, acc_scratch, side_counts):
    it = iter(refs)
    a_ref, w_ref = next(it), next(it)
    if aliased:
        next(it)
    side_in = [next(it) for _ in side_counts]
    o_ref = next(it)
    side_out = [next(it) for _ in side_counts]
    wb_ref = next(it) if cast else None
    acc_ref = next(it) if acc_scratch else None
    step = (pl.program_id(0) * nm + pl.program_id(1)) * nk + pl.program_id(2)
    _side_cast_step(step, list(zip(side_in, side_out, side_counts)))
    if cast:
        @pl.when(pl.program_id(1) == 0)
        def _():
            wb_ref[...] = w_ref[...].astype(BF16)
        w = wb_ref[...]
    else:
        w = w_ref[...]
    p = jnp.dot(a_ref[...], w, preferred_element_type=F32)
    if nk == 1:
        o_ref[...] = p.astype(o_ref.dtype).reshape(o_ref.shape)
        return
    k = pl.program_id(2)
    tgt = acc_ref if acc_scratch else o_ref

    @pl.when(k == 0)
    def _():
        tgt[...] = p

    @pl.when(k > 0)
    def _():
        tgt[...] += p

    if acc_scratch:
        @pl.when(k == nk - 1)
        def _():
            o_ref[...] = acc_ref[...].astype(o_ref.dtype)


def _matmul(a, w, layer, *, row0, nrows, wcol, ncols, out_dtype, bm, bn, nk=1, name, stacked=None, out_into=None,
            side_srcs=()):
    kdim = a.shape[1]
    bk = kdim // nk
    cast = w.dtype != BF16
    assert nk == 1 or not cast
    acc_scratch = nk > 1 and out_dtype != F32
    r0 = row0 // bm
    nm = nrows // bm
    grid = (ncols // bn, nm, nk)
    sides = [_SideCast(src, grid[0] * nm * nk, first, n) for src, first, n in side_srcs]
    in_specs = [
        pl.BlockSpec((bm, bk), lambda j, i, k: (r0 + i, k)),
        pl.BlockSpec((None, bk, bn), lambda j, i, k: (layer, k, wcol(j))),
    ]
    args = [a, w]
    aliases = {}
    if out_into is not None:
        aliases = {len(args): 0}
        in_specs.append(pl.BlockSpec(memory_space=pl.ANY))
        args.append(out_into)
    if stacked is None:
        out_specs = [pl.BlockSpec((bm, bn), lambda j, i, k: (i, j))]
        out_shape = [jax.ShapeDtypeStruct((nrows, ncols), out_dtype)]
    else:
        n_seq, depth, seq, out_layer = stacked
        out_specs = [pl.BlockSpec((bm // seq, None, seq, bn), lambda j, i, k: (i, out_layer, 0, j))]
        out_shape = [jax.ShapeDtypeStruct((n_seq, depth, seq, ncols), out_dtype)]
    for sc in sides:
        i_spec, o_spec, o_shape = sc.specs(lambda j, i, k: (j * nm + i) * nk + k)
        in_specs.append(i_spec)
        args.append(sc.src3)
        out_specs.append(o_spec)
        out_shape.append(o_shape)
    scratch = ([pltpu.VMEM((bk, bn), BF16)] if cast else []) + ([pltpu.VMEM((bm, bn), F32)] if acc_scratch else [])
    outs = pl.pallas_call(
        functools.partial(_mm_kernel, nk=nk, nm=nm, cast=cast, aliased=out_into is not None,
                          acc_scratch=acc_scratch, side_counts=tuple(sc.count for sc in sides)),
        grid=grid,
        in_specs=in_specs,
        out_specs=out_specs,
        out_shape=out_shape,
        scratch_shapes=scratch,
        input_output_aliases=aliases,
        compiler_params=_cparams(("arbitrary", "arbitrary", "arbitrary"), 56),
        name=name,
    )(*args)
    return outs[0] if not sides else outs


def _swiglu_kernel(*refs, half, side_counts):
    it = iter(refs)
    a_ref, wa_ref, wb_ref = next(it), next(it), next(it)
    side_in = [next(it) for _ in side_counts]
    o_ref = next(it)
    side_out = [next(it) for _ in side_counts]
    _side_cast_step(pl.program_id(0) * half + pl.program_id(1), list(zip(side_in, side_out, side_counts)))
    x = a_ref[...]
    ga = jnp.dot(x, wa_ref[...], preferred_element_type=F32)
    gb = jnp.dot(x, wb_ref[...], preferred_element_type=F32)
    o_ref[...] = (_silu(ga) * gb).astype(BF16)


def _swiglu_up(h, w_up_b, layer, hidden, *, bm, bn, side_srcs=()):
    m, d = h.shape
    half = hidden // bn
    nm = m // bm
    sides = [_SideCast(src, half * nm, first, n) for src, first, n in side_srcs]
    in_specs = [
        pl.BlockSpec((bm, d), lambda i, j: (i, 0)),
        pl.BlockSpec((None, d, bn), lambda i, j: (layer, 0, j)),
        pl.BlockSpec((None, d, bn), lambda i, j: (layer, 0, half + j)),
    ]
    args = [h, w_up_b, w_up_b]
    out_specs = [pl.BlockSpec((bm, bn), lambda i, j: (i, j))]
    out_shape = [jax.ShapeDtypeStruct((m, hidden), BF16)]
    for sc in sides:
        i_spec, o_spec, o_shape = sc.specs(lambda i, j: i * half + j)
        in_specs.append(i_spec)
        args.append(sc.src3)
        out_specs.append(o_spec)
        out_shape.append(o_shape)
    outs = pl.pallas_call(
        functools.partial(_swiglu_kernel, half=half, side_counts=tuple(sc.count for sc in sides)),
        grid=(nm, half),
        in_specs=in_specs,
        out_specs=out_specs,
        out_shape=out_shape,
        compiler_params=_cparams(("arbitrary", "arbitrary"), 56),
        name="swiglu_up",
    )(*args)
    return outs[0] if not sides else outs


def _merge_kernel(r_ref, m_ref, d_ref, w_ref, g0_ref, g1_ref, g2_ref, o_ref):
    acc = jax.nn.sigmoid(g0_ref[...].astype(F32)) * jnp.dot(r_ref[...], w_ref[0], preferred_element_type=F32)
    acc += jax.nn.sigmoid(g1_ref[...].astype(F32)) * jnp.dot(m_ref[...], w_ref[1], preferred_element_type=F32)
    acc += jax.nn.sigmoid(g2_ref[...].astype(F32)) * jnp.dot(d_ref[...], w_ref[2], preferred_element_type=F32)
    o_ref[...] = acc.astype(BF16)


def _merge(ret_o, mlp_o, diff_o, w_branch, layer, z, gate_col0, d, *, bm, bn):
    m = ret_o.shape[0]
    branch = pl.BlockSpec((bm, BRANCH_W), lambda j, i: (i, 0))

    def gate_spec(t):
        off = (gate_col0 + t * d) // bn
        return pl.BlockSpec((bm, bn), lambda j, i: (i, off + j))

    return pl.pallas_call(
        _merge_kernel,
        grid=(d // bn, m // bm),
        in_specs=[branch, branch, branch,
                  pl.BlockSpec((None, 3, BRANCH_W, bn), lambda j, i: (layer, 0, 0, j)),
                  gate_spec(0), gate_spec(1), gate_spec(2)],
        out_specs=pl.BlockSpec((bm, bn), lambda j, i: (i, j)),
        out_shape=jax.ShapeDtypeStruct((m, d), BF16),
        compiler_params=_cparams(("arbitrary", "arbitrary"), 56),
        name="merge",
    )(ret_o, mlp_o, diff_o, w_branch, z, z, z)


def _rope_tables(n_tok):
    n_rows = n_tok // GRID_W
    rows = jnp.repeat(jnp.arange(n_rows, dtype=F32), GRID_W)
    cols = jnp.tile(jnp.arange(GRID_W, dtype=F32), n_rows)
    axis_dim = DK // 2
    inv = ROPE_BASE ** (-jnp.arange(0, axis_dim, 2, dtype=F32) / axis_dim)
    ang = jnp.stack([rows[:, None] * inv, cols[:, None] * inv], axis=1)
    cos, sin = jnp.cos(ang), jnp.sin(ang)
    zero = jnp.zeros_like(sin)
    cos_t = jnp.concatenate([cos, cos], axis=-1).reshape(n_tok, DK)
    sin_lo = jnp.concatenate([-sin, zero], axis=-1).reshape(n_tok, DK)
    sin_hi = jnp.concatenate([zero, sin], axis=-1).reshape(n_tok, DK)
    return jnp.stack([cos_t, sin_lo, sin_hi], axis=0)


def _rope(x, cos_t, sin_lo, sin_hi):
    return x * cos_t + pltpu.roll(x, DK - 32, 1) * sin_lo + pltpu.roll(x, 32, 1) * sin_hi


def _log_sigmoid(x):
    return jnp.minimum(x, 0.0) - jnp.log(1.0 + jnp.exp(-jnp.abs(x)))


def _retention_kernel(*refs, n_chunks, hg, rope, has_s0, emit_state, n_aliased):
    it = iter(refs)
    q_ref, k_ref, v_ref, g_ref, logit_ref = next(it), next(it), next(it), next(it), next(it)
    rope_ref = next(it) if rope else None
    s0_ref = next(it) if has_s0 else None
    for _ in range(n_aliased):
        next(it)
    o_ref = next(it)
    st_ref = next(it) if emit_state else None
    ks_scr, sb_scr = next(it), next(it)

    row = lax.broadcasted_iota(jnp.int32, (CHUNK, CHUNK), 0).astype(F32)
    col = lax.broadcasted_iota(jnp.int32, (CHUNK, CHUNK), 1).astype(F32)
    rel = row - col
    row_v = lax.broadcasted_iota(jnp.int32, (CHUNK, DV), 0).astype(F32)
    k_scale = DK ** -0.5

    def rows(c):
        if isinstance(c, int):
            return pl.ds(c * CHUNK, CHUNK)
        return pl.ds(pl.multiple_of(c * CHUNK, CHUNK), CHUNK)

    def maybe_rope(x, r):
        if not rope:
            return x
        return _rope(x, rope_ref[0, r, :], rope_ref[1, r, :], rope_ref[2, r, :])

    def loop(lo, hi, body, init, reverse=False):
        if n_chunks <= 4:
            carry = init
            for c in (range(hi - 1, lo - 1, -1) if reverse else range(lo, hi)):
                carry = body(c, carry)
            return carry
        if reverse:
            return lax.fori_loop(lo, hi, lambda t, cr: body(hi - 1 - t + lo, cr), init)
        return lax.fori_loop(lo, hi, body, init)

    consts = []
    for hh in range(hg):
        lg_f = _log_sigmoid(logit_ref[0, hh])[0:1, :]
        lg_b = _log_sigmoid(logit_ref[1, hh])[0:1, :]
        lg_fv = jnp.concatenate([lg_f, lg_f], axis=1)
        lg_bv = jnp.concatenate([lg_b, lg_b], axis=1)
        consts.append(dict(
            decay=(jnp.where(rel >= 0, jnp.exp(lg_f * jnp.maximum(rel, 0.0)), 0.0)
                   + jnp.where(rel <= 0, jnp.exp(lg_b * jnp.maximum(-rel, 0.0)), 0.0)),
            read_f=jnp.exp(lg_fv * (row_v + 1.0)),
            read_b=jnp.exp(lg_bv * (CHUNK - row_v)),
            write_f=jnp.exp(lg_f * (CHUNK - 1.0 - row)),
            write_b=jnp.exp(lg_b * row),
            cd_f=jnp.exp(lg_fv * float(CHUNK)),
            cd_b=jnp.exp(lg_bv * float(CHUNK)),
        ))
    if has_s0:
        s0_f = tuple(s0_ref[0, hh] for hh in range(hg))
        s0_b = tuple(s0_ref[1, hh] for hh in range(hg))
    else:
        s0_f = s0_b = tuple(jnp.zeros((DK, DV), F32) for _ in range(hg))

    def bwd_body(c, s_b):
        r = rows(c)
        out = []
        for hh, cst in enumerate(consts):
            kq = slice(hh * DK, (hh + 1) * DK)
            vq = slice(hh * DV, (hh + 1) * DV)
            ks = maybe_rope(k_ref[r, kq].astype(F32), r) * k_scale
            ks_scr[r, kq] = ks.astype(BF16)
            sb_scr[hh, c] = s_b[hh].astype(BF16)
            kw = (ks * cst["write_b"]).astype(BF16)
            out.append(s_b[hh] * cst["cd_b"] + lax.dot_general(
                kw, v_ref[r, vq], (((0,), (0,)), ((), ())), preferred_element_type=F32))
        return tuple(out)

    s_b = loop(0, n_chunks, bwd_body, s0_b, reverse=True)

    def fwd_body(c, s_f):
        r = rows(c)
        out = []
        for hh, cst in enumerate(consts):
            kq = slice(hh * DK, (hh + 1) * DK)
            vq = slice(hh * DV, (hh + 1) * DV)
            q = maybe_rope(q_ref[r, kq].astype(F32), r).astype(BF16)
            ks = ks_scr[r, kq]
            v = v_ref[r, vq]
            att = lax.dot_general(q, ks, (((1,), (1,)), ((), ())), preferred_element_type=F32) * cst["decay"]
            o = jnp.dot(att.astype(BF16), v, preferred_element_type=F32)
            o += jnp.dot(q, s_f[hh].astype(BF16), preferred_element_type=F32) * cst["read_f"]
            o += jnp.dot(q, sb_scr[hh, c], preferred_element_type=F32) * cst["read_b"]
            oc = o - jnp.mean(o, axis=-1, keepdims=True)
            on = oc * lax.rsqrt(jnp.mean(oc * oc, axis=-1, keepdims=True) + EPS)
            o_ref[r, vq] = (_silu(g_ref[r, vq].astype(F32)) * on).astype(BF16)
            kw = (ks.astype(F32) * cst["write_f"]).astype(BF16)
            out.append(s_f[hh] * cst["cd_f"] + lax.dot_general(
                kw, v, (((0,), (0,)), ((), ())), preferred_element_type=F32))
        return tuple(out)

    s_f = loop(0, n_chunks, fwd_body, s0_f)
    if emit_state:
        for hh in range(hg):
            st_ref[0, hh] = s_f[hh]
            st_ref[1, hh] = s_b[hh]


def _retention(z, logit_b, layer, *, row0, n_seq, seq, hg, rope_tab=None, state_in=None, state_depth=None,
               out_into=None, state_into=None):
    emit_state = state_depth is not None
    n_chunks = seq // CHUNK
    sb = row0 // seq
    kw, vw = hg * DK, hg * DV
    in_specs = [
        pl.BlockSpec((seq, kw), lambda b, h: (sb + b, h)),
        pl.BlockSpec((seq, kw), lambda b, h: (sb + b, HEADS * DK // kw + h)),
        pl.BlockSpec((seq, vw), lambda b, h: (sb + b, 2 * HEADS * DK // vw + h)),
        pl.BlockSpec((seq, vw), lambda b, h: (sb + b, (2 * HEADS * DK + BRANCH_W) // vw + h)),
        pl.BlockSpec((None, 2, hg, 8, 128), lambda b, h: (layer, 0, h, 0, 0)),
    ]
    args = [z, z, z, z, logit_b]
    if rope_tab is not None:
        in_specs.append(pl.BlockSpec((3, seq, DK), lambda b, h: (0, 0, 0)))
        args.append(rope_tab)
    if state_in is not None:
        in_specs.append(pl.BlockSpec((None, None, 2, hg, DK, DV), lambda b, h: (b, layer, 0, h, 0, 0)))
        args.append(state_in)
    aliases = {}
    for out_idx, arr in ((0, out_into), (1, state_into)):
        if arr is not None:
            aliases[len(args)] = out_idx
            in_specs.append(pl.BlockSpec(memory_space=pl.ANY))
            args.append(arr)
    out_specs = [pl.BlockSpec((seq, vw), lambda b, h: (sb + b, h))]
    out_shape = [jax.ShapeDtypeStruct((z.shape[0], BRANCH_W), BF16)]
    if emit_state:
        out_specs.append(pl.BlockSpec((None, None, 2, hg, DK, DV), lambda b, h: (b, layer, 0, h, 0, 0)))
        out_shape.append(jax.ShapeDtypeStruct((n_seq, state_depth, 2, HEADS, DK, DV), F32))
    return pl.pallas_call(
        functools.partial(_retention_kernel, n_chunks=n_chunks, hg=hg, rope=rope_tab is not None,
                          has_s0=state_in is not None, emit_state=emit_state, n_aliased=len(aliases)),
        grid=(n_seq, HEADS // hg),
        in_specs=in_specs,
        out_specs=out_specs,
        out_shape=out_shape,
        scratch_shapes=[pltpu.VMEM((seq, kw), BF16), pltpu.VMEM((hg, n_chunks, DK, DV), BF16)],
        input_output_aliases=aliases,
        compiler_params=_cparams(("arbitrary", "arbitrary"), 48),
        name="retention",
    )(*args)


def _gmlp_kernel(u_ref, v_ref, ng_ref, ws_ref, bs_ref, o_ref, *, n_chunks):
    vn = (_rms(_gelu_tanh(v_ref[...].astype(F32))) * ng_ref[...]).astype(BF16)
    for t in range(n_chunks):
        r = slice(t * CHUNK, (t + 1) * CHUNK)
        for g in range(HEADS):
            cs = slice(g * DV, (g + 1) * DV)
            mixed = jnp.dot(ws_ref[g].astype(BF16), vn[r, cs], preferred_element_type=F32) + bs_ref[g]
            o_ref[r, cs] = (_gelu_tanh(u_ref[r, cs].astype(F32)) * mixed).astype(BF16)


def _gmlp(z, mlp_norm_g3, mlp_ws, bs_b, layer, *, u_col0):
    m = z.shape[0]
    bt = _pick(m, (512, 256, 128))
    ub = u_col0 // BRANCH_W
    return pl.pallas_call(
        functools.partial(_gmlp_kernel, n_chunks=bt // CHUNK),
        grid=(m // bt,),
        in_specs=[
            pl.BlockSpec((bt, BRANCH_W), lambda i: (i, ub)),
            pl.BlockSpec((bt, BRANCH_W), lambda i: (i, ub + 1)),
            pl.BlockSpec((None, 1, BRANCH_W), lambda i: (layer, 0, 0)),
            pl.BlockSpec((None, HEADS, CHUNK, CHUNK), lambda i: (layer, 0, 0, 0)),
            pl.BlockSpec((None, HEADS, CHUNK, DV), lambda i: (layer, 0, 0, 0)),
        ],
        out_specs=pl.BlockSpec((bt, BRANCH_W), lambda i: (i, 0)),
        out_shape=jax.ShapeDtypeStruct((m, BRANCH_W), BF16),
        compiler_params=_cparams(("arbitrary",), 40),
        name="gmlp",
    )(z, z, mlp_norm_g3, mlp_ws, bs_b)


Q_SCALE = DK ** -0.5 * LOG2E


def _attn_prep_kernel(q_ref, k_ref, kc_ref, rope_ref, qo_ref, ko_ref, *, cache_blocks):
    r = pl.program_id(1)

    @pl.when(r < cache_blocks)
    def _():
        ko_ref[...] = kc_ref[...].astype(BF16)

    @pl.when(r >= cache_blocks)
    def _():
        cos_t, sin_lo, sin_hi = rope_ref[0], rope_ref[1], rope_ref[2]
        for g in range(BRANCH_W // DK):
            gs = slice(g * DK, (g + 1) * DK)
            ko_ref[:, gs] = _rope(k_ref[:, gs].astype(F32), cos_t, sin_lo, sin_hi).astype(BF16)
            qo_ref[:, gs] = (_rope(q_ref[:, gs].astype(F32), cos_t, sin_lo, sin_hi) * Q_SCALE).astype(BF16)


def _attn_prep(z, q_col0, row0, kv_lat, cache_k4, layer, rope_tab, *, n_seq, seq):
    past = cache_k4.shape[2]
    tr = _pick(math.gcd(past, seq), (512, 256, 128))
    cb, nb = past // tr, seq // tr
    qb0, qc0 = row0 // tr, q_col0 // BRANCH_W

    def new_blk(r):
        return jnp.maximum(r - cb, 0)

    return pl.pallas_call(
        functools.partial(_attn_prep_kernel, cache_blocks=cb),
        grid=(n_seq, cb + nb),
        in_specs=[
            pl.BlockSpec((tr, BRANCH_W), lambda b, r: (qb0 + b * nb + new_blk(r), qc0)),
            pl.BlockSpec((tr, BRANCH_W), lambda b, r: (b * nb + new_blk(r), 0)),
            pl.BlockSpec((None, None, tr, BRANCH_W), lambda b, r: (b, layer, jnp.minimum(r, cb - 1), 0)),
            pl.BlockSpec((3, tr, DK), lambda b, r: (0, new_blk(r), 0)),
        ],
        out_specs=[
            pl.BlockSpec((tr, BRANCH_W), lambda b, r: (b * nb + new_blk(r), 0)),
            pl.BlockSpec((None, tr, BRANCH_W), lambda b, r: (b, r, 0)),
        ],
        out_shape=[jax.ShapeDtypeStruct((n_seq * seq, BRANCH_W), BF16),
                   jax.ShapeDtypeStruct((n_seq, past + seq, BRANCH_W), BF16)],
        compiler_params=_cparams(("arbitrary", "arbitrary"), 48),
        name="attn_prep",
    )(z, kv_lat, cache_k4, rope_tab)


def _diff_attn_kernel(*refs, tq, n_new, n_cache, hg, prepped, lam_init, aliased):
    it = iter(refs)
    q_ref, k_ref, v_ref = next(it), next(it), next(it)
    vc_ref = next(it) if n_cache else None
    lam_ref, sg_ref = next(it), next(it)
    if aliased:
        next(it)
    o_ref = next(it)
    k_scr = None if prepped else next(it)
    v_scr = next(it)
    sub = min(tq, 256)

    @pl.when(pl.program_id(2) == 0)
    def _():
        for hh in range(hg):
            hs = slice(hh * DV, (hh + 1) * DV)
            if not prepped:
                k_scr[hh] = k_ref[:, hs].astype(BF16)
            if n_cache:
                v_scr[hh, 0:n_cache, :] = vc_ref[:, hs].astype(BF16)
            v_scr[hh, n_cache:n_cache + n_new, :] = v_ref[:, hs].astype(BF16)

    lv = lam_ref[...]
    lam = (jnp.exp(jnp.sum(lv[0:1] * lv[1:2], axis=-1, keepdims=True))
           - jnp.exp(jnp.sum(lv[2:3] * lv[3:4], axis=-1, keepdims=True)) + lam_init)
    for hh, r0 in [(hh, r0) for hh in range(hg) for r0 in range(0, tq, sub)]:
        rs = slice(r0, r0 + sub)
        pvs, invs = [], []
        for c in range(2):
            cs = slice(hh * DV + c * DK, hh * DV + (c + 1) * DK)
            if prepped:
                q, k = q_ref[rs, cs], k_ref[:, cs]
            else:
                q = (q_ref[rs, cs].astype(F32) * Q_SCALE).astype(BF16)
                k = k_scr[hh, :, c * DK:(c + 1) * DK]
            s = lax.dot_general(q, k, (((1,), (1,)), ((), ())), preferred_element_type=F32)
            e = jnp.exp2(s - jnp.max(s, axis=-1, keepdims=True))
            invs.append(1.0 / jnp.sum(e, axis=-1, keepdims=True))
            pvs.append(jnp.dot(e.astype(BF16), v_scr[hh], preferred_element_type=F32))
        o = pvs[0] * invs[0] - pvs[1] * (lam * invs[1])
        o_ref[rs, hh * DV:(hh + 1) * DV] = (_rms(o) * sg_ref[...] * (1.0 - lam_init)).astype(BF16)


def _diff_attn(q_arr, q_spec, k_arr, k_spec, v_arr, v_spec, layer, diff_lambda, subln_g3, lam_init, *,
               out_rows, row0, n_seq, seq, n_keys, hg, tq, prepped, cache_v=None, out_into=None):
    n_cache = 0 if cache_v is None else cache_v.shape[2]
    nq = seq // tq
    w = hg * DV
    in_specs = [q_spec, k_spec, v_spec]
    args = [q_arr, k_arr, v_arr]
    if n_cache:
        in_specs.append(pl.BlockSpec((None, None, n_cache, w), lambda b, h, i: (b, layer, 0, h)))
        args.append(cache_v)
    in_specs += [pl.BlockSpec((None, 4, DK), lambda b, h, i: (layer, 0, 0)),
                 pl.BlockSpec((None, 1, DV), lambda b, h, i: (layer, 0, 0))]
    args += [diff_lambda, subln_g3]
    aliases = {}
    if out_into is not None:
        aliases = {len(args): 0}
        in_specs.append(pl.BlockSpec(memory_space=pl.ANY))
        args.append(out_into)
    ob0 = row0 // tq
    scratch = [] if prepped else [pltpu.VMEM((hg, n_keys, 2 * DK), BF16)]
    scratch.append(pltpu.VMEM((hg, n_keys, DV), BF16))
    return pl.pallas_call(
        functools.partial(_diff_attn_kernel, tq=tq, n_new=seq, n_cache=n_cache, hg=hg, prepped=prepped,
                          lam_init=lam_init, aliased=out_into is not None),
        grid=(n_seq, HEADS // hg, nq),
        in_specs=in_specs,
        out_specs=pl.BlockSpec((tq, w), lambda b, h, i: (ob0 + b * nq + i, h)),
        out_shape=jax.ShapeDtypeStruct((out_rows, BRANCH_W), BF16),
        scratch_shapes=scratch,
        input_output_aliases=aliases,
        compiler_params=_cparams(("arbitrary", "arbitrary", "arbitrary"), 56),
        name="diff_attn",
    )(*args)


def kernel(x_prompt, x_sample, cache_k, cache_v, state_ret, c, c_ctx, w_mod, b_mod, norm_g, w_in,
           ret_decay_logit, mlp_norm_g, mlp_ws, mlp_bs, diff_lambda, diff_subln_g, w_branch, w_o,
           w_up, w_down):
    batch, seq, d = x_prompt.shape
    dec_batch, dec_seq, _ = x_sample.shape
    depth = w_in.shape[0]
    hidden = w_down.shape[1]
    past = cache_k.shape[2]
    n_ctx = batch * seq
    n_lat = dec_batch * dec_seq
    tok = _Tokens(n_ctx, dec_batch, dec_seq)
    m = tok.total

    c_rq, c_rk, c_rv, c_rg = 0, HEADS * DK, 2 * HEADS * DK, 2 * HEADS * DK + BRANCH_W
    c_mu = c_rg + BRANCH_W
    c_dq = c_mu + 2 * BRANCH_W
    c_dk = c_dq + BRANCH_W
    c_dv = c_dk + BRANCH_W
    n_main = c_dk + 3 * d

    bm = _pick(math.gcd(n_ctx, n_lat), (1024, 512, 256, 128))
    assert bm % seq == 0
    bn = _pick(math.gcd(d, BRANCH_W), (1024, 512, 256, 128))
    bm_half = _pick(m, (512, 256, 128))
    n_keys = past + dec_seq
    tq_lat = _pick(dec_seq, (1024, 512, 256, 128))

    w_in_bs = [w_in[0:1].astype(BF16)] + [None] * (depth - 1)
    w_up_bs = [None] * depth
    w_branch_b = w_o_b = w_down_b = None
    cond = jnp.zeros((N_COND, d), F32).at[0].set(c_ctx).at[1:1 + dec_batch].set(c)
    b_mod3 = b_mod.reshape(depth, 1, 6 * d)
    logit_b = jnp.broadcast_to(ret_decay_logit[:, :, :, None, None], (depth, 2, HEADS, 8, 128))
    bs_b = jnp.broadcast_to(mlp_bs[:, :, :, None], (depth, HEADS, CHUNK, DV))
    mlp_norm_g3 = mlp_norm_g.reshape(depth, 1, BRANCH_W)
    subln_g3 = diff_subln_g.reshape(depth, 1, DV)
    cache_k4 = cache_k.reshape(dec_batch, depth, past, BRANCH_W)
    cache_v4 = cache_v.reshape(dec_batch, depth, past, BRANCH_W)
    rope_tab = _rope_tables(dec_seq)

    mods = [jnp.transpose(_modulation(cond, w_mod, b_mod3, l), (1, 0, 2)) for l in range(depth)]
    x_parts = [x_prompt.reshape(n_ctx, d), x_sample.reshape(n_lat, d)]
    h, = _token_call(tok, x_parts, d, norm_g=norm_g, layer_b=0, mod_b=mods[0], gn=0, shift=0, scale=1)

    new_k = new_v = new_s = None
    main_split = c_dk // bn
    w_ctx = 4 * DV
    nq_lat = dec_seq // tq_lat
    for l in range(depth):
        lam_init = 0.8 - 0.6 * math.exp(-0.3 * l)
        w_in_b = w_in_bs[l]
        main_kw = dict(row0=0, nrows=m, ncols=n_main, out_dtype=BF16, bm=bm, bn=bn, name="w_in_main",
                       wcol=lambda j: jnp.where(j < main_split, j, j + 2 * BRANCH_W // bn))
        if l == 0:
            side = [(w_branch, 0, None), (w_o, 0, None), (w_up, 0, d)]
            z, w_branch_b, w_o_b, w_up0 = _matmul(h, w_in_b, 0, side_srcs=side, **main_kw)
            w_branch_b = w_branch_b.reshape(w_branch.shape)
            w_o_b = w_o_b.reshape(w_o.shape)
            w_up_bs[0] = w_up0.reshape(1, d, 2 * hidden)
        else:
            z = _matmul(h, w_in_b, 0, **main_kw)
        new_k = _matmul(h, w_in_b, 0, row0=0, nrows=n_ctx, ncols=BRANCH_W, out_dtype=F32, bm=bm, bn=bn,
                        wcol=lambda j: c_dk // bn + j, name="w_in_dk_ctx", stacked=(batch, depth, seq, l),
                        out_into=new_k)
        new_v = _matmul(h, w_in_b, 0, row0=0, nrows=n_ctx, ncols=BRANCH_W, out_dtype=F32, bm=bm, bn=bn,
                        wcol=lambda j: c_dv // bn + j, name="w_in_dv_ctx", stacked=(batch, depth, seq, l),
                        out_into=new_v)
        kv_lat = _matmul(h, w_in_b, 0, row0=n_ctx, nrows=n_lat, ncols=2 * BRANCH_W, out_dtype=BF16, bm=bm,
                         bn=bn, wcol=lambda j: c_dk // bn + j, name="w_in_kv_lat")

        ret_o, new_s = _retention(z, logit_b, l, row0=0, n_seq=batch, seq=seq, hg=HEADS, state_depth=depth,
                                  state_into=new_s)
        ret_o, = _retention(z, logit_b, l, row0=n_ctx, n_seq=dec_batch, seq=dec_seq, hg=2,
                            rope_tab=rope_tab, state_in=state_ret, out_into=ret_o)

        mlp_o = _gmlp(z, mlp_norm_g3, mlp_ws, bs_b, l, u_col0=c_mu)

        stacked_spec = pl.BlockSpec((None, None, seq, w_ctx), lambda b, hq, i: (b, l, 0, hq))
        diff_o = _diff_attn(
            z, pl.BlockSpec((seq, w_ctx), lambda b, hq, i: (b, c_dq // w_ctx + hq)),
            new_k, stacked_spec, new_v, stacked_spec, l, diff_lambda, subln_g3, lam_init,
            out_rows=m, row0=0, n_seq=batch, seq=seq, n_keys=seq, hg=4, tq=seq, prepped=False)
        q_lat, k_all = _attn_prep(z, c_dq, n_ctx, kv_lat, cache_k4, l, rope_tab, n_seq=dec_batch, seq=dec_seq)
        diff_o = _diff_attn(
            q_lat, pl.BlockSpec((tq_lat, DV), lambda b, hq, i: (b * nq_lat + i, hq)),
            k_all, pl.BlockSpec((None, n_keys, DV), lambda b, hq, i: (b, 0, hq)),
            kv_lat, pl.BlockSpec((dec_seq, DV), lambda b, hq, i: (b, HEADS + hq)),
            l, diff_lambda, subln_g3, lam_init, out_rows=m, row0=n_ctx, n_seq=dec_batch, seq=dec_seq,
            n_keys=n_keys, hg=1, tq=tq_lat, prepped=True, cache_v=cache_v4, out_into=diff_o)

        merged = _merge(ret_o, mlp_o, diff_o, w_branch_b, l, z, c_dk, d, bm=bm_half, bn=bn)
        y = _matmul(merged, w_o_b, l, row0=0, nrows=m, ncols=d, out_dtype=BF16, bm=bm, bn=bn,
                    wcol=lambda j: j, name="w_o")
        x, h2 = _token_call(tok, x_parts, d, y=y, norm_g=norm_g, layer_a=l, mod_a=mods[l], layer_b=l,
                            mod_b=mods[l], gy=1, gate=2, gn=2, shift=3, scale=4)
        x_parts = [x]

        up_kw = dict(bm=bm, bn=_pick(hidden, (256, 128)))
        if l == 0:
            side = [(w_down, 0, None)] + [(w, i * d, d) for i in range(1, depth) for w in (w_in, w_up)]
            act, w_down_b, *rest = _swiglu_up(h2, w_up_bs[0], 0, hidden, side_srcs=side, **up_kw)
            w_down_b = w_down_b.reshape(w_down.shape)
            w_in_bs[1:] = [r.reshape(1, d, w_in.shape[2]) for r in rest[0::2]]
            w_up_bs[1:] = [r.reshape(1, d, 2 * hidden) for r in rest[1::2]]
        else:
            act = _swiglu_up(h2, w_up_bs[l], 0, hidden, **up_kw)
        f = _matmul(act, w_down_b, l, row0=0, nrows=m, ncols=d, out_dtype=BF16, bm=bm_half,
                    bn=_pick(d, (512, 256, 128)), wcol=lambda j: j, name="w_down")
        if l + 1 < depth:
            x, h = _token_call(tok, x_parts, d, y=f, norm_g=norm_g, layer_a=l, mod_a=mods[l], layer_b=l + 1,
                               mod_b=mods[l + 1], gy=3, gate=5, gn=0, shift=0, scale=1)
            x_parts = [x]
        else:
            y_ctx, y_lat = _token_call(tok, x_parts, d, y=f, norm_g=norm_g, layer_a=l, mod_a=mods[l],
                                       split_out=True, gy=3, gate=5)

    return (y_ctx.reshape(batch, seq, d), y_lat.reshape(dec_batch, dec_seq, d),
            new_k.reshape(batch, depth, seq, HEADS, 2, DK), new_v.reshape(batch, depth, seq, HEADS, DV), new_s)
```

```python
import functools
import math

import jax
import jax.numpy as jnp
from jax import lax
from jax.experimental import pallas as pl
from jax.experimental.pallas import tpu as pltpu

F32 = jnp.float32
BF16 = jnp.bfloat16

HEADS = 8
DK = 128
DV = 256
CHUNK = 128
BRANCH_W = HEADS * DV
GRID_W = 64
ROPE_BASE = 10000.0
EPS = 1e-6
N_COND = 16
MIB = 1 << 20
LOG2E = 1.4426950408889634


def _pick(n, prefs):
    for p in prefs:
        if n % p == 0:
            return p
    raise ValueError(f"no block size in {prefs} divides {n}")


def _cparams(dims, vmem_mib):
    return pltpu.CompilerParams(dimension_semantics=dims, vmem_limit_bytes=vmem_mib * MIB)


def _silu(x):
    return x * jax.nn.sigmoid(x)


def _gelu_tanh(x):
    return 0.5 * x * (1.0 + jnp.tanh(math.sqrt(2.0 / math.pi) * (x + 0.044715 * (x * x * x))))


def _rms(x):
    return x * lax.rsqrt(jnp.mean(x * x, axis=-1, keepdims=True) + EPS)


def _mod_kernel(c_ref, w_ref, b_ref, o_ref):
    s = _silu(c_ref[...]).astype(BF16)
    o_ref[...] = jnp.dot(s, w_ref[...].astype(BF16), preferred_element_type=F32) + b_ref[...]


def _modulation(cond, w_mod, b_mod3, layer):
    n_cond, d = cond.shape
    bn = _pick(d, (512, 256, 128))
    per_seg = d // bn
    return pl.pallas_call(
        _mod_kernel,
        grid=(6 * per_seg,),
        in_specs=[
            pl.BlockSpec((n_cond, d), lambda j: (0, 0)),
            pl.BlockSpec((None, d, bn), lambda j: (layer, 0, j)),
            pl.BlockSpec((None, 1, bn), lambda j: (layer, 0, j)),
        ],
        out_specs=pl.BlockSpec((None, n_cond, bn), lambda j: (j // per_seg, 0, j % per_seg)),
        out_shape=jax.ShapeDtypeStruct((6, n_cond, d), F32),
        compiler_params=_cparams(("arbitrary",), 40),
        name="modulation",
    )(cond, w_mod, b_mod3)


class _Tokens:
    def __init__(self, n_ctx, n_dec_seq, dec_seq):
        self.n_ctx = n_ctx
        self.n_dec_seq = n_dec_seq
        self.dec_seq = dec_seq
        self.total = n_ctx + n_dec_seq * dec_seq
        self.br = _pick(math.gcd(n_ctx, dec_seq), (256, 128))
        self.ctx_blocks = n_ctx // self.br

    def cond_index(self, i):
        per = self.dec_seq // self.br
        return jnp.where(i < self.ctx_blocks, 0, 1 + (i - self.ctx_blocks) // per)

    def split_specs(self, d):
        nc = self.ctx_blocks
        return [pl.BlockSpec((self.br, d), lambda i: (jnp.minimum(i, nc - 1), 0)),
                pl.BlockSpec((self.br, d), lambda i: (jnp.maximum(i - nc, 0), 0))]


def _row_spec(tok, d):
    return pl.BlockSpec((tok.br, d), lambda i: (i, 0))


def _ng_spec(layer, d):
    return pl.BlockSpec((None, 4, d), lambda i: (layer, 0, 0))


def _mod_spec(tok, d):
    return pl.BlockSpec((None, 6, d), lambda i: (tok.cond_index(i), 0, 0))


def _token_kernel(*refs, ctx_blocks, split_in, split_out, has_y, has_h, gy, gate, gn, shift, scale):
    it = iter(refs)
    x_refs = [next(it), next(it)] if split_in else [next(it)]
    if has_y:
        y_ref, nga_ref, moda_ref = next(it), next(it), next(it)
    if has_h:
        ngb_ref, modb_ref = next(it), next(it)
    if has_y:
        xo_refs = [next(it), next(it)] if split_out else [next(it)]
    if has_h:
        h_ref = next(it)
    in_ctx = pl.program_id(0) < ctx_blocks

    x = jnp.where(in_ctx, x_refs[0][...], x_refs[1][...]) if split_in else x_refs[0][...]
    if has_y:
        x = x + moda_ref[gate:gate + 1, :] * (_rms(y_ref[...].astype(F32)) * nga_ref[gy:gy + 1, :])
        if split_out:
            @pl.when(in_ctx)
            def _():
                xo_refs[0][...] = x

            @pl.when(jnp.logical_not(in_ctx))
            def _():
                xo_refs[1][...] = x
        else:
            xo_refs[0][...] = x
    if has_h:
        hn = _rms(x) * ngb_ref[gn:gn + 1, :]
        h_ref[...] = (hn * (1.0 + modb_ref[scale:scale + 1, :]) + modb_ref[shift:shift + 1, :]).astype(BF16)


def _token_call(tok, x_parts, d, *, y=None, norm_g=None, layer_a=None, mod_a=None, layer_b=None, mod_b=None,
                split_out=False, gy=0, gate=0, gn=0, shift=0, scale=0):
    split_in = len(x_parts) == 2
    has_y, has_h = y is not None, mod_b is not None
    in_specs = tok.split_specs(d) if split_in else [_row_spec(tok, d)]
    args = list(x_parts)
    if has_y:
        in_specs += [_row_spec(tok, d), _ng_spec(layer_a, d), _mod_spec(tok, d)]
        args += [y, norm_g, mod_a]
    if has_h:
        in_specs += [_ng_spec(layer_b, d), _mod_spec(tok, d)]
        args += [norm_g, mod_b]
    out_specs, out_shape = [], []
    if has_y:
        if split_out:
            out_specs += tok.split_specs(d)
            out_shape += [jax.ShapeDtypeStruct((tok.n_ctx, d), F32),
                          jax.ShapeDtypeStruct((tok.total - tok.n_ctx, d), F32)]
        else:
            out_specs.append(_row_spec(tok, d))
            out_shape.append(jax.ShapeDtypeStruct((tok.total, d), F32))
    if has_h:
        out_specs.append(_row_spec(tok, d))
        out_shape.append(jax.ShapeDtypeStruct((tok.total, d), BF16))
    return pl.pallas_call(
        functools.partial(_token_kernel, ctx_blocks=tok.ctx_blocks, split_in=split_in, split_out=split_out,
                          has_y=has_y, has_h=has_h, gy=gy, gate=gate, gn=gn, shift=shift, scale=scale),
        grid=(tok.total // tok.br,),
        in_specs=in_specs,
        out_specs=out_specs,
        out_shape=out_shape,
        compiler_params=_cparams(("arbitrary",), 48),
        name="token_norm",
    )(*args)


class _SideCast:
    def __init__(self, src, steps, first_row=0, n_rows=None):
        cols = src.shape[-1]
        flat = src.reshape(-1, cols)
        n_rows = flat.shape[0] - first_row if n_rows is None else n_rows
        self.rows = next(r for r in (16, 32, 64, 96, 128, 192, 256, 384, 512, 768, 1024)
                         if n_rows % r == 0 and first_row % r == 0 and n_rows // r <= steps)
        self.cols = cols
        self.count = n_rows // self.rows
        self.first = first_row // self.rows
        self.src3 = flat.reshape(-1, self.rows, cols)

    def specs(self, step_of):
        blk = (None, self.rows, self.cols)

        def chunk(*ids):
            return jnp.minimum(step_of(*ids), self.count - 1)

        return (pl.BlockSpec(blk, lambda *ids: (self.first + chunk(*ids), 0, 0)),
                pl.BlockSpec(blk, lambda *ids: (chunk(*ids), 0, 0)),
                jax.ShapeDtypeStruct((self.count, self.rows, self.cols), BF16))


def _side_cast_step(step, sides):
    for in_ref, out_ref, count in sides:
        @pl.when(step < count)
        def _():
            out_ref[...] = in_ref[...].astype(BF16)


def _mm_kernel(*refs, nk, nm, cast, aliased, acc_scratch, side_counts):
    it = iter(refs)
    a_ref, w_ref = next(it), next(it)
    if aliased:
        next(it)
    side_in = [next(it) for _ in side_counts]
    o_ref = next(it)
    side_out = [next(it) for _ in side_counts]
    wb_ref = next(it) if cast else None
    acc_ref = next(it) if acc_scratch else None
    step = (pl.program_id(0) * nm + pl.program_id(1)) * nk + pl.program_id(2)
    _side_cast_step(step, list(zip(side_in, side_out, side_counts)))
    if cast:
        @pl.when(pl.program_id(1) == 0)
        def _():
            wb_ref[...] = w_ref[...].astype(BF16)
        w = wb_ref[...]
    else:
        w = w_ref[...]
    p = jnp.dot(a_ref[...], w, preferred_element_type=F32)
    if nk == 1:
        o_ref[...] = p.astype(o_ref.dtype).reshape(o_ref.shape)
        return
    k = pl.program_id(2)
    tgt = acc_ref if acc_scratch else o_ref

    @pl.when(k == 0)
    def _():
        tgt[...] = p

    @pl.when(k > 0)
    def _():
        tgt[...] += p

    if acc_scratch:
        @pl.when(k == nk - 1)
        def _():
            o_ref[...] = acc_ref[...].astype(o_ref.dtype)


def _matmul(a, w, layer, *, row0, nrows, wcol, ncols, out_dtype, bm, bn, nk=1, name, stacked=None, out_into=None,
            side_srcs=()):
    kdim = a.shape[1]
    bk = kdim // nk
    cast = w.dtype != BF16
    assert nk == 1 or not cast
    acc_scratch = nk > 1 and out_dtype != F32
    r0 = row0 // bm
    nm = nrows // bm
    grid = (ncols // bn, nm, nk)
    sides = [_SideCast(src, grid[0] * nm * nk, first, n) for src, first, n in side_srcs]
    in_specs = [
        pl.BlockSpec((bm, bk), lambda j, i, k: (r0 + i, k)),
        pl.BlockSpec((None, bk, bn), lambda j, i, k: (layer, k, wcol(j))),
    ]
    args = [a, w]
    aliases = {}
    if out_into is not None:
        aliases = {len(args): 0}
        in_specs.append(pl.BlockSpec(memory_space=pl.ANY))
        args.append(out_into)
    if stacked is None:
        out_specs = [pl.BlockSpec((bm, bn), lambda j, i, k: (i, j))]
        out_shape = [jax.ShapeDtypeStruct((nrows, ncols), out_dtype)]
    else:
        n_seq, depth, seq, out_layer = stacked
        out_specs = [pl.BlockSpec((bm // seq, None, seq, bn), lambda j, i, k: (i, out_layer, 0, j))]
        out_shape = [jax.ShapeDtypeStruct((n_seq, depth, seq, ncols), out_dtype)]
    for sc in sides:
        i_spec, o_spec, o_shape = sc.specs(lambda j, i, k: (j * nm + i) * nk + k)
        in_specs.append(i_spec)
        args.append(sc.src3)
        out_specs.append(o_spec)
        out_shape.append(o_shape)
    scratch = ([pltpu.VMEM((bk, bn), BF16)] if cast else []) + ([pltpu.VMEM((bm, bn), F32)] if acc_scratch else [])
    outs = pl.pallas_call(
        functools.partial(_mm_kernel, nk=nk, nm=nm, cast=cast, aliased=out_into is not None,
                          acc_scratch=acc_scratch, side_counts=tuple(sc.count for sc in sides)),
        grid=grid,
        in_specs=in_specs,
        out_specs=out_specs,
        out_shape=out_shape,
        scratch_shapes=scratch,
        input_output_aliases=aliases,
        compiler_params=_cparams(("arbitrary", "arbitrary", "arbitrary"), 56),
        name=name,
    )(*args)
    return outs[0] if not sides else outs


def _swiglu_kernel(*refs, half, side_counts):
    it = iter(refs)
    a_ref, wa_ref, wb_ref = next(it), next(it), next(it)
    side_in = [next(it) for _ in side_counts]
    o_ref = next(it)
    side_out = [next(it) for _ in side_counts]
    _side_cast_step(pl.program_id(0) * half + pl.program_id(1), list(zip(side_in, side_out, side_counts)))
    x = a_ref[...]
    ga = jnp.dot(x, wa_ref[...], preferred_element_type=F32)
    gb = jnp.dot(x, wb_ref[...], preferred_element_type=F32)
    o_ref[...] = (_silu(ga) * gb).astype(BF16)


def _swiglu_up(h, w_up_b, layer, hidden, *, bm, bn, side_srcs=()):
    m, d = h.shape
    half = hidden // bn
    nm = m // bm
    sides = [_SideCast(src, half * nm, first, n) for src, first, n in side_srcs]
    in_specs = [
        pl.BlockSpec((bm, d), lambda i, j: (i, 0)),
        pl.BlockSpec((None, d, bn), lambda i, j: (layer, 0, j)),
        pl.BlockSpec((None, d, bn), lambda i, j: (layer, 0, half + j)),
    ]
    args = [h, w_up_b, w_up_b]
    out_specs = [pl.BlockSpec((bm, bn), lambda i, j: (i, j))]
    out_shape = [jax.ShapeDtypeStruct((m, hidden), BF16)]
    for sc in sides:
        i_spec, o_spec, o_shape = sc.specs(lambda i, j: i * half + j)
        in_specs.append(i_spec)
        args.append(sc.src3)
        out_specs.append(o_spec)
        out_shape.append(o_shape)
    outs = pl.pallas_call(
        functools.partial(_swiglu_kernel, half=half, side_counts=tuple(sc.count for sc in sides)),
        grid=(nm, half),
        in_specs=in_specs,
        out_specs=out_specs,
        out_shape=out_shape,
        compiler_params=_cparams(("arbitrary", "arbitrary"), 56),
        name="swiglu_up",
    )(*args)
    return outs[0] if not sides else outs


def _merge_kernel(r_ref, m_ref, d_ref, w_ref, g0_ref, g1_ref, g2_ref, o_ref):
    acc = jax.nn.sigmoid(g0_ref[...].astype(F32)) * jnp.dot(r_ref[...], w_ref[0], preferred_element_type=F32)
    acc += jax.nn.sigmoid(g1_ref[...].astype(F32)) * jnp.dot(m_ref[...], w_ref[1], preferred_element_type=F32)
    acc += jax.nn.sigmoid(g2_ref[...].astype(F32)) * jnp.dot(d_ref[...], w_ref[2], preferred_element_type=F32)
    o_ref[...] = acc.astype(BF16)


def _merge(ret_o, mlp_o, diff_o, w_branch, layer, z, gate_col0, d, *, bm, bn):
    m = ret_o.shape[0]
    branch = pl.BlockSpec((bm, BRANCH_W), lambda j, i: (i, 0))

    def gate_spec(t):
        off = (gate_col0 + t * d) // bn
        return pl.BlockSpec((bm, bn), lambda j, i: (i, off + j))

    return pl.pallas_call(
        _merge_kernel,
        grid=(d // bn, m // bm),
        in_specs=[branch, branch, branch,
                  pl.BlockSpec((None, 3, BRANCH_W, bn), lambda j, i: (layer, 0, 0, j)),
                  gate_spec(0), gate_spec(1), gate_spec(2)],
        out_specs=pl.BlockSpec((bm, bn), lambda j, i: (i, j)),
        out_shape=jax.ShapeDtypeStruct((m, d), BF16),
        compiler_params=_cparams(("arbitrary", "arbitrary"), 56),
        name="merge",
    )(ret_o, mlp_o, diff_o, w_branch, z, z, z)


def _rope_tables(n_tok):
    n_rows = n_tok // GRID_W
    rows = jnp.repeat(jnp.arange(n_rows, dtype=F32), GRID_W)
    cols = jnp.tile(jnp.arange(GRID_W, dtype=F32), n_rows)
    axis_dim = DK // 2
    inv = ROPE_BASE ** (-jnp.arange(0, axis_dim, 2, dtype=F32) / axis_dim)
    ang = jnp.stack([rows[:, None] * inv, cols[:, None] * inv], axis=1)
    cos, sin = jnp.cos(ang), jnp.sin(ang)
    zero = jnp.zeros_like(sin)
    cos_t = jnp.concatenate([cos, cos], axis=-1).reshape(n_tok, DK)
    sin_lo = jnp.concatenate([-sin, zero], axis=-1).reshape(n_tok, DK)
    sin_hi = jnp.concatenate([zero, sin], axis=-1).reshape(n_tok, DK)
    return jnp.stack([cos_t, sin_lo, sin_hi], axis=0)


def _rope(x, cos_t, sin_lo, sin_hi):
    return x * cos_t + pltpu.roll(x, DK - 32, 1) * sin_lo + pltpu.roll(x, 32, 1) * sin_hi


def _log_sigmoid(x):
    return jnp.minimum(x, 0.0) - jnp.log(1.0 + jnp.exp(-jnp.abs(x)))


def _retention_kernel(*refs, n_chunks, hg, rope, has_s0, emit_state, n_aliased):
    it = iter(refs)
    q_ref, k_ref, v_ref, g_ref, logit_ref = next(it), next(it), next(it), next(it), next(it)
    rope_ref = next(it) if rope else None
    s0_ref = next(it) if has_s0 else None
    for _ in range(n_aliased):
        next(it)
    o_ref = next(it)
    st_ref = next(it) if emit_state else None
    ks_scr, sb_scr = next(it), next(it)

    row = lax.broadcasted_iota(jnp.int32, (CHUNK, CHUNK), 0).astype(F32)
    col = lax.broadcasted_iota(jnp.int32, (CHUNK, CHUNK), 1).astype(F32)
    rel = row - col
    row_v = lax.broadcasted_iota(jnp.int32, (CHUNK, DV), 0).astype(F32)
    k_scale = DK ** -0.5

    def rows(c):
        if isinstance(c, int):
            return pl.ds(c * CHUNK, CHUNK)
        return pl.ds(pl.multiple_of(c * CHUNK, CHUNK), CHUNK)

    def maybe_rope(x, r):
        if not rope:
            return x
        return _rope(x, rope_ref[0, r, :], rope_ref[1, r, :], rope_ref[2, r, :])

    def loop(lo, hi, body, init, reverse=False):
        if n_chunks <= 4:
            carry = init
            for c in (range(hi - 1, lo - 1, -1) if reverse else range(lo, hi)):
                carry = body(c, carry)
            return carry
        if reverse:
            return lax.fori_loop(lo, hi, lambda t, cr: body(hi - 1 - t + lo, cr), init, unroll=4)
        return lax.fori_loop(lo, hi, body, init, unroll=4)

    consts = []
    for hh in range(hg):
        lg_f = _log_sigmoid(logit_ref[0, hh])[0:1, :]
        lg_b = _log_sigmoid(logit_ref[1, hh])[0:1, :]
        lg_fv = jnp.concatenate([lg_f, lg_f], axis=1)
        lg_bv = jnp.concatenate([lg_b, lg_b], axis=1)
        consts.append(dict(
            decay=(jnp.where(rel >= 0, jnp.exp(lg_f * jnp.maximum(rel, 0.0)), 0.0)
                   + jnp.where(rel <= 0, jnp.exp(lg_b * jnp.maximum(-rel, 0.0)), 0.0)),
            read_f=jnp.exp(lg_fv * (row_v + 1.0)),
            read_b=jnp.exp(lg_bv * (CHUNK - row_v)),
            write_f=jnp.exp(lg_f * (CHUNK - 1.0 - row)),
            write_b=jnp.exp(lg_b * row),
            cd_f=jnp.exp(lg_fv * float(CHUNK)),
            cd_b=jnp.exp(lg_bv * float(CHUNK)),
        ))
    if has_s0:
        s0_f = tuple(s0_ref[0, hh] for hh in range(hg))
        s0_b = tuple(s0_ref[1, hh] for hh in range(hg))
    else:
        s0_f = s0_b = tuple(jnp.zeros((DK, DV), F32) for _ in range(hg))

    def bwd_body(c, s_b):
        r = rows(c)
        out = []
        for hh, cst in enumerate(consts):
            kq = slice(hh * DK, (hh + 1) * DK)
            vq = slice(hh * DV, (hh + 1) * DV)
            ks = maybe_rope(k_ref[r, kq].astype(F32), r) * k_scale
            ks_scr[r, kq] = ks.astype(BF16)
            sb_scr[hh, c] = s_b[hh].astype(BF16)
            kw = (ks * cst["write_b"]).astype(BF16)
            out.append(s_b[hh] * cst["cd_b"] + lax.dot_general(
                kw, v_ref[r, vq], (((0,), (0,)), ((), ())), preferred_element_type=F32))
        return tuple(out)

    s_b = loop(0, n_chunks, bwd_body, s0_b, reverse=True)

    def fwd_body(c, s_f):
        r = rows(c)
        out = []
        for hh, cst in enumerate(consts):
            kq = slice(hh * DK, (hh + 1) * DK)
            vq = slice(hh * DV, (hh + 1) * DV)
            q = maybe_rope(q_ref[r, kq].astype(F32), r).astype(BF16)
            ks = ks_scr[r, kq]
            v = v_ref[r, vq]
            att = lax.dot_general(q, ks, (((1,), (1,)), ((), ())), preferred_element_type=F32) * cst["decay"]
            o = jnp.dot(att.astype(BF16), v, preferred_element_type=F32)
            o += jnp.dot(q, s_f[hh].astype(BF16), preferred_element_type=F32) * cst["read_f"]
            o += jnp.dot(q, sb_scr[hh, c], preferred_element_type=F32) * cst["read_b"]
            oc = o - jnp.mean(o, axis=-1, keepdims=True)
            on = oc * lax.rsqrt(jnp.mean(oc * oc, axis=-1, keepdims=True) + EPS)
            o_ref[r, vq] = (_silu(g_ref[r, vq].astype(F32)) * on).astype(BF16)
            kw = (ks.astype(F32) * cst["write_f"]).astype(BF16)
            out.append(s_f[hh] * cst["cd_f"] + lax.dot_general(
                kw, v, (((0,), (0,)), ((), ())), preferred_element_type=F32))
        return tuple(out)

    s_f = loop(0, n_chunks, fwd_body, s0_f)
    if emit_state:
        for hh in range(hg):
            st_ref[0, hh] = s_f[hh]
            st_ref[1, hh] = s_b[hh]


def _retention(z, logit_b, layer, *, row0, n_seq, seq, hg, rope_tab=None, state_in=None, state_depth=None,
               out_into=None, state_into=None):
    emit_state = state_depth is not None
    n_chunks = seq // CHUNK
    sb = row0 // seq
    kw, vw = hg * DK, hg * DV
    in_specs = [
        pl.BlockSpec((seq, kw), lambda b, h: (sb + b, h)),
        pl.BlockSpec((seq, kw), lambda b, h: (sb + b, HEADS * DK // kw + h)),
        pl.BlockSpec((seq, vw), lambda b, h: (sb + b, 2 * HEADS * DK // vw + h)),
        pl.BlockSpec((seq, vw), lambda b, h: (sb + b, (2 * HEADS * DK + BRANCH_W) // vw + h)),
        pl.BlockSpec((None, 2, hg, 8, 128), lambda b, h: (layer, 0, h, 0, 0)),
    ]
    args = [z, z, z, z, logit_b]
    if rope_tab is not None:
        in_specs.append(pl.BlockSpec((3, seq, DK), lambda b, h: (0, 0, 0)))
        args.append(rope_tab)
    if state_in is not None:
        in_specs.append(pl.BlockSpec((None, None, 2, hg, DK, DV), lambda b, h: (b, layer, 0, h, 0, 0)))
        args.append(state_in)
    aliases = {}
    for out_idx, arr in ((0, out_into), (1, state_into)):
        if arr is not None:
            aliases[len(args)] = out_idx
            in_specs.append(pl.BlockSpec(memory_space=pl.ANY))
            args.append(arr)
    out_specs = [pl.BlockSpec((seq, vw), lambda b, h: (sb + b, h))]
    out_shape = [jax.ShapeDtypeStruct((z.shape[0], BRANCH_W), BF16)]
    if emit_state:
        out_specs.append(pl.BlockSpec((None, None, 2, hg, DK, DV), lambda b, h: (b, layer, 0, h, 0, 0)))
        out_shape.append(jax.ShapeDtypeStruct((n_seq, state_depth, 2, HEADS, DK, DV), F32))
    return pl.pallas_call(
        functools.partial(_retention_kernel, n_chunks=n_chunks, hg=hg, rope=rope_tab is not None,
                          has_s0=state_in is not None, emit_state=emit_state, n_aliased=len(aliases)),
        grid=(n_seq, HEADS // hg),
        in_specs=in_specs,
        out_specs=out_specs,
        out_shape=out_shape,
        scratch_shapes=[pltpu.VMEM((seq, kw), BF16), pltpu.VMEM((hg, n_chunks, DK, DV), BF16)],
        input_output_aliases=aliases,
        compiler_params=_cparams(("arbitrary", "arbitrary"), 48),
        name="retention",
    )(*args)


def _gmlp_kernel(u_ref, v_ref, ng_ref, ws_ref, bs_ref, o_ref, *, n_chunks):
    vn = (_rms(_gelu_tanh(v_ref[...].astype(F32))) * ng_ref[...]).astype(BF16)
    for t in range(n_chunks):
        r = slice(t * CHUNK, (t + 1) * CHUNK)
        for g in range(HEADS):
            cs = slice(g * DV, (g + 1) * DV)
            mixed = jnp.dot(ws_ref[g].astype(BF16), vn[r, cs], preferred_element_type=F32) + bs_ref[g]
            o_ref[r, cs] = (_gelu_tanh(u_ref[r, cs].astype(F32)) * mixed).astype(BF16)


def _gmlp(z, mlp_norm_g3, mlp_ws, bs_b, layer, *, u_col0):
    m = z.shape[0]
    bt = _pick(m, (512, 256, 128))
    ub = u_col0 // BRANCH_W
    return pl.pallas_call(
        functools.partial(_gmlp_kernel, n_chunks=bt // CHUNK),
        grid=(m // bt,),
        in_specs=[
            pl.BlockSpec((bt, BRANCH_W), lambda i: (i, ub)),
            pl.BlockSpec((bt, BRANCH_W), lambda i: (i, ub + 1)),
            pl.BlockSpec((None, 1, BRANCH_W), lambda i: (layer, 0, 0)),
            pl.BlockSpec((None, HEADS, CHUNK, CHUNK), lambda i: (layer, 0, 0, 0)),
            pl.BlockSpec((None, HEADS, CHUNK, DV), lambda i: (layer, 0, 0, 0)),
        ],
        out_specs=pl.BlockSpec((bt, BRANCH_W), lambda i: (i, 0)),
        out_shape=jax.ShapeDtypeStruct((m, BRANCH_W), BF16),
        compiler_params=_cparams(("arbitrary",), 40),
        name="gmlp",
    )(z, z, mlp_norm_g3, mlp_ws, bs_b)


Q_SCALE = DK ** -0.5 * LOG2E


def _attn_prep_kernel(q_ref, k_ref, kc_ref, rope_ref, qo_ref, ko_ref, *, cache_blocks):
    r = pl.program_id(1)

    @pl.when(r < cache_blocks)
    def _():
        ko_ref[...] = kc_ref[...].astype(BF16)

    @pl.when(r >= cache_blocks)
    def _():
        cos_t, sin_lo, sin_hi = rope_ref[0], rope_ref[1], rope_ref[2]
        for g in range(BRANCH_W // DK):
            gs = slice(g * DK, (g + 1) * DK)
            ko_ref[:, gs] = _rope(k_ref[:, gs].astype(F32), cos_t, sin_lo, sin_hi).astype(BF16)
            qo_ref[:, gs] = (_rope(q_ref[:, gs].astype(F32), cos_t, sin_lo, sin_hi) * Q_SCALE).astype(BF16)


def _attn_prep(z, q_col0, row0, kv_lat, cache_k4, layer, rope_tab, *, n_seq, seq):
    past = cache_k4.shape[2]
    tr = _pick(math.gcd(past, seq), (512, 256, 128))
    cb, nb = past // tr, seq // tr
    qb0, qc0 = row0 // tr, q_col0 // BRANCH_W

    def new_blk(r):
        return jnp.maximum(r - cb, 0)

    return pl.pallas_call(
        functools.partial(_attn_prep_kernel, cache_blocks=cb),
        grid=(n_seq, cb + nb),
        in_specs=[
            pl.BlockSpec((tr, BRANCH_W), lambda b, r: (qb0 + b * nb + new_blk(r), qc0)),
            pl.BlockSpec((tr, BRANCH_W), lambda b, r: (b * nb + new_blk(r), 0)),
            pl.BlockSpec((None, None, tr, BRANCH_W), lambda b, r: (b, layer, jnp.minimum(r, cb - 1), 0)),
            pl.BlockSpec((3, tr, DK), lambda b, r: (0, new_blk(r), 0)),
        ],
        out_specs=[
            pl.BlockSpec((tr, BRANCH_W), lambda b, r: (b * nb + new_blk(r), 0)),
            pl.BlockSpec((None, tr, BRANCH_W), lambda b, r: (b, r, 0)),
        ],
        out_shape=[jax.ShapeDtypeStruct((n_seq * seq, BRANCH_W), BF16),
                   jax.ShapeDtypeStruct((n_seq, past + seq, BRANCH_W), BF16)],
        compiler_params=_cparams(("arbitrary", "arbitrary"), 48),
        name="attn_prep",
    )(z, kv_lat, cache_k4, rope_tab)


def _diff_attn_kernel(*refs, tq, n_new, n_cache, hg, prepped, lam_init, aliased):
    it = iter(refs)
    q_ref, k_ref, v_ref = next(it), next(it), next(it)
    vc_ref = next(it) if n_cache else None
    lam_ref, sg_ref = next(it), next(it)
    if aliased:
        next(it)
    o_ref = next(it)
    k_scr = None if prepped else next(it)
    v_scr = next(it)
    sub = min(tq, 256)

    @pl.when(pl.program_id(2) == 0)
    def _():
        for hh in range(hg):
            hs = slice(hh * DV, (hh + 1) * DV)
            if not prepped:
                k_scr[hh] = k_ref[:, hs].astype(BF16)
            if n_cache:
                v_scr[hh, 0:n_cache, :] = vc_ref[:, hs].astype(BF16)
            v_scr[hh, n_cache:n_cache + n_new, :] = v_ref[:, hs].astype(BF16)

    lv = lam_ref[...]
    lam = (jnp.exp(jnp.sum(lv[0:1] * lv[1:2], axis=-1, keepdims=True))
           - jnp.exp(jnp.sum(lv[2:3] * lv[3:4], axis=-1, keepdims=True)) + lam_init)
    for hh, r0 in [(hh, r0) for hh in range(hg) for r0 in range(0, tq, sub)]:
        rs = slice(r0, r0 + sub)
        pvs, invs = [], []
        for c in range(2):
            cs = slice(hh * DV + c * DK, hh * DV + (c + 1) * DK)
            if prepped:
                q, k = q_ref[rs, cs], k_ref[:, cs]
            else:
                q = (q_ref[rs, cs].astype(F32) * Q_SCALE).astype(BF16)
                k = k_scr[hh, :, c * DK:(c + 1) * DK]
            s = lax.dot_general(q, k, (((1,), (1,)), ((), ())), preferred_element_type=F32)
            e = jnp.exp2(s - jnp.max(s, axis=-1, keepdims=True))
            invs.append(1.0 / jnp.sum(e, axis=-1, keepdims=True))
            pvs.append(jnp.dot(e.astype(BF16), v_scr[hh], preferred_element_type=F32))
        o = pvs[0] * invs[0] - pvs[1] * (lam * invs[1])
        o_ref[rs, hh * DV:(hh + 1) * DV] = (_rms(o) * sg_ref[...] * (1.0 - lam_init)).astype(BF16)


def _diff_attn(q_arr, q_spec, k_arr, k_spec, v_arr, v_spec, layer, diff_lambda, subln_g3, lam_init, *,
               out_rows, row0, n_seq, seq, n_keys, hg, tq, prepped, cache_v=None, out_into=None):
    n_cache = 0 if cache_v is None else cache_v.shape[2]
    nq = seq // tq
    w = hg * DV
    in_specs = [q_spec, k_spec, v_spec]
    args = [q_arr, k_arr, v_arr]
    if n_cache:
        in_specs.append(pl.BlockSpec((None, None, n_cache, w), lambda b, h, i: (b, layer, 0, h)))
        args.append(cache_v)
    in_specs += [pl.BlockSpec((None, 4, DK), lambda b, h, i: (layer, 0, 0)),
                 pl.BlockSpec((None, 1, DV), lambda b, h, i: (layer, 0, 0))]
    args += [diff_lambda, subln_g3]
    aliases = {}
    if out_into is not None:
        aliases = {len(args): 0}
        in_specs.append(pl.BlockSpec(memory_space=pl.ANY))
        args.append(out_into)
    ob0 = row0 // tq
    scratch = [] if prepped else [pltpu.VMEM((hg, n_keys, 2 * DK), BF16)]
    scratch.append(pltpu.VMEM((hg, n_keys, DV), BF16))
    return pl.pallas_call(
        functools.partial(_diff_attn_kernel, tq=tq, n_new=seq, n_cache=n_cache, hg=hg, prepped=prepped,
                          lam_init=lam_init, aliased=out_into is not None),
        grid=(n_seq, HEADS // hg, nq),
        in_specs=in_specs,
        out_specs=pl.BlockSpec((tq, w), lambda b, h, i: (ob0 + b * nq + i, h)),
        out_shape=jax.ShapeDtypeStruct((out_rows, BRANCH_W), BF16),
        scratch_shapes=scratch,
        input_output_aliases=aliases,
        compiler_params=_cparams(("arbitrary", "arbitrary", "arbitrary"), 56),
        name="diff_attn",
    )(*args)


def kernel(x_prompt, x_sample, cache_k, cache_v, state_ret, c, c_ctx, w_mod, b_mod, norm_g, w_in,
           ret_decay_logit, mlp_norm_g, mlp_ws, mlp_bs, diff_lambda, diff_subln_g, w_branch, w_o,
           w_up, w_down):
    batch, seq, d = x_prompt.shape
    dec_batch, dec_seq, _ = x_sample.shape
    depth = w_in.shape[0]
    hidden = w_down.shape[1]
    past = cache_k.shape[2]
    n_ctx = batch * seq
    n_lat = dec_batch * dec_seq
    tok = _Tokens(n_ctx, dec_batch, dec_seq)
    m = tok.total

    c_rq, c_rk, c_rv, c_rg = 0, HEADS * DK, 2 * HEADS * DK, 2 * HEADS * DK + BRANCH_W
    c_mu = c_rg + BRANCH_W
    c_dq = c_mu + 2 * BRANCH_W
    c_dk = c_dq + BRANCH_W
    c_dv = c_dk + BRANCH_W
    n_main = c_dk + 3 * d

    bm = _pick(math.gcd(n_ctx, n_lat), (1024, 512, 256, 128))
    assert bm % seq == 0
    bn = _pick(math.gcd(d, BRANCH_W), (1024, 512, 256, 128))
    bm_half = _pick(m, (512, 256, 128))
    n_keys = past + dec_seq
    tq_lat = _pick(dec_seq, (1024, 512, 256, 128))

    w_in_bs = [w_in[0:1].astype(BF16)] + [None] * (depth - 1)
    w_up_bs = [None] * depth
    w_branch_b = w_o_b = w_down_b = None
    cond = jnp.zeros((N_COND, d), F32).at[0].set(c_ctx).at[1:1 + dec_batch].set(c)
    b_mod3 = b_mod.reshape(depth, 1, 6 * d)
    logit_b = jnp.broadcast_to(ret_decay_logit[:, :, :, None, None], (depth, 2, HEADS, 8, 128))
    bs_b = jnp.broadcast_to(mlp_bs[:, :, :, None], (depth, HEADS, CHUNK, DV))
    mlp_norm_g3 = mlp_norm_g.reshape(depth, 1, BRANCH_W)
    subln_g3 = diff_subln_g.reshape(depth, 1, DV)
    cache_k4 = cache_k.reshape(dec_batch, depth, past, BRANCH_W)
    cache_v4 = cache_v.reshape(dec_batch, depth, past, BRANCH_W)
    rope_tab = _rope_tables(dec_seq)

    mods = [jnp.transpose(_modulation(cond, w_mod, b_mod3, l), (1, 0, 2)) for l in range(depth)]
    x_parts = [x_prompt.reshape(n_ctx, d), x_sample.reshape(n_lat, d)]
    h, = _token_call(tok, x_parts, d, norm_g=norm_g, layer_b=0, mod_b=mods[0], gn=0, shift=0, scale=1)

    new_k = new_v = new_s = None
    main_split = c_dk // bn
    w_ctx = 4 * DV
    nq_lat = dec_seq // tq_lat
    for l in range(depth):
        lam_init = 0.8 - 0.6 * math.exp(-0.3 * l)
        w_in_b = w_in_bs[l]
        main_kw = dict(row0=0, nrows=m, ncols=n_main, out_dtype=BF16, bm=bm, bn=bn, name="w_in_main",
                       wcol=lambda j: jnp.where(j < main_split, j, j + 2 * BRANCH_W // bn))
        if l == 0:
            side = [(w_branch, 0, None), (w_o, 0, None), (w_up, 0, d)]
            z, w_branch_b, w_o_b, w_up0 = _matmul(h, w_in_b, 0, side_srcs=side, **main_kw)
            w_branch_b = w_branch_b.reshape(w_branch.shape)
            w_o_b = w_o_b.reshape(w_o.shape)
            w_up_bs[0] = w_up0.reshape(1, d, 2 * hidden)
        else:
            z = _matmul(h, w_in_b, 0, **main_kw)
        new_k = _matmul(h, w_in_b, 0, row0=0, nrows=n_ctx, ncols=BRANCH_W, out_dtype=F32, bm=bm, bn=bn,
                        wcol=lambda j: c_dk // bn + j, name="w_in_dk_ctx", stacked=(batch, depth, seq, l),
                        out_into=new_k)
        new_v = _matmul(h, w_in_b, 0, row0=0, nrows=n_ctx, ncols=BRANCH_W, out_dtype=F32, bm=bm, bn=bn,
                        wcol=lambda j: c_dv // bn + j, name="w_in_dv_ctx", stacked=(batch, depth, seq, l),
                        out_into=new_v)
        kv_lat = _matmul(h, w_in_b, 0, row0=n_ctx, nrows=n_lat, ncols=2 * BRANCH_W, out_dtype=BF16, bm=bm,
                         bn=bn, wcol=lambda j: c_dk // bn + j, name="w_in_kv_lat")

        ret_o, new_s = _retention(z, logit_b, l, row0=0, n_seq=batch, seq=seq, hg=HEADS, state_depth=depth,
                                  state_into=new_s)
        ret_o, = _retention(z, logit_b, l, row0=n_ctx, n_seq=dec_batch, seq=dec_seq, hg=2,
                            rope_tab=rope_tab, state_in=state_ret, out_into=ret_o)

        mlp_o = _gmlp(z, mlp_norm_g3, mlp_ws, bs_b, l, u_col0=c_mu)

        stacked_spec = pl.BlockSpec((None, None, seq, w_ctx), lambda b, hq, i: (b, l, 0, hq))
        diff_o = _diff_attn(
            z, pl.BlockSpec((seq, w_ctx), lambda b, hq, i: (b, c_dq // w_ctx + hq)),
            new_k, stacked_spec, new_v, stacked_spec, l, diff_lambda, subln_g3, lam_init,
            out_rows=m, row0=0, n_seq=batch, seq=seq, n_keys=seq, hg=4, tq=seq, prepped=False)
        q_lat, k_all = _attn_prep(z, c_dq, n_ctx, kv_lat, cache_k4, l, rope_tab, n_seq=dec_batch, seq=dec_seq)
        diff_o = _diff_attn(
            q_lat, pl.BlockSpec((tq_lat, DV), lambda b, hq, i: (b * nq_lat + i, hq)),
            k_all, pl.BlockSpec((None, n_keys, DV), lambda b, hq, i: (b, 0, hq)),
            kv_lat, pl.BlockSpec((dec_seq, DV), lambda b, hq, i: (b, HEADS + hq)),
            l, diff_lambda, subln_g3, lam_init, out_rows=m, row0=n_ctx, n_seq=dec_batch, seq=dec_seq,
            n_keys=n_keys, hg=1, tq=tq_lat, prepped=True, cache_v=cache_v4, out_into=diff_o)

        merged = _merge(ret_o, mlp_o, diff_o, w_branch_b, l, z, c_dk, d, bm=bm_half, bn=bn)
        y = _matmul(merged, w_o_b, l, row0=0, nrows=m, ncols=d, out_dtype=BF16, bm=bm, bn=bn,
                    wcol=lambda j: j, name="w_o")
        x, h2 = _token_call(tok, x_parts, d, y=y, norm_g=norm_g, layer_a=l, mod_a=mods[l], layer_b=l,
                            mod_b=mods[l], gy=1, gate=2, gn=2, shift=3, scale=4)
        x_parts = [x]

        up_bn = _pick(hidden, (256, 128))
        if l == 0:
            side = [(w_down, 0, None)] + [(w, i * d, d) for i in range(1, depth) for w in (w_in, w_up)]
            act, w_down_b, *rest = _swiglu_up(h2, w_up_bs[0], 0, hidden, side_srcs=side, bm=bm, bn=up_bn)
            w_down_b = w_down_b.reshape(w_down.shape)
            w_in_bs[1:] = [r.reshape(1, d, w_in.shape[2]) for r in rest[0::2]]
            w_up_bs[1:] = [r.reshape(1, d, 2 * hidden) for r in rest[1::2]]
        else:
            act = _swiglu_up(h2, w_up_bs[l], 0, hidden, bm=_pick(m, (2 * bm, bm)), bn=up_bn)
        f = _matmul(act, w_down_b, l, row0=0, nrows=m, ncols=d, out_dtype=BF16, bm=bm_half,
                    bn=_pick(d, (512, 256, 128)), wcol=lambda j: j, name="w_down")
        if l + 1 < depth:
            x, h = _token_call(tok, x_parts, d, y=f, norm_g=norm_g, layer_a=l, mod_a=mods[l], layer_b=l + 1,
                               mod_b=mods[l + 1], gy=3, gate=5, gn=0, shift=0, scale=1)
            x_parts = [x]
        else:
            y_ctx, y_lat = _token_call(tok, x_parts, d, y=f, norm_g=norm_g, layer_a=l, mod_a=mods[l],
                                       split_out=True, gy=3, gate=5)

    return (y_ctx.reshape(batch, seq, d), y_lat.reshape(dec_batch, dec_seq, d),
            new_k.reshape(batch, depth, seq, HEADS, 2, DK), new_v.reshape(batch, depth, seq, HEADS, DV), new_s)
```

```python
import functools
import math

import jax
import jax.numpy as jnp
from jax import lax
from jax.experimental import pallas as pl
from jax.experimental.pallas import tpu as pltpu

F32 = jnp.float32
BF16 = jnp.bfloat16

HEADS = 8
DK = 128
DV = 256
CHUNK = 128
BRANCH_W = HEADS * DV
GRID_W = 64
ROPE_BASE = 10000.0
EPS = 1e-6
N_COND = 16
MIB = 1 << 20
LOG2E = 1.4426950408889634


def _pick(n, prefs):
    for p in prefs:
        if n % p == 0:
            return p
    raise ValueError(f"no block size in {prefs} divides {n}")


def _cparams(dims, vmem_mib):
    return pltpu.CompilerParams(dimension_semantics=dims, vmem_limit_bytes=vmem_mib * MIB)


def _silu(x):
    return x * jax.nn.sigmoid(x)


def _gelu_tanh(x):
    return 0.5 * x * (1.0 + jnp.tanh(math.sqrt(2.0 / math.pi) * (x + 0.044715 * (x * x * x))))


def _rms(x):
    return x * lax.rsqrt(jnp.mean(x * x, axis=-1, keepdims=True) + EPS)


def _mod_kernel(c_ref, w_ref, b_ref, o_ref):
    s = _silu(c_ref[...]).astype(BF16)
    o_ref[...] = jnp.dot(s, w_ref[...].astype(BF16), preferred_element_type=F32) + b_ref[...]


def _modulation(cond, w_mod, b_mod3, layer):
    n_cond, d = cond.shape
    bn = _pick(d, (512, 256, 128))
    per_seg = d // bn
    return pl.pallas_call(
        _mod_kernel,
        grid=(6 * per_seg,),
        in_specs=[
            pl.BlockSpec((n_cond, d), lambda j: (0, 0)),
            pl.BlockSpec((None, d, bn), lambda j: (layer, 0, j)),
            pl.BlockSpec((None, 1, bn), lambda j: (layer, 0, j)),
        ],
        out_specs=pl.BlockSpec((None, n_cond, bn), lambda j: (j // per_seg, 0, j % per_seg)),
        out_shape=jax.ShapeDtypeStruct((6, n_cond, d), F32),
        compiler_params=_cparams(("arbitrary",), 40),
        name="modulation",
    )(cond, w_mod, b_mod3)


class _Tokens:
    def __init__(self, n_ctx, n_dec_seq, dec_seq):
        self.n_ctx = n_ctx
        self.n_dec_seq = n_dec_seq
        self.dec_seq = dec_seq
        self.total = n_ctx + n_dec_seq * dec_seq
        self.br = _pick(math.gcd(n_ctx, dec_seq), (256, 128))
        self.ctx_blocks = n_ctx // self.br

    def cond_index(self, i):
        per = self.dec_seq // self.br
        return jnp.where(i < self.ctx_blocks, 0, 1 + (i - self.ctx_blocks) // per)

    def split_specs(self, d):
        nc = self.ctx_blocks
        return [pl.BlockSpec((self.br, d), lambda i: (jnp.minimum(i, nc - 1), 0)),
                pl.BlockSpec((self.br, d), lambda i: (jnp.maximum(i - nc, 0), 0))]


def _row_spec(tok, d):
    return pl.BlockSpec((tok.br, d), lambda i: (i, 0))


def _ng_spec(layer, d):
    return pl.BlockSpec((None, 4, d), lambda i: (layer, 0, 0))


def _mod_spec(tok, d):
    return pl.BlockSpec((None, 6, d), lambda i: (tok.cond_index(i), 0, 0))


def _token_kernel(*refs, ctx_blocks, split_in, split_out, has_y, has_h, gy, gate, gn, shift, scale):
    it = iter(refs)
    x_refs = [next(it), next(it)] if split_in else [next(it)]
    if has_y:
        y_ref, nga_ref, moda_ref = next(it), next(it), next(it)
    if has_h:
        ngb_ref, modb_ref = next(it), next(it)
    if has_y:
        xo_refs = [next(it), next(it)] if split_out else [next(it)]
    if has_h:
        h_ref = next(it)
    in_ctx = pl.program_id(0) < ctx_blocks

    x = jnp.where(in_ctx, x_refs[0][...], x_refs[1][...]) if split_in else x_refs[0][...]
    if has_y:
        x = x + moda_ref[gate:gate + 1, :] * (_rms(y_ref[...].astype(F32)) * nga_ref[gy:gy + 1, :])
        if split_out:
            @pl.when(in_ctx)
            def _():
                xo_refs[0][...] = x

            @pl.when(jnp.logical_not(in_ctx))
            def _():
                xo_refs[1][...] = x
        else:
            xo_refs[0][...] = x
    if has_h:
        hn = _rms(x) * ngb_ref[gn:gn + 1, :]
        h_ref[...] = (hn * (1.0 + modb_ref[scale:scale + 1, :]) + modb_ref[shift:shift + 1, :]).astype(BF16)


def _token_call(tok, x_parts, d, *, y=None, norm_g=None, layer_a=None, mod_a=None, layer_b=None, mod_b=None,
                split_out=False, gy=0, gate=0, gn=0, shift=0, scale=0):
    split_in = len(x_parts) == 2
    has_y, has_h = y is not None, mod_b is not None
    in_specs = tok.split_specs(d) if split_in else [_row_spec(tok, d)]
    args = list(x_parts)
    if has_y:
        in_specs += [_row_spec(tok, d), _ng_spec(layer_a, d), _mod_spec(tok, d)]
        args += [y, norm_g, mod_a]
    if has_h:
        in_specs += [_ng_spec(layer_b, d), _mod_spec(tok, d)]
        args += [norm_g, mod_b]
    out_specs, out_shape = [], []
    if has_y:
        if split_out:
            out_specs += tok.split_specs(d)
            out_shape += [jax.ShapeDtypeStruct((tok.n_ctx, d), F32),
                          jax.ShapeDtypeStruct((tok.total - tok.n_ctx, d), F32)]
        else:
            out_specs.append(_row_spec(tok, d))
            out_shape.append(jax.ShapeDtypeStruct((tok.total, d), F32))
    if has_h:
        out_specs.append(_row_spec(tok, d))
        out_shape.append(jax.ShapeDtypeStruct((tok.total, d), BF16))
    return pl.pallas_call(
        functools.partial(_token_kernel, ctx_blocks=tok.ctx_blocks, split_in=split_in, split_out=split_out,
                          has_y=has_y, has_h=has_h, gy=gy, gate=gate, gn=gn, shift=shift, scale=scale),
        grid=(tok.total // tok.br,),
        in_specs=in_specs,
        out_specs=out_specs,
        out_shape=out_shape,
        compiler_params=_cparams(("arbitrary",), 48),
        name="token_norm",
    )(*args)


class _SideCast:
    def __init__(self, src, steps, first_row=0, n_rows=None):
        cols = src.shape[-1]
        flat = src.reshape(-1, cols)
        n_rows = flat.shape[0] - first_row if n_rows is None else n_rows
        self.rows = next(r for r in (16, 32, 64, 96, 128, 192, 256, 384, 512, 768, 1024)
                         if n_rows % r == 0 and first_row % r == 0 and n_rows // r <= steps)
        self.cols = cols
        self.count = n_rows // self.rows
        self.first = first_row // self.rows
        self.src3 = flat.reshape(-1, self.rows, cols)

    def specs(self, step_of):
        blk = (None, self.rows, self.cols)

        def chunk(*ids):
            return jnp.minimum(step_of(*ids), self.count - 1)

        return (pl.BlockSpec(blk, lambda *ids: (self.first + chunk(*ids), 0, 0)),
                pl.BlockSpec(blk, lambda *ids: (chunk(*ids), 0, 0)),
                jax.ShapeDtypeStruct((self.count, self.rows, self.cols), BF16))


def _side_cast_step(step, sides):
    for in_ref, out_ref, count in sides:
        @pl.when(step < count)
        def _():
            out_ref[...] = in_ref[...].astype(BF16)


def _mm_kernel(*refs, nk, nm, cast, aliased, acc_scratch, side_counts):
    it = iter(refs)
    a_ref, w_ref = next(it), next(it)
    if aliased:
        next(it)
    side_in = [next(it) for _ in side_counts]
    o_ref = next(it)
    side_out = [next(it) for _ in side_counts]
    wb_ref = next(it) if cast else None
    acc_ref = next(it) if acc_scratch else None
    step = (pl.program_id(0) * nm + pl.program_id(1)) * nk + pl.program_id(2)
    _side_cast_step(step, list(zip(side_in, side_out, side_counts)))
    if cast:
        @pl.when(pl.program_id(1) == 0)
        def _():
            wb_ref[...] = w_ref[...].astype(BF16)
        w = wb_ref[...]
    else:
        w = w_ref[...]
    p = jnp.dot(a_ref[...], w, preferred_element_type=F32)
    if nk == 1:
        o_ref[...] = p.astype(o_ref.dtype).reshape(o_ref.shape)
        return
    k = pl.program_id(2)
    tgt = acc_ref if acc_scratch else o_ref

    @pl.when(k == 0)
    def _():
        tgt[...] = p

    @pl.when(k > 0)
    def _():
        tgt[...] += p

    if acc_scratch:
        @pl.when(k == nk - 1)
        def _():
            o_ref[...] = acc_ref[...].astype(o_ref.dtype)


def _matmul(a, w, layer, *, row0, nrows, wcol, ncols, out_dtype, bm, bn, nk=1, name, stacked=None, out_into=None,
            side_srcs=()):
    kdim = a.shape[1]
    bk = kdim // nk
    cast = w.dtype != BF16
    assert nk == 1 or not cast
    acc_scratch = nk > 1 and out_dtype != F32
    r0 = row0 // bm
    nm = nrows // bm
    grid = (ncols // bn, nm, nk)
    sides = [_SideCast(src, grid[0] * nm * nk, first, n) for src, first, n in side_srcs]
    in_specs = [
        pl.BlockSpec((bm, bk), lambda j, i, k: (r0 + i, k)),
        pl.BlockSpec((None, bk, bn), lambda j, i, k: (layer, k, wcol(j))),
    ]
    args = [a, w]
    aliases = {}
    if out_into is not None:
        aliases = {len(args): 0}
        in_specs.append(pl.BlockSpec(memory_space=pl.ANY))
        args.append(out_into)
    if stacked is None:
        out_specs = [pl.BlockSpec((bm, bn), lambda j, i, k: (i, j))]
        out_shape = [jax.ShapeDtypeStruct((nrows, ncols), out_dtype)]
    else:
        n_seq, depth, seq, out_layer = stacked
        out_specs = [pl.BlockSpec((bm // seq, None, seq, bn), lambda j, i, k: (i, out_layer, 0, j))]
        out_shape = [jax.ShapeDtypeStruct((n_seq, depth, seq, ncols), out_dtype)]
    for sc in sides:
        i_spec, o_spec, o_shape = sc.specs(lambda j, i, k: (j * nm + i) * nk + k)
        in_specs.append(i_spec)
        args.append(sc.src3)
        out_specs.append(o_spec)
        out_shape.append(o_shape)
    scratch = ([pltpu.VMEM((bk, bn), BF16)] if cast else []) + ([pltpu.VMEM((bm, bn), F32)] if acc_scratch else [])
    outs = pl.pallas_call(
        functools.partial(_mm_kernel, nk=nk, nm=nm, cast=cast, aliased=out_into is not None,
                          acc_scratch=acc_scratch, side_counts=tuple(sc.count for sc in sides)),
        grid=grid,
        in_specs=in_specs,
        out_specs=out_specs,
        out_shape=out_shape,
        scratch_shapes=scratch,
        input_output_aliases=aliases,
        compiler_params=_cparams(("arbitrary", "arbitrary", "arbitrary"), 56),
        name=name,
    )(*args)
    return outs[0] if not sides else outs


def _mm_cast_ahead_kernel(*refs, nb, nm, n_slabs, ks, side_counts):
    it = iter(refs)
    a_ref, w_ref = next(it), next(it)
    side_in = [next(it) for _ in side_counts]
    o_ref = next(it)
    side_out = [next(it) for _ in side_counts]
    w_scr = next(it)
    j, i = pl.program_id(0), pl.program_id(1)
    _side_cast_step(j * nm + i, list(zip(side_in, side_out, side_counts)))

    @pl.when(jnp.logical_and(j < nb, i < n_slabs))
    def _():
        w_scr[j % 2, pl.ds(pl.multiple_of(i * ks, ks), ks), :] = w_ref[...].astype(BF16)

    @pl.when(j >= 1)
    def _():
        o_ref[...] = jnp.dot(a_ref[...], w_scr[(j - 1) % 2], preferred_element_type=F32).astype(o_ref.dtype)


def _matmul_cast_ahead(a, w, layer, *, wcol, ncols, out_dtype, bm, bn, name, side_srcs=()):
    m, kdim = a.shape
    nb, nm = ncols // bn, m // bm
    ks = next(r for r in range(16, kdim + 1, 16) if kdim % r == 0 and kdim // r <= nm)
    n_slabs = kdim // ks
    sides = [_SideCast(src, (nb + 1) * nm, first, n) for src, first, n in side_srcs]

    def row_blk(j, i):
        return jnp.where(j == 0, 0, i)

    in_specs = [
        pl.BlockSpec((bm, kdim), lambda j, i: (row_blk(j, i), 0)),
        pl.BlockSpec((None, ks, bn), lambda j, i: (layer, jnp.where(j == nb, n_slabs - 1, jnp.minimum(i, n_slabs - 1)),
                                                   wcol(jnp.minimum(j, nb - 1)))),
    ]
    args = [a, w]
    out_specs = [pl.BlockSpec((bm, bn), lambda j, i: (row_blk(j, i), jnp.maximum(j - 1, 0)))]
    out_shape = [jax.ShapeDtypeStruct((m, ncols), out_dtype)]
    for sc in sides:
        i_spec, o_spec, o_shape = sc.specs(lambda j, i: j * nm + i)
        in_specs.append(i_spec)
        args.append(sc.src3)
        out_specs.append(o_spec)
        out_shape.append(o_shape)
    outs = pl.pallas_call(
        functools.partial(_mm_cast_ahead_kernel, nb=nb, nm=nm, n_slabs=n_slabs, ks=ks,
                          side_counts=tuple(sc.count for sc in sides)),
        grid=(nb + 1, nm),
        in_specs=in_specs,
        out_specs=out_specs,
        out_shape=out_shape,
        scratch_shapes=[pltpu.VMEM((2, kdim, bn), BF16)],
        compiler_params=_cparams(("arbitrary", "arbitrary"), 56),
        name=name,
    )(*args)
    return outs[0] if not sides else outs


def _swiglu_kernel(*refs, half, side_counts):
    it = iter(refs)
    a_ref, wa_ref, wb_ref = next(it), next(it), next(it)
    side_in = [next(it) for _ in side_counts]
    o_ref = next(it)
    side_out = [next(it) for _ in side_counts]
    _side_cast_step(pl.program_id(0) * half + pl.program_id(1), list(zip(side_in, side_out, side_counts)))
    x = a_ref[...]
    ga = jnp.dot(x, wa_ref[...], preferred_element_type=F32)
    gb = jnp.dot(x, wb_ref[...], preferred_element_type=F32)
    o_ref[...] = (_silu(ga) * gb).astype(BF16)


def _swiglu_up(h, w_up_b, layer, hidden, *, bm, bn, side_srcs=()):
    m, d = h.shape
    half = hidden // bn
    nm = m // bm
    sides = [_SideCast(src, half * nm, first, n) for src, first, n in side_srcs]
    in_specs = [
        pl.BlockSpec((bm, d), lambda i, j: (i, 0)),
        pl.BlockSpec((None, d, bn), lambda i, j: (layer, 0, j)),
        pl.BlockSpec((None, d, bn), lambda i, j: (layer, 0, half + j)),
    ]
    args = [h, w_up_b, w_up_b]
    out_specs = [pl.BlockSpec((bm, bn), lambda i, j: (i, j))]
    out_shape = [jax.ShapeDtypeStruct((m, hidden), BF16)]
    for sc in sides:
        i_spec, o_spec, o_shape = sc.specs(lambda i, j: i * half + j)
        in_specs.append(i_spec)
        args.append(sc.src3)
        out_specs.append(o_spec)
        out_shape.append(o_shape)
    outs = pl.pallas_call(
        functools.partial(_swiglu_kernel, half=half, side_counts=tuple(sc.count for sc in sides)),
        grid=(nm, half),
        in_specs=in_specs,
        out_specs=out_specs,
        out_shape=out_shape,
        compiler_params=_cparams(("arbitrary", "arbitrary"), 56),
        name="swiglu_up",
    )(*args)
    return outs[0] if not sides else outs


def _merge_kernel(r_ref, m_ref, d_ref, w_ref, g0_ref, g1_ref, g2_ref, o_ref):
    acc = jax.nn.sigmoid(g0_ref[...].astype(F32)) * jnp.dot(r_ref[...], w_ref[0], preferred_element_type=F32)
    acc += jax.nn.sigmoid(g1_ref[...].astype(F32)) * jnp.dot(m_ref[...], w_ref[1], preferred_element_type=F32)
    acc += jax.nn.sigmoid(g2_ref[...].astype(F32)) * jnp.dot(d_ref[...], w_ref[2], preferred_element_type=F32)
    o_ref[...] = acc.astype(BF16)


def _merge(ret_o, mlp_o, diff_o, w_branch, layer, z, gate_col0, d, *, bm, bn):
    m = ret_o.shape[0]
    branch = pl.BlockSpec((bm, BRANCH_W), lambda j, i: (i, 0))

    def gate_spec(t):
        off = (gate_col0 + t * d) // bn
        return pl.BlockSpec((bm, bn), lambda j, i: (i, off + j))

    return pl.pallas_call(
        _merge_kernel,
        grid=(d // bn, m // bm),
        in_specs=[branch, branch, branch,
                  pl.BlockSpec((None, 3, BRANCH_W, bn), lambda j, i: (layer, 0, 0, j)),
                  gate_spec(0), gate_spec(1), gate_spec(2)],
        out_specs=pl.BlockSpec((bm, bn), lambda j, i: (i, j)),
        out_shape=jax.ShapeDtypeStruct((m, d), BF16),
        compiler_params=_cparams(("arbitrary", "arbitrary"), 56),
        name="merge",
    )(ret_o, mlp_o, diff_o, w_branch, z, z, z)


def _rope_tables(n_tok):
    n_rows = n_tok // GRID_W
    rows = jnp.repeat(jnp.arange(n_rows, dtype=F32), GRID_W)
    cols = jnp.tile(jnp.arange(GRID_W, dtype=F32), n_rows)
    axis_dim = DK // 2
    inv = ROPE_BASE ** (-jnp.arange(0, axis_dim, 2, dtype=F32) / axis_dim)
    ang = jnp.stack([rows[:, None] * inv, cols[:, None] * inv], axis=1)
    cos, sin = jnp.cos(ang), jnp.sin(ang)
    zero = jnp.zeros_like(sin)
    cos_t = jnp.concatenate([cos, cos], axis=-1).reshape(n_tok, DK)
    sin_lo = jnp.concatenate([-sin, zero], axis=-1).reshape(n_tok, DK)
    sin_hi = jnp.concatenate([zero, sin], axis=-1).reshape(n_tok, DK)
    return jnp.stack([cos_t, sin_lo, sin_hi], axis=0)


def _rope(x, cos_t, sin_lo, sin_hi):
    return x * cos_t + pltpu.roll(x, DK - 32, 1) * sin_lo + pltpu.roll(x, 32, 1) * sin_hi


def _log_sigmoid(x):
    return jnp.minimum(x, 0.0) - jnp.log(1.0 + jnp.exp(-jnp.abs(x)))


def _retention_kernel(*refs, n_chunks, hg, rope, has_s0, emit_state, n_aliased):
    it = iter(refs)
    q_ref, k_ref, v_ref, g_ref, logit_ref = next(it), next(it), next(it), next(it), next(it)
    rope_ref = next(it) if rope else None
    s0_ref = next(it) if has_s0 else None
    for _ in range(n_aliased):
        next(it)
    o_ref = next(it)
    st_ref = next(it) if emit_state else None
    ks_scr, sb_scr = next(it), next(it)

    row = lax.broadcasted_iota(jnp.int32, (CHUNK, CHUNK), 0).astype(F32)
    col = lax.broadcasted_iota(jnp.int32, (CHUNK, CHUNK), 1).astype(F32)
    rel = row - col
    row_v = lax.broadcasted_iota(jnp.int32, (CHUNK, DV), 0).astype(F32)
    k_scale = DK ** -0.5

    def rows(c):
        if isinstance(c, int):
            return pl.ds(c * CHUNK, CHUNK)
        return pl.ds(pl.multiple_of(c * CHUNK, CHUNK), CHUNK)

    def maybe_rope(x, r):
        if not rope:
            return x
        return _rope(x, rope_ref[0, r, :], rope_ref[1, r, :], rope_ref[2, r, :])

    def loop(lo, hi, body, init, reverse=False):
        if n_chunks <= 4:
            carry = init
            for c in (range(hi - 1, lo - 1, -1) if reverse else range(lo, hi)):
                carry = body(c, carry)
            return carry
        if reverse:
            return lax.fori_loop(lo, hi, lambda t, cr: body(hi - 1 - t + lo, cr), init, unroll=4)
        return lax.fori_loop(lo, hi, body, init, unroll=4)

    consts = []
    for hh in range(hg):
        lg_f = _log_sigmoid(logit_ref[0, hh])[0:1, :]
        lg_b = _log_sigmoid(logit_ref[1, hh])[0:1, :]
        lg_fv = jnp.concatenate([lg_f, lg_f], axis=1)
        lg_bv = jnp.concatenate([lg_b, lg_b], axis=1)
        consts.append(dict(
            decay=(jnp.where(rel >= 0, jnp.exp(lg_f * jnp.maximum(rel, 0.0)), 0.0)
                   + jnp.where(rel <= 0, jnp.exp(lg_b * jnp.maximum(-rel, 0.0)), 0.0)),
            read_f=jnp.exp(lg_fv * (row_v + 1.0)),
            read_b=jnp.exp(lg_bv * (CHUNK - row_v)),
            write_f=jnp.exp(lg_f * (CHUNK - 1.0 - row)),
            write_b=jnp.exp(lg_b * row),
            cd_f=jnp.exp(lg_fv * float(CHUNK)),
            cd_b=jnp.exp(lg_bv * float(CHUNK)),
        ))
    if has_s0:
        s0_f = tuple(s0_ref[0, hh] for hh in range(hg))
        s0_b = tuple(s0_ref[1, hh] for hh in range(hg))
    else:
        s0_f = s0_b = tuple(jnp.zeros((DK, DV), F32) for _ in range(hg))

    def bwd_body(c, s_b):
        r = rows(c)
        out = []
        for hh, cst in enumerate(consts):
            kq = slice(hh * DK, (hh + 1) * DK)
            vq = slice(hh * DV, (hh + 1) * DV)
            ks = maybe_rope(k_ref[r, kq].astype(F32), r) * k_scale
            ks_scr[r, kq] = ks.astype(BF16)
            sb_scr[hh, c] = s_b[hh].astype(BF16)
            kw = (ks * cst["write_b"]).astype(BF16)
            out.append(s_b[hh] * cst["cd_b"] + lax.dot_general(
                kw, v_ref[r, vq], (((0,), (0,)), ((), ())), preferred_element_type=F32))
        return tuple(out)

    s_b = loop(0, n_chunks, bwd_body, s0_b, reverse=True)

    def fwd_body(c, s_f):
        r = rows(c)
        out = []
        for hh, cst in enumerate(consts):
            kq = slice(hh * DK, (hh + 1) * DK)
            vq = slice(hh * DV, (hh + 1) * DV)
            q = maybe_rope(q_ref[r, kq].astype(F32), r).astype(BF16)
            ks = ks_scr[r, kq]
            v = v_ref[r, vq]
            att = lax.dot_general(q, ks, (((1,), (1,)), ((), ())), preferred_element_type=F32) * cst["decay"]
            o = jnp.dot(att.astype(BF16), v, preferred_element_type=F32)
            o += jnp.dot(q, s_f[hh].astype(BF16), preferred_element_type=F32) * cst["read_f"]
            o += jnp.dot(q, sb_scr[hh, c], preferred_element_type=F32) * cst["read_b"]
            oc = o - jnp.mean(o, axis=-1, keepdims=True)
            on = oc * lax.rsqrt(jnp.mean(oc * oc, axis=-1, keepdims=True) + EPS)
            o_ref[r, vq] = (_silu(g_ref[r, vq].astype(F32)) * on).astype(BF16)
            kw = (ks.astype(F32) * cst["write_f"]).astype(BF16)
            out.append(s_f[hh] * cst["cd_f"] + lax.dot_general(
                kw, v, (((0,), (0,)), ((), ())), preferred_element_type=F32))
        return tuple(out)

    s_f = loop(0, n_chunks, fwd_body, s0_f)
    if emit_state:
        for hh in range(hg):
            st_ref[0, hh] = s_f[hh]
            st_ref[1, hh] = s_b[hh]


def _retention(z, logit_b, layer, *, row0, n_seq, seq, hg, rope_tab=None, state_in=None, state_depth=None,
               out_into=None, state_into=None):
    emit_state = state_depth is not None
    n_chunks = seq // CHUNK
    sb = row0 // seq
    kw, vw = hg * DK, hg * DV
    in_specs = [
        pl.BlockSpec((seq, kw), lambda b, h: (sb + b, h)),
        pl.BlockSpec((seq, kw), lambda b, h: (sb + b, HEADS * DK // kw + h)),
        pl.BlockSpec((seq, vw), lambda b, h: (sb + b, 2 * HEADS * DK // vw + h)),
        pl.BlockSpec((seq, vw), lambda b, h: (sb + b, (2 * HEADS * DK + BRANCH_W) // vw + h)),
        pl.BlockSpec((None, 2, hg, 8, 128), lambda b, h: (layer, 0, h, 0, 0)),
    ]
    args = [z, z, z, z, logit_b]
    if rope_tab is not None:
        in_specs.append(pl.BlockSpec((3, seq, DK), lambda b, h: (0, 0, 0)))
        args.append(rope_tab)
    if state_in is not None:
        in_specs.append(pl.BlockSpec((None, None, 2, hg, DK, DV), lambda b, h: (b, layer, 0, h, 0, 0)))
        args.append(state_in)
    aliases = {}
    for out_idx, arr in ((0, out_into), (1, state_into)):
        if arr is not None:
            aliases[len(args)] = out_idx
            in_specs.append(pl.BlockSpec(memory_space=pl.ANY))
            args.append(arr)
    out_specs = [pl.BlockSpec((seq, vw), lambda b, h: (sb + b, h))]
    out_shape = [jax.ShapeDtypeStruct((z.shape[0], BRANCH_W), BF16)]
    if emit_state:
        out_specs.append(pl.BlockSpec((None, None, 2, hg, DK, DV), lambda b, h: (b, layer, 0, h, 0, 0)))
        out_shape.append(jax.ShapeDtypeStruct((n_seq, state_depth, 2, HEADS, DK, DV), F32))
    return pl.pallas_call(
        functools.partial(_retention_kernel, n_chunks=n_chunks, hg=hg, rope=rope_tab is not None,
                          has_s0=state_in is not None, emit_state=emit_state, n_aliased=len(aliases)),
        grid=(n_seq, HEADS // hg),
        in_specs=in_specs,
        out_specs=out_specs,
        out_shape=out_shape,
        scratch_shapes=[pltpu.VMEM((seq, kw), BF16), pltpu.VMEM((hg, n_chunks, DK, DV), BF16)],
        input_output_aliases=aliases,
        compiler_params=_cparams(("arbitrary", "arbitrary"), 48),
        name="retention",
    )(*args)


def _gmlp_kernel(u_ref, v_ref, ng_ref, ws_ref, bs_ref, o_ref, *, n_chunks):
    vn = (_rms(_gelu_tanh(v_ref[...].astype(F32))) * ng_ref[...]).astype(BF16)
    for t in range(n_chunks):
        r = slice(t * CHUNK, (t + 1) * CHUNK)
        for g in range(HEADS):
            cs = slice(g * DV, (g + 1) * DV)
            mixed = jnp.dot(ws_ref[g].astype(BF16), vn[r, cs], preferred_element_type=F32) + bs_ref[g]
            o_ref[r, cs] = (_gelu_tanh(u_ref[r, cs].astype(F32)) * mixed).astype(BF16)


def _gmlp(z, mlp_norm_g3, mlp_ws, bs_b, layer, *, u_col0):
    m = z.shape[0]
    bt = _pick(m, (512, 256, 128))
    ub = u_col0 // BRANCH_W
    return pl.pallas_call(
        functools.partial(_gmlp_kernel, n_chunks=bt // CHUNK),
        grid=(m // bt,),
        in_specs=[
            pl.BlockSpec((bt, BRANCH_W), lambda i: (i, ub)),
            pl.BlockSpec((bt, BRANCH_W), lambda i: (i, ub + 1)),
            pl.BlockSpec((None, 1, BRANCH_W), lambda i: (layer, 0, 0)),
            pl.BlockSpec((None, HEADS, CHUNK, CHUNK), lambda i: (layer, 0, 0, 0)),
            pl.BlockSpec((None, HEADS, CHUNK, DV), lambda i: (layer, 0, 0, 0)),
        ],
        out_specs=pl.BlockSpec((bt, BRANCH_W), lambda i: (i, 0)),
        out_shape=jax.ShapeDtypeStruct((m, BRANCH_W), BF16),
        compiler_params=_cparams(("arbitrary",), 40),
        name="gmlp",
    )(z, z, mlp_norm_g3, mlp_ws, bs_b)


Q_SCALE = DK ** -0.5 * LOG2E


def _attn_prep_kernel(q_ref, k_ref, kc_ref, rope_ref, qo_ref, ko_ref, *, cache_blocks):
    r = pl.program_id(1)

    @pl.when(r < cache_blocks)
    def _():
        ko_ref[...] = kc_ref[...].astype(BF16)

    @pl.when(r >= cache_blocks)
    def _():
        cos_t, sin_lo, sin_hi = rope_ref[0], rope_ref[1], rope_ref[2]
        for g in range(BRANCH_W // DK):
            gs = slice(g * DK, (g + 1) * DK)
            ko_ref[:, gs] = _rope(k_ref[:, gs].astype(F32), cos_t, sin_lo, sin_hi).astype(BF16)
            qo_ref[:, gs] = (_rope(q_ref[:, gs].astype(F32), cos_t, sin_lo, sin_hi) * Q_SCALE).astype(BF16)


def _attn_prep(z, q_col0, row0, kv_lat, cache_k4, layer, rope_tab, *, n_seq, seq):
    past = cache_k4.shape[2]
    tr = _pick(math.gcd(past, seq), (512, 256, 128))
    cb, nb = past // tr, seq // tr
    qb0, qc0 = row0 // tr, q_col0 // BRANCH_W

    def new_blk(r):
        return jnp.maximum(r - cb, 0)

    return pl.pallas_call(
        functools.partial(_attn_prep_kernel, cache_blocks=cb),
        grid=(n_seq, cb + nb),
        in_specs=[
            pl.BlockSpec((tr, BRANCH_W), lambda b, r: (qb0 + b * nb + new_blk(r), qc0)),
            pl.BlockSpec((tr, BRANCH_W), lambda b, r: (b * nb + new_blk(r), 0)),
            pl.BlockSpec((None, None, tr, BRANCH_W), lambda b, r: (b, layer, jnp.minimum(r, cb - 1), 0)),
            pl.BlockSpec((3, tr, DK), lambda b, r: (0, new_blk(r), 0)),
        ],
        out_specs=[
            pl.BlockSpec((tr, BRANCH_W), lambda b, r: (b * nb + new_blk(r), 0)),
            pl.BlockSpec((None, tr, BRANCH_W), lambda b, r: (b, r, 0)),
        ],
        out_shape=[jax.ShapeDtypeStruct((n_seq * seq, BRANCH_W), BF16),
                   jax.ShapeDtypeStruct((n_seq, past + seq, BRANCH_W), BF16)],
        compiler_params=_cparams(("arbitrary", "arbitrary"), 48),
        name="attn_prep",
    )(z, kv_lat, cache_k4, rope_tab)


def _diff_attn_kernel(*refs, tq, n_new, n_cache, hg, prepped, lam_init, aliased):
    it = iter(refs)
    q_ref, k_ref, v_ref = next(it), next(it), next(it)
    vc_ref = next(it) if n_cache else None
    lam_ref, sg_ref = next(it), next(it)
    if aliased:
        next(it)
    o_ref = next(it)
    k_scr = None if prepped else next(it)
    v_scr = next(it)
    sub = min(tq, 256)

    @pl.when(pl.program_id(2) == 0)
    def _():
        for hh in range(hg):
            hs = slice(hh * DV, (hh + 1) * DV)
            if not prepped:
                k_scr[hh] = k_ref[:, hs].astype(BF16)
            if n_cache:
                v_scr[hh, 0:n_cache, :] = vc_ref[:, hs].astype(BF16)
            v_scr[hh, n_cache:n_cache + n_new, :] = v_ref[:, hs].astype(BF16)

    lv = lam_ref[...]
    lam = (jnp.exp(jnp.sum(lv[0:1] * lv[1:2], axis=-1, keepdims=True))
           - jnp.exp(jnp.sum(lv[2:3] * lv[3:4], axis=-1, keepdims=True)) + lam_init)
    for hh, r0 in [(hh, r0) for hh in range(hg) for r0 in range(0, tq, sub)]:
        rs = slice(r0, r0 + sub)
        pvs, invs = [], []
        for c in range(2):
            cs = slice(hh * DV + c * DK, hh * DV + (c + 1) * DK)
            if prepped:
                q, k = q_ref[rs, cs], k_ref[:, cs]
            else:
                q = (q_ref[rs, cs].astype(F32) * Q_SCALE).astype(BF16)
                k = k_scr[hh, :, c * DK:(c + 1) * DK]
            s = lax.dot_general(q, k, (((1,), (1,)), ((), ())), preferred_element_type=F32)
            e = jnp.exp2(s - jnp.max(s, axis=-1, keepdims=True))
            invs.append(1.0 / jnp.sum(e, axis=-1, keepdims=True))
            pvs.append(jnp.dot(e.astype(BF16), v_scr[hh], preferred_element_type=F32))
        o = pvs[0] * invs[0] - pvs[1] * (lam * invs[1])
        o_ref[rs, hh * DV:(hh + 1) * DV] = (_rms(o) * sg_ref[...] * (1.0 - lam_init)).astype(BF16)


def _diff_attn(q_arr, q_spec, k_arr, k_spec, v_arr, v_spec, layer, diff_lambda, subln_g3, lam_init, *,
               out_rows, row0, n_seq, seq, n_keys, hg, tq, prepped, cache_v=None, out_into=None):
    n_cache = 0 if cache_v is None else cache_v.shape[2]
    nq = seq // tq
    w = hg * DV
    in_specs = [q_spec, k_spec, v_spec]
    args = [q_arr, k_arr, v_arr]
    if n_cache:
        in_specs.append(pl.BlockSpec((None, None, n_cache, w), lambda b, h, i: (b, layer, 0, h)))
        args.append(cache_v)
    in_specs += [pl.BlockSpec((None, 4, DK), lambda b, h, i: (layer, 0, 0)),
                 pl.BlockSpec((None, 1, DV), lambda b, h, i: (layer, 0, 0))]
    args += [diff_lambda, subln_g3]
    aliases = {}
    if out_into is not None:
        aliases = {len(args): 0}
        in_specs.append(pl.BlockSpec(memory_space=pl.ANY))
        args.append(out_into)
    ob0 = row0 // tq
    scratch = [] if prepped else [pltpu.VMEM((hg, n_keys, 2 * DK), BF16)]
    scratch.append(pltpu.VMEM((hg, n_keys, DV), BF16))
    return pl.pallas_call(
        functools.partial(_diff_attn_kernel, tq=tq, n_new=seq, n_cache=n_cache, hg=hg, prepped=prepped,
                          lam_init=lam_init, aliased=out_into is not None),
        grid=(n_seq, HEADS // hg, nq),
        in_specs=in_specs,
        out_specs=pl.BlockSpec((tq, w), lambda b, h, i: (ob0 + b * nq + i, h)),
        out_shape=jax.ShapeDtypeStruct((out_rows, BRANCH_W), BF16),
        scratch_shapes=scratch,
        input_output_aliases=aliases,
        compiler_params=_cparams(("arbitrary", "arbitrary", "arbitrary"), 56),
        name="diff_attn",
    )(*args)


def kernel(x_prompt, x_sample, cache_k, cache_v, state_ret, c, c_ctx, w_mod, b_mod, norm_g, w_in,
           ret_decay_logit, mlp_norm_g, mlp_ws, mlp_bs, diff_lambda, diff_subln_g, w_branch, w_o,
           w_up, w_down):
    batch, seq, d = x_prompt.shape
    dec_batch, dec_seq, _ = x_sample.shape
    depth = w_in.shape[0]
    hidden = w_down.shape[1]
    past = cache_k.shape[2]
    n_ctx = batch * seq
    n_lat = dec_batch * dec_seq
    tok = _Tokens(n_ctx, dec_batch, dec_seq)
    m = tok.total

    c_rq, c_rk, c_rv, c_rg = 0, HEADS * DK, 2 * HEADS * DK, 2 * HEADS * DK + BRANCH_W
    c_mu = c_rg + BRANCH_W
    c_dq = c_mu + 2 * BRANCH_W
    c_dk = c_dq + BRANCH_W
    c_dv = c_dk + BRANCH_W
    n_main = c_dk + 3 * d

    bm = _pick(math.gcd(n_ctx, n_lat), (1024, 512, 256, 128))
    assert bm % seq == 0
    bn = _pick(math.gcd(d, BRANCH_W), (1024, 512, 256, 128))
    bm_half = _pick(m, (512, 256, 128))
    n_keys = past + dec_seq
    tq_lat = _pick(dec_seq, (1024, 512, 256, 128))

    w_kv0_b = w_in[0:1, :, c_dk:c_dk + 2 * BRANCH_W].astype(BF16)
    w_in_bs = [None] * depth
    w_up_bs = [None] * depth
    w_branch_b = w_o_b = w_down_b = None
    cond = jnp.zeros((N_COND, d), F32).at[0].set(c_ctx).at[1:1 + dec_batch].set(c)
    b_mod3 = b_mod.reshape(depth, 1, 6 * d)
    logit_b = jnp.broadcast_to(ret_decay_logit[:, :, :, None, None], (depth, 2, HEADS, 8, 128))
    bs_b = jnp.broadcast_to(mlp_bs[:, :, :, None], (depth, HEADS, CHUNK, DV))
    mlp_norm_g3 = mlp_norm_g.reshape(depth, 1, BRANCH_W)
    subln_g3 = diff_subln_g.reshape(depth, 1, DV)
    cache_k4 = cache_k.reshape(dec_batch, depth, past, BRANCH_W)
    cache_v4 = cache_v.reshape(dec_batch, depth, past, BRANCH_W)
    rope_tab = _rope_tables(dec_seq)

    mods = [jnp.transpose(_modulation(cond, w_mod, b_mod3, l), (1, 0, 2)) for l in range(depth)]
    x_parts = [x_prompt.reshape(n_ctx, d), x_sample.reshape(n_lat, d)]
    h, = _token_call(tok, x_parts, d, norm_g=norm_g, layer_b=0, mod_b=mods[0], gn=0, shift=0, scale=1)

    new_k = new_v = new_s = None
    main_split = c_dk // bn
    w_ctx = 4 * DV
    nq_lat = dec_seq // tq_lat
    for l in range(depth):
        lam_init = 0.8 - 0.6 * math.exp(-0.3 * l)
        main_kw = dict(ncols=n_main, out_dtype=BF16, bm=bm, bn=bn, name="w_in_main",
                       wcol=lambda j: jnp.where(j < main_split, j, j + 2 * BRANCH_W // bn))
        if l == 0:
            side = [(w_branch, 0, None), (w_o, 0, None), (w_up, 0, d)]
            z, w_branch_b, w_o_b, w_up0 = _matmul_cast_ahead(h, w_in, 0, side_srcs=side, **main_kw)
            w_branch_b = w_branch_b.reshape(w_branch.shape)
            w_o_b = w_o_b.reshape(w_o.shape)
            w_up_bs[0] = w_up0.reshape(1, d, 2 * hidden)
            w_kv_b, kv_col0 = w_kv0_b, 0
        else:
            z = _matmul(h, w_in_bs[l], 0, row0=0, nrows=m, **main_kw)
            w_kv_b, kv_col0 = w_in_bs[l], c_dk
        new_k = _matmul(h, w_kv_b, 0, row0=0, nrows=n_ctx, ncols=BRANCH_W, out_dtype=F32, bm=bm, bn=bn,
                        wcol=lambda j: kv_col0 // bn + j, name="w_in_dk_ctx", stacked=(batch, depth, seq, l),
                        out_into=new_k)
        new_v = _matmul(h, w_kv_b, 0, row0=0, nrows=n_ctx, ncols=BRANCH_W, out_dtype=F32, bm=bm, bn=bn,
                        wcol=lambda j: (kv_col0 + BRANCH_W) // bn + j, name="w_in_dv_ctx",
                        stacked=(batch, depth, seq, l), out_into=new_v)
        kv_lat = _matmul(h, w_kv_b, 0, row0=n_ctx, nrows=n_lat, ncols=2 * BRANCH_W, out_dtype=BF16, bm=bm,
                         bn=bn, wcol=lambda j: kv_col0 // bn + j, name="w_in_kv_lat")

        ret_o, new_s = _retention(z, logit_b, l, row0=0, n_seq=batch, seq=seq, hg=HEADS, state_depth=depth,
                                  state_into=new_s)
        ret_o, = _retention(z, logit_b, l, row0=n_ctx, n_seq=dec_batch, seq=dec_seq, hg=2,
                            rope_tab=rope_tab, state_in=state_ret, out_into=ret_o)

        mlp_o = _gmlp(z, mlp_norm_g3, mlp_ws, bs_b, l, u_col0=c_mu)

        stacked_spec = pl.BlockSpec((None, None, seq, w_ctx), lambda b, hq, i: (b, l, 0, hq))
        diff_o = _diff_attn(
            z, pl.BlockSpec((seq, w_ctx), lambda b, hq, i: (b, c_dq // w_ctx + hq)),
            new_k, stacked_spec, new_v, stacked_spec, l, diff_lambda, subln_g3, lam_init,
            out_rows=m, row0=0, n_seq=batch, seq=seq, n_keys=seq, hg=4, tq=seq, prepped=False)
        q_lat, k_all = _attn_prep(z, c_dq, n_ctx, kv_lat, cache_k4, l, rope_tab, n_seq=dec_batch, seq=dec_seq)
        diff_o = _diff_attn(
            q_lat, pl.BlockSpec((tq_lat, DV), lambda b, hq, i: (b * nq_lat + i, hq)),
            k_all, pl.BlockSpec((None, n_keys, DV), lambda b, hq, i: (b, 0, hq)),
            kv_lat, pl.BlockSpec((dec_seq, DV), lambda b, hq, i: (b, HEADS + hq)),
            l, diff_lambda, subln_g3, lam_init, out_rows=m, row0=n_ctx, n_seq=dec_batch, seq=dec_seq,
            n_keys=n_keys, hg=1, tq=tq_lat, prepped=True, cache_v=cache_v4, out_into=diff_o)

        merged = _merge(ret_o, mlp_o, diff_o, w_branch_b, l, z, c_dk, d, bm=bm_half, bn=bn)
        y = _matmul(merged, w_o_b, l, row0=0, nrows=m, ncols=d, out_dtype=BF16, bm=bm, bn=bn,
                    wcol=lambda j: j, name="w_o")
        x, h2 = _token_call(tok, x_parts, d, y=y, norm_g=norm_g, layer_a=l, mod_a=mods[l], layer_b=l,
                            mod_b=mods[l], gy=1, gate=2, gn=2, shift=3, scale=4)
        x_parts = [x]

        up_bn = _pick(hidden, (256, 128))
        if l == 0:
            side = [(w_down, 0, None)] + [(w, i * d, d) for i in range(1, depth) for w in (w_in, w_up)]
            act, w_down_b, *rest = _swiglu_up(h2, w_up_bs[0], 0, hidden, side_srcs=side, bm=bm, bn=up_bn)
            w_down_b = w_down_b.reshape(w_down.shape)
            w_in_bs[1:] = [r.reshape(1, d, w_in.shape[2]) for r in rest[0::2]]
            w_up_bs[1:] = [r.reshape(1, d, 2 * hidden) for r in rest[1::2]]
        else:
            act = _swiglu_up(h2, w_up_bs[l], 0, hidden, bm=_pick(m, (2 * bm, bm)), bn=up_bn)
        f = _matmul(act, w_down_b, l, row0=0, nrows=m, ncols=d, out_dtype=BF16, bm=bm_half,
                    bn=_pick(d, (512, 256, 128)), wcol=lambda j: j, name="w_down")
        if l + 1 < depth:
            x, h = _token_call(tok, x_parts, d, y=f, norm_g=norm_g, layer_a=l, mod_a=mods[l], layer_b=l + 1,
                               mod_b=mods[l + 1], gy=3, gate=5, gn=0, shift=0, scale=1)
            x_parts = [x]
        else:
            y_ctx, y_lat = _token_call(tok, x_parts, d, y=f, norm_g=norm_g, layer_a=l, mod_a=mods[l],
                                       split_out=True, gy=3, gate=5)

    return (y_ctx.reshape(batch, seq, d), y_lat.reshape(dec_batch, dec_seq, d),
            new_k.reshape(batch, depth, seq, HEADS, 2, DK), new_v.reshape(batch, depth, seq, HEADS, DV), new_s)
```

```python
import functools
import math

import jax
import jax.numpy as jnp
from jax import lax
from jax.experimental import pallas as pl
from jax.experimental.pallas import tpu as pltpu

F32 = jnp.float32
BF16 = jnp.bfloat16

HEADS = 8
DK = 128
DV = 256
CHUNK = 128
BRANCH_W = HEADS * DV
GRID_W = 64
ROPE_BASE = 10000.0
EPS = 1e-6
N_COND = 16
MIB = 1 << 20
LOG2E = 1.4426950408889634


def _pick(n, prefs):
    for p in prefs:
        if n % p == 0:
            return p
    raise ValueError(f"no block size in {prefs} divides {n}")


def _cparams(dims, vmem_mib):
    return pltpu.CompilerParams(dimension_semantics=dims, vmem_limit_bytes=vmem_mib * MIB)


def _silu(x):
    return x * jax.nn.sigmoid(x)


def _gelu_tanh(x):
    return 0.5 * x * (1.0 + jnp.tanh(math.sqrt(2.0 / math.pi) * (x + 0.044715 * (x * x * x))))


def _rms(x):
    return x * lax.rsqrt(jnp.mean(x * x, axis=-1, keepdims=True) + EPS)


def _mod_kernel(c_ref, w_ref, b_ref, o_ref):
    s = _silu(c_ref[...]).astype(BF16)
    o_ref[...] = jnp.dot(s, w_ref[...].astype(BF16), preferred_element_type=F32) + b_ref[...]


def _modulation(cond, w_mod, b_mod3, layer):
    n_cond, d = cond.shape
    bn = _pick(d, (512, 256, 128))
    per_seg = d // bn
    return pl.pallas_call(
        _mod_kernel,
        grid=(6 * per_seg,),
        in_specs=[
            pl.BlockSpec((n_cond, d), lambda j: (0, 0)),
            pl.BlockSpec((None, d, bn), lambda j: (layer, 0, j)),
            pl.BlockSpec((None, 1, bn), lambda j: (layer, 0, j)),
        ],
        out_specs=pl.BlockSpec((None, n_cond, bn), lambda j: (j // per_seg, 0, j % per_seg)),
        out_shape=jax.ShapeDtypeStruct((6, n_cond, d), F32),
        compiler_params=_cparams(("arbitrary",), 40),
        name="modulation",
    )(cond, w_mod, b_mod3)


class _Tokens:
    def __init__(self, n_ctx, n_dec_seq, dec_seq):
        self.n_ctx = n_ctx
        self.n_dec_seq = n_dec_seq
        self.dec_seq = dec_seq
        self.total = n_ctx + n_dec_seq * dec_seq
        self.br = _pick(math.gcd(n_ctx, dec_seq), (256, 128))
        self.ctx_blocks = n_ctx // self.br

    def cond_index(self, i):
        per = self.dec_seq // self.br
        return jnp.where(i < self.ctx_blocks, 0, 1 + (i - self.ctx_blocks) // per)

    def split_specs(self, d):
        nc = self.ctx_blocks
        return [pl.BlockSpec((self.br, d), lambda i: (jnp.minimum(i, nc - 1), 0)),
                pl.BlockSpec((self.br, d), lambda i: (jnp.maximum(i - nc, 0), 0))]


def _row_spec(tok, d):
    return pl.BlockSpec((tok.br, d), lambda i: (i, 0))


def _ng_spec(layer, d):
    return pl.BlockSpec((None, 4, d), lambda i: (layer, 0, 0))


def _mod_spec(tok, d):
    return pl.BlockSpec((None, 6, d), lambda i: (tok.cond_index(i), 0, 0))


def _token_kernel(*refs, ctx_blocks, split_in, split_out, has_y, has_h, gy, gate, gn, shift, scale):
    it = iter(refs)
    x_refs = [next(it), next(it)] if split_in else [next(it)]
    if has_y:
        y_ref, nga_ref, moda_ref = next(it), next(it), next(it)
    if has_h:
        ngb_ref, modb_ref = next(it), next(it)
    if has_y:
        xo_refs = [next(it), next(it)] if split_out else [next(it)]
    if has_h:
        h_ref = next(it)
    in_ctx = pl.program_id(0) < ctx_blocks

    x = jnp.where(in_ctx, x_refs[0][...], x_refs[1][...]) if split_in else x_refs[0][...]
    if has_y:
        x = x + moda_ref[gate:gate + 1, :] * (_rms(y_ref[...].astype(F32)) * nga_ref[gy:gy + 1, :])
        if split_out:
            @pl.when(in_ctx)
            def _():
                xo_refs[0][...] = x

            @pl.when(jnp.logical_not(in_ctx))
            def _():
                xo_refs[1][...] = x
        else:
            xo_refs[0][...] = x
    if has_h:
        hn = _rms(x) * ngb_ref[gn:gn + 1, :]
        h_ref[...] = (hn * (1.0 + modb_ref[scale:scale + 1, :]) + modb_ref[shift:shift + 1, :]).astype(BF16)


def _token_call(tok, x_parts, d, *, y=None, norm_g=None, layer_a=None, mod_a=None, layer_b=None, mod_b=None,
                split_out=False, gy=0, gate=0, gn=0, shift=0, scale=0):
    split_in = len(x_parts) == 2
    has_y, has_h = y is not None, mod_b is not None
    in_specs = tok.split_specs(d) if split_in else [_row_spec(tok, d)]
    args = list(x_parts)
    if has_y:
        in_specs += [_row_spec(tok, d), _ng_spec(layer_a, d), _mod_spec(tok, d)]
        args += [y, norm_g, mod_a]
    if has_h:
        in_specs += [_ng_spec(layer_b, d), _mod_spec(tok, d)]
        args += [norm_g, mod_b]
    out_specs, out_shape = [], []
    if has_y:
        if split_out:
            out_specs += tok.split_specs(d)
            out_shape += [jax.ShapeDtypeStruct((tok.n_ctx, d), F32),
                          jax.ShapeDtypeStruct((tok.total - tok.n_ctx, d), F32)]
        else:
            out_specs.append(_row_spec(tok, d))
            out_shape.append(jax.ShapeDtypeStruct((tok.total, d), F32))
    if has_h:
        out_specs.append(_row_spec(tok, d))
        out_shape.append(jax.ShapeDtypeStruct((tok.total, d), BF16))
    return pl.pallas_call(
        functools.partial(_token_kernel, ctx_blocks=tok.ctx_blocks, split_in=split_in, split_out=split_out,
                          has_y=has_y, has_h=has_h, gy=gy, gate=gate, gn=gn, shift=shift, scale=scale),
        grid=(tok.total // tok.br,),
        in_specs=in_specs,
        out_specs=out_specs,
        out_shape=out_shape,
        compiler_params=_cparams(("arbitrary",), 48),
        name="token_norm",
    )(*args)


class _SideCast:
    def __init__(self, src, steps, first_row=0, n_rows=None):
        cols = src.shape[-1]
        flat = src.reshape(-1, cols)
        n_rows = flat.shape[0] - first_row if n_rows is None else n_rows
        self.rows = next(r for r in (16, 32, 64, 96, 128, 192, 256, 384, 512, 768, 1024)
                         if n_rows % r == 0 and first_row % r == 0 and n_rows // r <= steps)
        self.cols = cols
        self.count = n_rows // self.rows
        self.first = first_row // self.rows
        self.src3 = flat.reshape(-1, self.rows, cols)

    def specs(self, step_of):
        blk = (None, self.rows, self.cols)

        def chunk(*ids):
            return jnp.minimum(step_of(*ids), self.count - 1)

        return (pl.BlockSpec(blk, lambda *ids: (self.first + chunk(*ids), 0, 0)),
                pl.BlockSpec(blk, lambda *ids: (chunk(*ids), 0, 0)),
                jax.ShapeDtypeStruct((self.count, self.rows, self.cols), BF16))


def _side_cast_step(step, sides):
    for in_ref, out_ref, count in sides:
        @pl.when(step < count)
        def _():
            out_ref[...] = in_ref[...].astype(BF16)


def _mm_kernel(*refs, nk, nm, cast, aliased, acc_scratch, side_counts):
    it = iter(refs)
    a_ref, w_ref = next(it), next(it)
    if aliased:
        next(it)
    side_in = [next(it) for _ in side_counts]
    o_ref = next(it)
    side_out = [next(it) for _ in side_counts]
    wb_ref = next(it) if cast else None
    acc_ref = next(it) if acc_scratch else None
    step = (pl.program_id(0) * nm + pl.program_id(1)) * nk + pl.program_id(2)
    _side_cast_step(step, list(zip(side_in, side_out, side_counts)))
    if cast:
        @pl.when(pl.program_id(1) == 0)
        def _():
            wb_ref[...] = w_ref[...].astype(BF16)
        w = wb_ref[...]
    else:
        w = w_ref[...]
    p = jnp.dot(a_ref[...], w, preferred_element_type=F32)
    if nk == 1:
        o_ref[...] = p.astype(o_ref.dtype).reshape(o_ref.shape)
        return
    k = pl.program_id(2)
    tgt = acc_ref if acc_scratch else o_ref

    @pl.when(k == 0)
    def _():
        tgt[...] = p

    @pl.when(k > 0)
    def _():
        tgt[...] += p

    if acc_scratch:
        @pl.when(k == nk - 1)
        def _():
            o_ref[...] = acc_ref[...].astype(o_ref.dtype)


def _matmul(a, w, layer, *, row0, nrows, wcol, ncols, out_dtype, bm, bn, nk=1, name, stacked=None, out_into=None,
            side_srcs=()):
    kdim = a.shape[1]
    bk = kdim // nk
    cast = w.dtype != BF16
    assert nk == 1 or not cast
    acc_scratch = nk > 1 and out_dtype != F32
    r0 = row0 // bm
    nm = nrows // bm
    grid = (ncols // bn, nm, nk)
    sides = [_SideCast(src, grid[0] * nm * nk, first, n) for src, first, n in side_srcs]
    in_specs = [
        pl.BlockSpec((bm, bk), lambda j, i, k: (r0 + i, k)),
        pl.BlockSpec((None, bk, bn), lambda j, i, k: (layer, k, wcol(j))),
    ]
    args = [a, w]
    aliases = {}
    if out_into is not None:
        aliases = {len(args): 0}
        in_specs.append(pl.BlockSpec(memory_space=pl.ANY))
        args.append(out_into)
    if stacked is None:
        out_specs = [pl.BlockSpec((bm, bn), lambda j, i, k: (i, j))]
        out_shape = [jax.ShapeDtypeStruct((nrows, ncols), out_dtype)]
    else:
        n_seq, depth, seq, out_layer = stacked
        out_specs = [pl.BlockSpec((bm // seq, None, seq, bn), lambda j, i, k: (i, out_layer, 0, j))]
        out_shape = [jax.ShapeDtypeStruct((n_seq, depth, seq, ncols), out_dtype)]
    for sc in sides:
        i_spec, o_spec, o_shape = sc.specs(lambda j, i, k: (j * nm + i) * nk + k)
        in_specs.append(i_spec)
        args.append(sc.src3)
        out_specs.append(o_spec)
        out_shape.append(o_shape)
    scratch = ([pltpu.VMEM((bk, bn), BF16)] if cast else []) + ([pltpu.VMEM((bm, bn), F32)] if acc_scratch else [])
    outs = pl.pallas_call(
        functools.partial(_mm_kernel, nk=nk, nm=nm, cast=cast, aliased=out_into is not None,
                          acc_scratch=acc_scratch, side_counts=tuple(sc.count for sc in sides)),
        grid=grid,
        in_specs=in_specs,
        out_specs=out_specs,
        out_shape=out_shape,
        scratch_shapes=scratch,
        input_output_aliases=aliases,
        compiler_params=_cparams(("arbitrary", "arbitrary", "arbitrary"), 56),
        name=name,
    )(*args)
    return outs[0] if not sides else outs


def _mm_cast_ahead_kernel(*refs, nb, nm, n_slabs, ks, side_counts):
    it = iter(refs)
    a_ref, w_ref = next(it), next(it)
    side_in = [next(it) for _ in side_counts]
    o_ref = next(it)
    side_out = [next(it) for _ in side_counts]
    w_scr = next(it)
    j, i = pl.program_id(0), pl.program_id(1)
    _side_cast_step(j * nm + i, list(zip(side_in, side_out, side_counts)))

    @pl.when(jnp.logical_and(j < nb, i < n_slabs))
    def _():
        w_scr[j % 2, pl.ds(pl.multiple_of(i * ks, ks), ks), :] = w_ref[...].astype(BF16)

    @pl.when(j >= 1)
    def _():
        o_ref[...] = jnp.dot(a_ref[...], w_scr[(j - 1) % 2], preferred_element_type=F32).astype(o_ref.dtype)


def _matmul_cast_ahead(a, w, layer, *, wcol, ncols, out_dtype, bm, bn, name, side_srcs=()):
    m, kdim = a.shape
    nb, nm = ncols // bn, m // bm
    ks = next(r for r in range(16, kdim + 1, 16) if kdim % r == 0 and kdim // r <= nm)
    n_slabs = kdim // ks
    sides = [_SideCast(src, (nb + 1) * nm, first, n) for src, first, n in side_srcs]

    def row_blk(j, i):
        return jnp.where(j == 0, 0, i)

    in_specs = [
        pl.BlockSpec((bm, kdim), lambda j, i: (row_blk(j, i), 0)),
        pl.BlockSpec((None, ks, bn), lambda j, i: (layer, jnp.where(j == nb, n_slabs - 1, jnp.minimum(i, n_slabs - 1)),
                                                   wcol(jnp.minimum(j, nb - 1)))),
    ]
    args = [a, w]
    out_specs = [pl.BlockSpec((bm, bn), lambda j, i: (row_blk(j, i), jnp.maximum(j - 1, 0)))]
    out_shape = [jax.ShapeDtypeStruct((m, ncols), out_dtype)]
    for sc in sides:
        i_spec, o_spec, o_shape = sc.specs(lambda j, i: j * nm + i)
        in_specs.append(i_spec)
        args.append(sc.src3)
        out_specs.append(o_spec)
        out_shape.append(o_shape)
    outs = pl.pallas_call(
        functools.partial(_mm_cast_ahead_kernel, nb=nb, nm=nm, n_slabs=n_slabs, ks=ks,
                          side_counts=tuple(sc.count for sc in sides)),
        grid=(nb + 1, nm),
        in_specs=in_specs,
        out_specs=out_specs,
        out_shape=out_shape,
        scratch_shapes=[pltpu.VMEM((2, kdim, bn), BF16)],
        compiler_params=_cparams(("arbitrary", "arbitrary"), 56),
        name=name,
    )(*args)
    return outs[0] if not sides else outs


def _swiglu_kernel(*refs, half, side_counts):
    it = iter(refs)
    a_ref, wa_ref, wb_ref = next(it), next(it), next(it)
    side_in = [next(it) for _ in side_counts]
    o_ref = next(it)
    side_out = [next(it) for _ in side_counts]
    _side_cast_step(pl.program_id(0) * half + pl.program_id(1), list(zip(side_in, side_out, side_counts)))
    x = a_ref[...]
    ga = jnp.dot(x, wa_ref[...], preferred_element_type=F32)
    gb = jnp.dot(x, wb_ref[...], preferred_element_type=F32)
    o_ref[...] = (_silu(ga) * gb).astype(BF16)


def _swiglu_up(h, w_up_b, layer, hidden, *, bm, bn, side_srcs=()):
    m, d = h.shape
    half = hidden // bn
    nm = m // bm
    sides = [_SideCast(src, half * nm, first, n) for src, first, n in side_srcs]
    in_specs = [
        pl.BlockSpec((bm, d), lambda i, j: (i, 0)),
        pl.BlockSpec((None, d, bn), lambda i, j: (layer, 0, j)),
        pl.BlockSpec((None, d, bn), lambda i, j: (layer, 0, half + j)),
    ]
    args = [h, w_up_b, w_up_b]
    out_specs = [pl.BlockSpec((bm, bn), lambda i, j: (i, j))]
    out_shape = [jax.ShapeDtypeStruct((m, hidden), BF16)]
    for sc in sides:
        i_spec, o_spec, o_shape = sc.specs(lambda i, j: i * half + j)
        in_specs.append(i_spec)
        args.append(sc.src3)
        out_specs.append(o_spec)
        out_shape.append(o_shape)
    outs = pl.pallas_call(
        functools.partial(_swiglu_kernel, half=half, side_counts=tuple(sc.count for sc in sides)),
        grid=(nm, half),
        in_specs=in_specs,
        out_specs=out_specs,
        out_shape=out_shape,
        compiler_params=_cparams(("arbitrary", "arbitrary"), 56),
        name="swiglu_up",
    )(*args)
    return outs[0] if not sides else outs


def _merge_kernel(r_ref, m_ref, d_ref, w_ref, g0_ref, g1_ref, g2_ref, o_ref):
    acc = jax.nn.sigmoid(g0_ref[...].astype(F32)) * jnp.dot(r_ref[...], w_ref[0], preferred_element_type=F32)
    acc += jax.nn.sigmoid(g1_ref[...].astype(F32)) * jnp.dot(m_ref[...], w_ref[1], preferred_element_type=F32)
    acc += jax.nn.sigmoid(g2_ref[...].astype(F32)) * jnp.dot(d_ref[...], w_ref[2], preferred_element_type=F32)
    o_ref[...] = acc.astype(BF16)


def _merge(ret_o, mlp_o, diff_o, w_branch, layer, z, gate_col0, d, *, bm, bn):
    m = ret_o.shape[0]
    branch = pl.BlockSpec((bm, BRANCH_W), lambda j, i: (i, 0))

    def gate_spec(t):
        off = (gate_col0 + t * d) // bn
        return pl.BlockSpec((bm, bn), lambda j, i: (i, off + j))

    return pl.pallas_call(
        _merge_kernel,
        grid=(d // bn, m // bm),
        in_specs=[branch, branch, branch,
                  pl.BlockSpec((None, 3, BRANCH_W, bn), lambda j, i: (layer, 0, 0, j)),
                  gate_spec(0), gate_spec(1), gate_spec(2)],
        out_specs=pl.BlockSpec((bm, bn), lambda j, i: (i, j)),
        out_shape=jax.ShapeDtypeStruct((m, d), BF16),
        compiler_params=_cparams(("arbitrary", "arbitrary"), 56),
        name="merge",
    )(ret_o, mlp_o, diff_o, w_branch, z, z, z)


def _rope_tables(n_tok):
    n_rows = n_tok // GRID_W
    rows = jnp.repeat(jnp.arange(n_rows, dtype=F32), GRID_W)
    cols = jnp.tile(jnp.arange(GRID_W, dtype=F32), n_rows)
    axis_dim = DK // 2
    inv = ROPE_BASE ** (-jnp.arange(0, axis_dim, 2, dtype=F32) / axis_dim)
    ang = jnp.stack([rows[:, None] * inv, cols[:, None] * inv], axis=1)
    cos, sin = jnp.cos(ang), jnp.sin(ang)
    zero = jnp.zeros_like(sin)
    cos_t = jnp.concatenate([cos, cos], axis=-1).reshape(n_tok, DK)
    sin_lo = jnp.concatenate([-sin, zero], axis=-1).reshape(n_tok, DK)
    sin_hi = jnp.concatenate([zero, sin], axis=-1).reshape(n_tok, DK)
    return jnp.stack([cos_t, sin_lo, sin_hi], axis=0)


def _rope(x, cos_t, sin_lo, sin_hi):
    return x * cos_t + pltpu.roll(x, DK - 32, 1) * sin_lo + pltpu.roll(x, 32, 1) * sin_hi


def _log_sigmoid(x):
    return jnp.minimum(x, 0.0) - jnp.log(1.0 + jnp.exp(-jnp.abs(x)))


def _retention_kernel(*refs, n_chunks, hg, rope, has_s0, emit_state, n_aliased):
    it = iter(refs)
    q_ref, k_ref, v_ref, g_ref, logit_ref = next(it), next(it), next(it), next(it), next(it)
    rope_ref = next(it) if rope else None
    s0_ref = next(it) if has_s0 else None
    for _ in range(n_aliased):
        next(it)
    o_ref = next(it)
    st_ref = next(it) if emit_state else None
    ks_scr, sb_scr = next(it), next(it)

    row = lax.broadcasted_iota(jnp.int32, (CHUNK, CHUNK), 0).astype(F32)
    col = lax.broadcasted_iota(jnp.int32, (CHUNK, CHUNK), 1).astype(F32)
    rel = row - col
    row_v = lax.broadcasted_iota(jnp.int32, (CHUNK, DV), 0).astype(F32)
    k_scale = DK ** -0.5

    def rows(c):
        if isinstance(c, int):
            return pl.ds(c * CHUNK, CHUNK)
        return pl.ds(pl.multiple_of(c * CHUNK, CHUNK), CHUNK)

    def maybe_rope(x, r):
        if not rope:
            return x
        return _rope(x, rope_ref[0, r, :], rope_ref[1, r, :], rope_ref[2, r, :])

    def loop(lo, hi, body, init, reverse=False):
        if n_chunks <= 4:
            carry = init
            for c in (range(hi - 1, lo - 1, -1) if reverse else range(lo, hi)):
                carry = body(c, carry)
            return carry
        if reverse:
            return lax.fori_loop(lo, hi, lambda t, cr: body(hi - 1 - t + lo, cr), init, unroll=4)
        return lax.fori_loop(lo, hi, body, init, unroll=4)

    consts = []
    for hh in range(hg):
        lg_f = _log_sigmoid(logit_ref[0, hh])[0:1, :]
        lg_b = _log_sigmoid(logit_ref[1, hh])[0:1, :]
        lg_fv = jnp.concatenate([lg_f, lg_f], axis=1)
        lg_bv = jnp.concatenate([lg_b, lg_b], axis=1)
        consts.append(dict(
            decay=(jnp.where(rel >= 0, jnp.exp(lg_f * jnp.maximum(rel, 0.0)), 0.0)
                   + jnp.where(rel <= 0, jnp.exp(lg_b * jnp.maximum(-rel, 0.0)), 0.0)),
            read_f=jnp.exp(lg_fv * (row_v + 1.0)),
            read_b=jnp.exp(lg_bv * (CHUNK - row_v)),
            write_f=jnp.exp(lg_f * (CHUNK - 1.0 - row)),
            write_b=jnp.exp(lg_b * row),
            cd_f=jnp.exp(lg_fv * float(CHUNK)),
            cd_b=jnp.exp(lg_bv * float(CHUNK)),
        ))
    if has_s0:
        s0_f = tuple(s0_ref[0, hh] for hh in range(hg))
        s0_b = tuple(s0_ref[1, hh] for hh in range(hg))
    else:
        s0_f = s0_b = tuple(jnp.zeros((DK, DV), F32) for _ in range(hg))

    def bwd_body(c, s_b):
        r = rows(c)
        out = []
        for hh, cst in enumerate(consts):
            kq = slice(hh * DK, (hh + 1) * DK)
            vq = slice(hh * DV, (hh + 1) * DV)
            ks = maybe_rope(k_ref[r, kq].astype(F32), r) * k_scale
            ks_scr[r, kq] = ks.astype(BF16)
            sb_scr[hh, c] = s_b[hh].astype(BF16)
            kw = (ks * cst["write_b"]).astype(BF16)
            out.append(s_b[hh] * cst["cd_b"] + lax.dot_general(
                kw, v_ref[r, vq], (((0,), (0,)), ((), ())), preferred_element_type=F32))
        return tuple(out)

    s_b = loop(0, n_chunks, bwd_body, s0_b, reverse=True)

    def fwd_body(c, s_f):
        r = rows(c)
        out = []
        for hh, cst in enumerate(consts):
            kq = slice(hh * DK, (hh + 1) * DK)
            vq = slice(hh * DV, (hh + 1) * DV)
            q = maybe_rope(q_ref[r, kq].astype(F32), r).astype(BF16)
            ks = ks_scr[r, kq]
            v = v_ref[r, vq]
            att = lax.dot_general(q, ks, (((1,), (1,)), ((), ())), preferred_element_type=F32) * cst["decay"]
            o = jnp.dot(att.astype(BF16), v, preferred_element_type=F32)
            o += jnp.dot(q, s_f[hh].astype(BF16), preferred_element_type=F32) * cst["read_f"]
            o += jnp.dot(q, sb_scr[hh, c], preferred_element_type=F32) * cst["read_b"]
            oc = o - jnp.mean(o, axis=-1, keepdims=True)
            on = oc * lax.rsqrt(jnp.mean(oc * oc, axis=-1, keepdims=True) + EPS)
            o_ref[r, vq] = (_silu(g_ref[r, vq].astype(F32)) * on).astype(BF16)
            kw = (ks.astype(F32) * cst["write_f"]).astype(BF16)
            out.append(s_f[hh] * cst["cd_f"] + lax.dot_general(
                kw, v, (((0,), (0,)), ((), ())), preferred_element_type=F32))
        return tuple(out)

    s_f = loop(0, n_chunks, fwd_body, s0_f)
    if emit_state:
        for hh in range(hg):
            st_ref[0, hh] = s_f[hh]
            st_ref[1, hh] = s_b[hh]


def _retention(z, logit_b, layer, *, row0, n_seq, seq, hg, rope_tab=None, state_in=None, state_depth=None,
               out_into=None, state_into=None):
    emit_state = state_depth is not None
    n_chunks = seq // CHUNK
    sb = row0 // seq
    kw, vw = hg * DK, hg * DV
    in_specs = [
        pl.BlockSpec((seq, kw), lambda b, h: (sb + b, h)),
        pl.BlockSpec((seq, kw), lambda b, h: (sb + b, HEADS * DK // kw + h)),
        pl.BlockSpec((seq, vw), lambda b, h: (sb + b, 2 * HEADS * DK // vw + h)),
        pl.BlockSpec((seq, vw), lambda b, h: (sb + b, (2 * HEADS * DK + BRANCH_W) // vw + h)),
        pl.BlockSpec((None, 2, hg, 8, 128), lambda b, h: (layer, 0, h, 0, 0)),
    ]
    args = [z, z, z, z, logit_b]
    if rope_tab is not None:
        in_specs.append(pl.BlockSpec((3, seq, DK), lambda b, h: (0, 0, 0)))
        args.append(rope_tab)
    if state_in is not None:
        in_specs.append(pl.BlockSpec((None, None, 2, hg, DK, DV), lambda b, h: (b, layer, 0, h, 0, 0)))
        args.append(state_in)
    aliases = {}
    for out_idx, arr in ((0, out_into), (1, state_into)):
        if arr is not None:
            aliases[len(args)] = out_idx
            in_specs.append(pl.BlockSpec(memory_space=pl.ANY))
            args.append(arr)
    out_specs = [pl.BlockSpec((seq, vw), lambda b, h: (sb + b, h))]
    out_shape = [jax.ShapeDtypeStruct((z.shape[0], BRANCH_W), BF16)]
    if emit_state:
        out_specs.append(pl.BlockSpec((None, None, 2, hg, DK, DV), lambda b, h: (b, layer, 0, h, 0, 0)))
        out_shape.append(jax.ShapeDtypeStruct((n_seq, state_depth, 2, HEADS, DK, DV), F32))
    return pl.pallas_call(
        functools.partial(_retention_kernel, n_chunks=n_chunks, hg=hg, rope=rope_tab is not None,
                          has_s0=state_in is not None, emit_state=emit_state, n_aliased=len(aliases)),
        grid=(n_seq, HEADS // hg),
        in_specs=in_specs,
        out_specs=out_specs,
        out_shape=out_shape,
        scratch_shapes=[pltpu.VMEM((seq, kw), BF16), pltpu.VMEM((hg, n_chunks, DK, DV), BF16)],
        input_output_aliases=aliases,
        compiler_params=_cparams(("arbitrary", "arbitrary"), 48),
        name="retention",
    )(*args)


def _gmlp_kernel(u_ref, v_ref, ng_ref, ws_ref, bs_ref, o_ref, *, n_chunks):
    vn = (_rms(_gelu_tanh(v_ref[...].astype(F32))) * ng_ref[...]).astype(BF16)
    for t in range(n_chunks):
        r = slice(t * CHUNK, (t + 1) * CHUNK)
        for g in range(HEADS):
            cs = slice(g * DV, (g + 1) * DV)
            mixed = jnp.dot(ws_ref[g].astype(BF16), vn[r, cs], preferred_element_type=F32) + bs_ref[g]
            o_ref[r, cs] = (_gelu_tanh(u_ref[r, cs].astype(F32)) * mixed).astype(BF16)


def _gmlp(z, mlp_norm_g3, mlp_ws, bs_b, layer, *, u_col0):
    m = z.shape[0]
    bt = _pick(m, (512, 256, 128))
    ub = u_col0 // BRANCH_W
    return pl.pallas_call(
        functools.partial(_gmlp_kernel, n_chunks=bt // CHUNK),
        grid=(m // bt,),
        in_specs=[
            pl.BlockSpec((bt, BRANCH_W), lambda i: (i, ub)),
            pl.BlockSpec((bt, BRANCH_W), lambda i: (i, ub + 1)),
            pl.BlockSpec((None, 1, BRANCH_W), lambda i: (layer, 0, 0)),
            pl.BlockSpec((None, HEADS, CHUNK, CHUNK), lambda i: (layer, 0, 0, 0)),
            pl.BlockSpec((None, HEADS, CHUNK, DV), lambda i: (layer, 0, 0, 0)),
        ],
        out_specs=pl.BlockSpec((bt, BRANCH_W), lambda i: (i, 0)),
        out_shape=jax.ShapeDtypeStruct((m, BRANCH_W), BF16),
        compiler_params=_cparams(("arbitrary",), 40),
        name="gmlp",
    )(z, z, mlp_norm_g3, mlp_ws, bs_b)


Q_SCALE = DK ** -0.5 * LOG2E


def _attn_prep_kernel(q_ref, k_ref, kc_ref, rope_ref, qo_ref, ko_ref, *, cache_blocks):
    r = pl.program_id(1)

    @pl.when(r < cache_blocks)
    def _():
        ko_ref[...] = kc_ref[...].astype(BF16)

    @pl.when(r >= cache_blocks)
    def _():
        cos_t, sin_lo, sin_hi = rope_ref[0], rope_ref[1], rope_ref[2]
        for g in range(BRANCH_W // DK):
            gs = slice(g * DK, (g + 1) * DK)
            ko_ref[:, gs] = _rope(k_ref[:, gs].astype(F32), cos_t, sin_lo, sin_hi).astype(BF16)
            qo_ref[:, gs] = (_rope(q_ref[:, gs].astype(F32), cos_t, sin_lo, sin_hi) * Q_SCALE).astype(BF16)


def _attn_prep(z, q_col0, row0, kv_lat, cache_k4, layer, rope_tab, *, n_seq, seq):
    past = cache_k4.shape[2]
    tr = _pick(math.gcd(past, seq), (512, 256, 128))
    cb, nb = past // tr, seq // tr
    qb0, qc0 = row0 // tr, q_col0 // BRANCH_W

    def new_blk(r):
        return jnp.maximum(r - cb, 0)

    return pl.pallas_call(
        functools.partial(_attn_prep_kernel, cache_blocks=cb),
        grid=(n_seq, cb + nb),
        in_specs=[
            pl.BlockSpec((tr, BRANCH_W), lambda b, r: (qb0 + b * nb + new_blk(r), qc0)),
            pl.BlockSpec((tr, BRANCH_W), lambda b, r: (b * nb + new_blk(r), 0)),
            pl.BlockSpec((None, None, tr, BRANCH_W), lambda b, r: (b, layer, jnp.minimum(r, cb - 1), 0)),
            pl.BlockSpec((3, tr, DK), lambda b, r: (0, new_blk(r), 0)),
        ],
        out_specs=[
            pl.BlockSpec((tr, BRANCH_W), lambda b, r: (b * nb + new_blk(r), 0)),
            pl.BlockSpec((None, tr, BRANCH_W), lambda b, r: (b, r, 0)),
        ],
        out_shape=[jax.ShapeDtypeStruct((n_seq * seq, BRANCH_W), BF16),
                   jax.ShapeDtypeStruct((n_seq, past + seq, BRANCH_W), BF16)],
        compiler_params=_cparams(("arbitrary", "arbitrary"), 48),
        name="attn_prep",
    )(z, kv_lat, cache_k4, rope_tab)


def _diff_attn_kernel(*refs, tq, n_new, n_cache, hg, prepped, lam_init, aliased):
    it = iter(refs)
    q_ref, k_ref, v_ref = next(it), next(it), next(it)
    vc_ref = next(it) if n_cache else None
    lam_ref, sg_ref = next(it), next(it)
    if aliased:
        next(it)
    o_ref = next(it)
    k_scr = None if prepped else next(it)
    v_scr = next(it)
    sub = min(tq, 256)

    @pl.when(pl.program_id(2) == 0)
    def _():
        for hh in range(hg):
            hs = slice(hh * DV, (hh + 1) * DV)
            if not prepped:
                k_scr[hh] = k_ref[:, hs].astype(BF16)
            if n_cache:
                v_scr[hh, 0:n_cache, :] = vc_ref[:, hs].astype(BF16)
            v_scr[hh, n_cache:n_cache + n_new, :] = v_ref[:, hs].astype(BF16)

    lv = lam_ref[...]
    lam = (jnp.exp(jnp.sum(lv[0:1] * lv[1:2], axis=-1, keepdims=True))
           - jnp.exp(jnp.sum(lv[2:3] * lv[3:4], axis=-1, keepdims=True)) + lam_init)
    for hh, r0 in [(hh, r0) for hh in range(hg) for r0 in range(0, tq, sub)]:
        rs = slice(r0, r0 + sub)
        pvs, invs = [], []
        for c in range(2):
            cs = slice(hh * DV + c * DK, hh * DV + (c + 1) * DK)
            if prepped:
                q, k = q_ref[rs, cs], k_ref[:, cs]
            else:
                q = (q_ref[rs, cs].astype(F32) * Q_SCALE).astype(BF16)
                k = k_scr[hh, :, c * DK:(c + 1) * DK]
            s = lax.dot_general(q, k, (((1,), (1,)), ((), ())), preferred_element_type=F32)
            e = jnp.exp2(s - jnp.max(s, axis=-1, keepdims=True))
            invs.append(1.0 / jnp.sum(e, axis=-1, keepdims=True))
            pvs.append(jnp.dot(e.astype(BF16), v_scr[hh], preferred_element_type=F32))
        o = pvs[0] * invs[0] - pvs[1] * (lam * invs[1])
        o_ref[rs, hh * DV:(hh + 1) * DV] = (_rms(o) * sg_ref[...] * (1.0 - lam_init)).astype(BF16)


def _diff_attn(q_arr, q_spec, k_arr, k_spec, v_arr, v_spec, layer, diff_lambda, subln_g3, lam_init, *,
               out_rows, row0, n_seq, seq, n_keys, hg, tq, prepped, cache_v=None, out_into=None):
    n_cache = 0 if cache_v is None else cache_v.shape[2]
    nq = seq // tq
    w = hg * DV
    in_specs = [q_spec, k_spec, v_spec]
    args = [q_arr, k_arr, v_arr]
    if n_cache:
        in_specs.append(pl.BlockSpec((None, None, n_cache, w), lambda b, h, i: (b, layer, 0, h)))
        args.append(cache_v)
    in_specs += [pl.BlockSpec((None, 4, DK), lambda b, h, i: (layer, 0, 0)),
                 pl.BlockSpec((None, 1, DV), lambda b, h, i: (layer, 0, 0))]
    args += [diff_lambda, subln_g3]
    aliases = {}
    if out_into is not None:
        aliases = {len(args): 0}
        in_specs.append(pl.BlockSpec(memory_space=pl.ANY))
        args.append(out_into)
    ob0 = row0 // tq
    scratch = [] if prepped else [pltpu.VMEM((hg, n_keys, 2 * DK), BF16)]
    scratch.append(pltpu.VMEM((hg, n_keys, DV), BF16))
    return pl.pallas_call(
        functools.partial(_diff_attn_kernel, tq=tq, n_new=seq, n_cache=n_cache, hg=hg, prepped=prepped,
                          lam_init=lam_init, aliased=out_into is not None),
        grid=(n_seq, HEADS // hg, nq),
        in_specs=in_specs,
        out_specs=pl.BlockSpec((tq, w), lambda b, h, i: (ob0 + b * nq + i, h)),
        out_shape=jax.ShapeDtypeStruct((out_rows, BRANCH_W), BF16),
        scratch_shapes=scratch,
        input_output_aliases=aliases,
        compiler_params=_cparams(("arbitrary", "arbitrary", "arbitrary"), 56),
        name="diff_attn",
    )(*args)


def kernel(x_prompt, x_sample, cache_k, cache_v, state_ret, c, c_ctx, w_mod, b_mod, norm_g, w_in,
           ret_decay_logit, mlp_norm_g, mlp_ws, mlp_bs, diff_lambda, diff_subln_g, w_branch, w_o,
           w_up, w_down):
    batch, seq, d = x_prompt.shape
    dec_batch, dec_seq, _ = x_sample.shape
    depth = w_in.shape[0]
    hidden = w_down.shape[1]
    past = cache_k.shape[2]
    n_ctx = batch * seq
    n_lat = dec_batch * dec_seq
    tok = _Tokens(n_ctx, dec_batch, dec_seq)
    m = tok.total

    c_rq, c_rk, c_rv, c_rg = 0, HEADS * DK, 2 * HEADS * DK, 2 * HEADS * DK + BRANCH_W
    c_mu = c_rg + BRANCH_W
    c_dq = c_mu + 2 * BRANCH_W
    c_dk = c_dq + BRANCH_W
    c_dv = c_dk + BRANCH_W
    n_main = c_dk + 3 * d

    bm = _pick(math.gcd(n_ctx, n_lat), (1024, 512, 256, 128))
    assert bm % seq == 0
    bn = _pick(math.gcd(d, BRANCH_W), (1024, 512, 256, 128))
    bm_half = _pick(m, (512, 256, 128))
    n_keys = past + dec_seq
    tq_lat = _pick(dec_seq, (1024, 512, 256, 128))

    w_kv0_b = w_in[0:1, :, c_dk:c_dk + 2 * BRANCH_W].astype(BF16)
    w_in_bs = [None] * depth
    w_up_bs = [None] * depth
    w_branch_b = w_o_b = w_down_b = None
    cond = jnp.zeros((N_COND, d), F32).at[0].set(c_ctx).at[1:1 + dec_batch].set(c)
    b_mod3 = b_mod.reshape(depth, 1, 6 * d)
    logit_b = jnp.broadcast_to(ret_decay_logit[:, :, :, None, None], (depth, 2, HEADS, 8, 128))
    bs_b = jnp.broadcast_to(mlp_bs[:, :, :, None], (depth, HEADS, CHUNK, DV))
    mlp_norm_g3 = mlp_norm_g.reshape(depth, 1, BRANCH_W)
    subln_g3 = diff_subln_g.reshape(depth, 1, DV)
    cache_k4 = cache_k.reshape(dec_batch, depth, past, BRANCH_W)
    cache_v4 = cache_v.reshape(dec_batch, depth, past, BRANCH_W)
    rope_tab = _rope_tables(dec_seq)

    mods = [jnp.transpose(_modulation(cond, w_mod, b_mod3, l), (1, 0, 2)) for l in range(depth)]
    x_parts = [x_prompt.reshape(n_ctx, d), x_sample.reshape(n_lat, d)]
    h, = _token_call(tok, x_parts, d, norm_g=norm_g, layer_b=0, mod_b=mods[0], gn=0, shift=0, scale=1)

    new_k = new_v = new_s = None
    main_split = c_dk // bn
    hg_ctx = HEADS
    w_ctx = hg_ctx * DV
    nq_lat = dec_seq // tq_lat
    for l in range(depth):
        lam_init = 0.8 - 0.6 * math.exp(-0.3 * l)
        main_kw = dict(ncols=n_main, out_dtype=BF16, bm=bm, bn=bn, name="w_in_main",
                       wcol=lambda j: jnp.where(j < main_split, j, j + 2 * BRANCH_W // bn))
        if l == 0:
            side = [(w_branch, 0, None), (w_o, 0, None), (w_up, 0, d)]
            z, w_branch_b, w_o_b, w_up0 = _matmul_cast_ahead(h, w_in, 0, side_srcs=side, **main_kw)
            w_branch_b = w_branch_b.reshape(w_branch.shape)
            w_o_b = w_o_b.reshape(w_o.shape)
            w_up_bs[0] = w_up0.reshape(1, d, 2 * hidden)
            w_kv_b, kv_col0 = w_kv0_b, 0
        else:
            z, w_up_l = _matmul(h, w_in_bs[l], 0, row0=0, nrows=m, side_srcs=[(w_up, l * d, d)], **main_kw)
            w_up_bs[l] = w_up_l.reshape(1, d, 2 * hidden)
            w_kv_b, kv_col0 = w_in_bs[l], c_dk
        new_k = _matmul(h, w_kv_b, 0, row0=0, nrows=n_ctx, ncols=BRANCH_W, out_dtype=F32, bm=bm, bn=bn,
                        wcol=lambda j: kv_col0 // bn + j, name="w_in_dk_ctx", stacked=(batch, depth, seq, l),
                        out_into=new_k)
        new_v = _matmul(h, w_kv_b, 0, row0=0, nrows=n_ctx, ncols=BRANCH_W, out_dtype=F32, bm=bm, bn=bn,
                        wcol=lambda j: (kv_col0 + BRANCH_W) // bn + j, name="w_in_dv_ctx",
                        stacked=(batch, depth, seq, l), out_into=new_v)
        kv_lat = _matmul(h, w_kv_b, 0, row0=n_ctx, nrows=n_lat, ncols=2 * BRANCH_W, out_dtype=BF16, bm=bm,
                         bn=bn, wcol=lambda j: kv_col0 // bn + j, name="w_in_kv_lat")

        ret_o, new_s = _retention(z, logit_b, l, row0=0, n_seq=batch, seq=seq, hg=HEADS, state_depth=depth,
                                  state_into=new_s)
        ret_o, = _retention(z, logit_b, l, row0=n_ctx, n_seq=dec_batch, seq=dec_seq, hg=2,
                            rope_tab=rope_tab, state_in=state_ret, out_into=ret_o)

        mlp_o = _gmlp(z, mlp_norm_g3, mlp_ws, bs_b, l, u_col0=c_mu)

        stacked_spec = pl.BlockSpec((None, None, seq, w_ctx), lambda b, hq, i: (b, l, 0, hq))
        diff_o = _diff_attn(
            z, pl.BlockSpec((seq, w_ctx), lambda b, hq, i: (b, c_dq // w_ctx + hq)),
            new_k, stacked_spec, new_v, stacked_spec, l, diff_lambda, subln_g3, lam_init,
            out_rows=m, row0=0, n_seq=batch, seq=seq, n_keys=seq, hg=hg_ctx, tq=seq, prepped=False)
        q_lat, k_all = _attn_prep(z, c_dq, n_ctx, kv_lat, cache_k4, l, rope_tab, n_seq=dec_batch, seq=dec_seq)
        diff_o = _diff_attn(
            q_lat, pl.BlockSpec((tq_lat, DV), lambda b, hq, i: (b * nq_lat + i, hq)),
            k_all, pl.BlockSpec((None, n_keys, DV), lambda b, hq, i: (b, 0, hq)),
            kv_lat, pl.BlockSpec((dec_seq, DV), lambda b, hq, i: (b, HEADS + hq)),
            l, diff_lambda, subln_g3, lam_init, out_rows=m, row0=n_ctx, n_seq=dec_batch, seq=dec_seq,
            n_keys=n_keys, hg=1, tq=tq_lat, prepped=True, cache_v=cache_v4, out_into=diff_o)

        merged = _merge(ret_o, mlp_o, diff_o, w_branch_b, l, z, c_dk, d, bm=bm_half, bn=bn)
        y = _matmul(merged, w_o_b, l, row0=0, nrows=m, ncols=d, out_dtype=BF16, bm=bm, bn=bn,
                    wcol=lambda j: j, name="w_o")
        x, h2 = _token_call(tok, x_parts, d, y=y, norm_g=norm_g, layer_a=l, mod_a=mods[l], layer_b=l,
                            mod_b=mods[l], gy=1, gate=2, gn=2, shift=3, scale=4)
        x_parts = [x]

        up_kw = dict(bm=_pick(m, (2 * bm, bm)), bn=_pick(hidden, (256, 128)))
        down_kw = dict(row0=0, nrows=m, ncols=d, out_dtype=BF16, bm=bm_half, bn=_pick(d, (512, 256, 128)),
                       wcol=lambda j: j, name="w_down")
        if l == 0:
            act, w_down_b = _swiglu_up(h2, w_up_bs[0], 0, hidden, side_srcs=[(w_down, 0, None)], **up_kw)
            w_down_b = w_down_b.reshape(w_down.shape)
        if l == 0 and depth > 1:
            f, *rest = _matmul(act, w_down_b, 0, side_srcs=[(w_in, i * d, d) for i in range(1, depth)], **down_kw)
            w_in_bs[1:] = [r.reshape(1, d, w_in.shape[2]) for r in rest]
        elif l == 0:
            f = _matmul(act, w_down_b, 0, **down_kw)
        else:
            act = _swiglu_up(h2, w_up_bs[l], 0, hidden, **up_kw)
            f = _matmul(act, w_down_b, l, **down_kw)
        if l + 1 < depth:
            x, h = _token_call(tok, x_parts, d, y=f, norm_g=norm_g, layer_a=l, mod_a=mods[l], layer_b=l + 1,
                               mod_b=mods[l + 1], gy=3, gate=5, gn=0, shift=0, scale=1)
            x_parts = [x]
        else:
            y_ctx, y_lat = _token_call(tok, x_parts, d, y=f, norm_g=norm_g, layer_a=l, mod_a=mods[l],
                                       split_out=True, gy=3, gate=5)

    return (y_ctx.reshape(batch, seq, d), y_lat.reshape(dec_batch, dec_seq, d),
            new_k.reshape(batch, depth, seq, HEADS, 2, DK), new_v.reshape(batch, depth, seq, HEADS, DV), new_s)
```

```python
import functools
import math

import jax
import jax.numpy as jnp
from jax import lax
from jax.experimental import pallas as pl
from jax.experimental.pallas import tpu as pltpu

F32 = jnp.float32
BF16 = jnp.bfloat16

HEADS = 8
DK = 128
DV = 256
CHUNK = 128
BRANCH_W = HEADS * DV
GRID_W = 64
ROPE_BASE = 10000.0
EPS = 1e-6
N_COND = 16
MIB = 1 << 20
V7X_VMEM_MIB = 64
VMEM_LIMIT_MIB = V7X_VMEM_MIB - 8
LOG2E = 1.4426950408889634


def _pick(n, prefs):
    for p in prefs:
        if n % p == 0:
            return p
    raise ValueError(f"no block size in {prefs} divides {n}")


def _cparams(*dims):
    return pltpu.CompilerParams(dimension_semantics=dims, vmem_limit_bytes=VMEM_LIMIT_MIB * MIB)


def _silu(x):
    return x * jax.nn.sigmoid(x)


def _gelu_tanh(x):
    return 0.5 * x * (1.0 + jnp.tanh(math.sqrt(2.0 / math.pi) * (x + 0.044715 * (x * x * x))))


def _rms(x):
    return x * lax.rsqrt(jnp.mean(x * x, axis=-1, keepdims=True) + EPS)


def _mod_kernel(c_ref, w_ref, b_ref, o_ref):
    s = _silu(c_ref[...]).astype(BF16)
    o_ref[...] = jnp.dot(s, w_ref[...].astype(BF16), preferred_element_type=F32) + b_ref[...]


def _modulation(cond, w_mod, b_mod3, layer):
    n_cond, d = cond.shape
    bn = _pick(d, (512, 256, 128))
    per_seg = d // bn
    return pl.pallas_call(
        _mod_kernel,
        grid=(6 * per_seg,),
        in_specs=[
            pl.BlockSpec((n_cond, d), lambda j: (0, 0)),
            pl.BlockSpec((None, d, bn), lambda j: (layer, 0, j)),
            pl.BlockSpec((None, 1, bn), lambda j: (layer, 0, j)),
        ],
        out_specs=pl.BlockSpec((None, n_cond, bn), lambda j: (j // per_seg, 0, j % per_seg)),
        out_shape=jax.ShapeDtypeStruct((6, n_cond, d), F32),
        compiler_params=_cparams("arbitrary"),
        name="modulation",
    )(cond, w_mod, b_mod3)


class _Tokens:
    def __init__(self, n_ctx, n_dec_seq, dec_seq):
        self.n_ctx = n_ctx
        self.n_dec_seq = n_dec_seq
        self.dec_seq = dec_seq
        self.total = n_ctx + n_dec_seq * dec_seq
        self.br = _pick(math.gcd(n_ctx, dec_seq), (256, 128))
        self.ctx_blocks = n_ctx // self.br

    def cond_index(self, i):
        per = self.dec_seq // self.br
        return jnp.where(i < self.ctx_blocks, 0, 1 + (i - self.ctx_blocks) // per)

    def split_specs(self, d):
        nc = self.ctx_blocks
        return [pl.BlockSpec((self.br, d), lambda i: (jnp.minimum(i, nc - 1), 0)),
                pl.BlockSpec((self.br, d), lambda i: (jnp.maximum(i - nc, 0), 0))]


def _row_spec(tok, d):
    return pl.BlockSpec((tok.br, d), lambda i: (i, 0))


def _ng_spec(layer, d):
    return pl.BlockSpec((None, 4, d), lambda i: (layer, 0, 0))


def _mod_spec(tok, d):
    return pl.BlockSpec((None, 6, d), lambda i: (tok.cond_index(i), 0, 0))


def _token_kernel(*refs, ctx_blocks, split_in, split_out, has_y, has_h, gy, gate, gn, shift, scale):
    it = iter(refs)
    x_refs = [next(it), next(it)] if split_in else [next(it)]
    if has_y:
        y_ref, nga_ref, moda_ref = next(it), next(it), next(it)
    if has_h:
        ngb_ref, modb_ref = next(it), next(it)
    if has_y:
        xo_refs = [next(it), next(it)] if split_out else [next(it)]
    if has_h:
        h_ref = next(it)
    in_ctx = pl.program_id(0) < ctx_blocks

    x = jnp.where(in_ctx, x_refs[0][...], x_refs[1][...]) if split_in else x_refs[0][...]
    if has_y:
        x = x + moda_ref[gate:gate + 1, :] * (_rms(y_ref[...].astype(F32)) * nga_ref[gy:gy + 1, :])
        if split_out:
            @pl.when(in_ctx)
            def _():
                xo_refs[0][...] = x

            @pl.when(jnp.logical_not(in_ctx))
            def _():
                xo_refs[1][...] = x
        else:
            xo_refs[0][...] = x
    if has_h:
        hn = _rms(x) * ngb_ref[gn:gn + 1, :]
        h_ref[...] = (hn * (1.0 + modb_ref[scale:scale + 1, :]) + modb_ref[shift:shift + 1, :]).astype(BF16)


def _token_call(tok, x_parts, d, *, y=None, norm_g=None, layer_a=None, mod_a=None, layer_b=None, mod_b=None,
                split_out=False, gy=0, gate=0, gn=0, shift=0, scale=0):
    split_in = len(x_parts) == 2
    has_y, has_h = y is not None, mod_b is not None
    in_specs = tok.split_specs(d) if split_in else [_row_spec(tok, d)]
    args = list(x_parts)
    if has_y:
        in_specs += [_row_spec(tok, d), _ng_spec(layer_a, d), _mod_spec(tok, d)]
        args += [y, norm_g, mod_a]
    if has_h:
        in_specs += [_ng_spec(layer_b, d), _mod_spec(tok, d)]
        args += [norm_g, mod_b]
    out_specs, out_shape = [], []
    if has_y:
        if split_out:
            out_specs += tok.split_specs(d)
            out_shape += [jax.ShapeDtypeStruct((tok.n_ctx, d), F32),
                          jax.ShapeDtypeStruct((tok.total - tok.n_ctx, d), F32)]
        else:
            out_specs.append(_row_spec(tok, d))
            out_shape.append(jax.ShapeDtypeStruct((tok.total, d), F32))
    if has_h:
        out_specs.append(_row_spec(tok, d))
        out_shape.append(jax.ShapeDtypeStruct((tok.total, d), BF16))
    return pl.pallas_call(
        functools.partial(_token_kernel, ctx_blocks=tok.ctx_blocks, split_in=split_in, split_out=split_out,
                          has_y=has_y, has_h=has_h, gy=gy, gate=gate, gn=gn, shift=shift, scale=scale),
        grid=(tok.total // tok.br,),
        in_specs=in_specs,
        out_specs=out_specs,
        out_shape=out_shape,
        compiler_params=_cparams("arbitrary"),
        name="token_norm",
    )(*args)


class _SideCast:
    def __init__(self, src, steps, first_row=0, n_rows=None):
        cols = src.shape[-1]
        flat = src.reshape(-1, cols)
        n_rows = flat.shape[0] - first_row if n_rows is None else n_rows
        self.rows = next(r for r in (16, 32, 64, 96, 128, 192, 256, 384, 512, 768, 1024)
                         if n_rows % r == 0 and first_row % r == 0 and n_rows // r <= steps)
        self.cols = cols
        self.count = n_rows // self.rows
        self.first = first_row // self.rows
        self.src3 = flat.reshape(-1, self.rows, cols)

    def specs(self, step_of):
        blk = (None, self.rows, self.cols)

        def chunk(*ids):
            return jnp.minimum(step_of(*ids), self.count - 1)

        return (pl.BlockSpec(blk, lambda *ids: (self.first + chunk(*ids), 0, 0)),
                pl.BlockSpec(blk, lambda *ids: (chunk(*ids), 0, 0)),
                jax.ShapeDtypeStruct((self.count, self.rows, self.cols), BF16))


def _side_cast_step(step, sides):
    for in_ref, out_ref, count in sides:
        @pl.when(step < count)
        def _():
            out_ref[...] = in_ref[...].astype(BF16)


def _mm_kernel(*refs, nk, nm, cast, aliased, acc_scratch, side_counts):
    it = iter(refs)
    a_ref, w_ref = next(it), next(it)
    if aliased:
        next(it)
    side_in = [next(it) for _ in side_counts]
    o_ref = next(it)
    side_out = [next(it) for _ in side_counts]
    wb_ref = next(it) if cast else None
    acc_ref = next(it) if acc_scratch else None
    step = (pl.program_id(0) * nm + pl.program_id(1)) * nk + pl.program_id(2)
    _side_cast_step(step, list(zip(side_in, side_out, side_counts)))
    if cast:
        @pl.when(pl.program_id(1) == 0)
        def _():
            wb_ref[...] = w_ref[...].astype(BF16)
        w = wb_ref[...]
    else:
        w = w_ref[...]
    p = jnp.dot(a_ref[...], w, preferred_element_type=F32)
    if nk == 1:
        o_ref[...] = p.astype(o_ref.dtype).reshape(o_ref.shape)
        return
    k = pl.program_id(2)
    tgt = acc_ref if acc_scratch else o_ref

    @pl.when(k == 0)
    def _():
        tgt[...] = p

    @pl.when(k > 0)
    def _():
        tgt[...] += p

    if acc_scratch:
        @pl.when(k == nk - 1)
        def _():
            o_ref[...] = acc_ref[...].astype(o_ref.dtype)


def _matmul(a, w, layer, *, row0, nrows, wcol, ncols, out_dtype, bm, bn, nk=1, name, stacked=None, out_into=None,
            side_srcs=()):
    kdim = a.shape[1]
    bk = kdim // nk
    cast = w.dtype != BF16
    assert nk == 1 or not cast
    acc_scratch = nk > 1 and out_dtype != F32
    r0 = row0 // bm
    nm = nrows // bm
    grid = (ncols // bn, nm, nk)
    sides = [_SideCast(src, grid[0] * nm * nk, first, n) for src, first, n in side_srcs]
    in_specs = [
        pl.BlockSpec((bm, bk), lambda j, i, k: (r0 + i, k)),
        pl.BlockSpec((None, bk, bn), lambda j, i, k: (layer, k, wcol(j))),
    ]
    args = [a, w]
    aliases = {}
    if out_into is not None:
        aliases = {len(args): 0}
        in_specs.append(pl.BlockSpec(memory_space=pl.ANY))
        args.append(out_into)
    if stacked is None:
        out_specs = [pl.BlockSpec((bm, bn), lambda j, i, k: (i, j))]
        out_shape = [jax.ShapeDtypeStruct((nrows, ncols), out_dtype)]
    else:
        n_seq, depth, seq, out_layer = stacked
        out_specs = [pl.BlockSpec((bm // seq, None, seq, bn), lambda j, i, k: (i, out_layer, 0, j))]
        out_shape = [jax.ShapeDtypeStruct((n_seq, depth, seq, ncols), out_dtype)]
    for sc in sides:
        i_spec, o_spec, o_shape = sc.specs(lambda j, i, k: (j * nm + i) * nk + k)
        in_specs.append(i_spec)
        args.append(sc.src3)
        out_specs.append(o_spec)
        out_shape.append(o_shape)
    scratch = ([pltpu.VMEM((bk, bn), BF16)] if cast else []) + ([pltpu.VMEM((bm, bn), F32)] if acc_scratch else [])
    outs = pl.pallas_call(
        functools.partial(_mm_kernel, nk=nk, nm=nm, cast=cast, aliased=out_into is not None,
                          acc_scratch=acc_scratch, side_counts=tuple(sc.count for sc in sides)),
        grid=grid,
        in_specs=in_specs,
        out_specs=out_specs,
        out_shape=out_shape,
        scratch_shapes=scratch,
        input_output_aliases=aliases,
        compiler_params=_cparams("arbitrary", "arbitrary", "arbitrary"),
        name=name,
    )(*args)
    return outs[0] if not sides else outs


def _mm_cast_ahead_kernel(*refs, nb, nm, n_slabs, ks, side_counts):
    it = iter(refs)
    a_ref, w_ref = next(it), next(it)
    side_in = [next(it) for _ in side_counts]
    o_ref = next(it)
    side_out = [next(it) for _ in side_counts]
    w_scr = next(it)
    j, i = pl.program_id(0), pl.program_id(1)
    _side_cast_step(j * nm + i, list(zip(side_in, side_out, side_counts)))

    @pl.when(jnp.logical_and(j < nb, i < n_slabs))
    def _():
        w_scr[j % 2, pl.ds(pl.multiple_of(i * ks, ks), ks), :] = w_ref[...].astype(BF16)

    @pl.when(j >= 1)
    def _():
        o_ref[...] = jnp.dot(a_ref[...], w_scr[(j - 1) % 2], preferred_element_type=F32).astype(o_ref.dtype)


def _matmul_cast_ahead(a, w, layer, *, wcol, ncols, out_dtype, bm, bn, name, side_srcs=()):
    m, kdim = a.shape
    nb, nm = ncols // bn, m // bm
    ks = next(r for r in range(16, kdim + 1, 16) if kdim % r == 0 and kdim // r <= nm)
    n_slabs = kdim // ks
    sides = [_SideCast(src, (nb + 1) * nm, first, n) for src, first, n in side_srcs]

    def row_blk(j, i):
        return jnp.where(j == 0, 0, i)

    in_specs = [
        pl.BlockSpec((bm, kdim), lambda j, i: (row_blk(j, i), 0)),
        pl.BlockSpec((None, ks, bn), lambda j, i: (layer, jnp.where(j == nb, n_slabs - 1, jnp.minimum(i, n_slabs - 1)),
                                                   wcol(jnp.minimum(j, nb - 1)))),
    ]
    args = [a, w]
    out_specs = [pl.BlockSpec((bm, bn), lambda j, i: (row_blk(j, i), jnp.maximum(j - 1, 0)))]
    out_shape = [jax.ShapeDtypeStruct((m, ncols), out_dtype)]
    for sc in sides:
        i_spec, o_spec, o_shape = sc.specs(lambda j, i: j * nm + i)
        in_specs.append(i_spec)
        args.append(sc.src3)
        out_specs.append(o_spec)
        out_shape.append(o_shape)
    outs = pl.pallas_call(
        functools.partial(_mm_cast_ahead_kernel, nb=nb, nm=nm, n_slabs=n_slabs, ks=ks,
                          side_counts=tuple(sc.count for sc in sides)),
        grid=(nb + 1, nm),
        in_specs=in_specs,
        out_specs=out_specs,
        out_shape=out_shape,
        scratch_shapes=[pltpu.VMEM((2, kdim, bn), BF16)],
        compiler_params=_cparams("arbitrary", "arbitrary"),
        name=name,
    )(*args)
    return outs[0] if not sides else outs


def _swiglu_kernel(*refs, half, side_counts):
    it = iter(refs)
    a_ref, wa_ref, wb_ref = next(it), next(it), next(it)
    side_in = [next(it) for _ in side_counts]
    o_ref = next(it)
    side_out = [next(it) for _ in side_counts]
    _side_cast_step(pl.program_id(0) * half + pl.program_id(1), list(zip(side_in, side_out, side_counts)))
    x = a_ref[...]
    ga = jnp.dot(x, wa_ref[...], preferred_element_type=F32)
    gb = jnp.dot(x, wb_ref[...], preferred_element_type=F32)
    o_ref[...] = (_silu(ga) * gb).astype(BF16)


def _swiglu_up(h, w_up_b, layer, hidden, *, bm, bn, side_srcs=()):
    m, d = h.shape
    half = hidden // bn
    nm = m // bm
    sides = [_SideCast(src, half * nm, first, n) for src, first, n in side_srcs]
    in_specs = [
        pl.BlockSpec((bm, d), lambda i, j: (i, 0)),
        pl.BlockSpec((None, d, bn), lambda i, j: (layer, 0, j)),
        pl.BlockSpec((None, d, bn), lambda i, j: (layer, 0, half + j)),
    ]
    args = [h, w_up_b, w_up_b]
    out_specs = [pl.BlockSpec((bm, bn), lambda i, j: (i, j))]
    out_shape = [jax.ShapeDtypeStruct((m, hidden), BF16)]
    for sc in sides:
        i_spec, o_spec, o_shape = sc.specs(lambda i, j: i * half + j)
        in_specs.append(i_spec)
        args.append(sc.src3)
        out_specs.append(o_spec)
        out_shape.append(o_shape)
    outs = pl.pallas_call(
        functools.partial(_swiglu_kernel, half=half, side_counts=tuple(sc.count for sc in sides)),
        grid=(nm, half),
        in_specs=in_specs,
        out_specs=out_specs,
        out_shape=out_shape,
        compiler_params=_cparams("arbitrary", "arbitrary"),
        name="swiglu_up",
    )(*args)
    return outs[0] if not sides else outs


def _merge_kernel(r_ref, m_ref, d_ref, w_ref, g0_ref, g1_ref, g2_ref, o_ref):
    acc = jax.nn.sigmoid(g0_ref[...].astype(F32)) * jnp.dot(r_ref[...], w_ref[0], preferred_element_type=F32)
    acc += jax.nn.sigmoid(g1_ref[...].astype(F32)) * jnp.dot(m_ref[...], w_ref[1], preferred_element_type=F32)
    acc += jax.nn.sigmoid(g2_ref[...].astype(F32)) * jnp.dot(d_ref[...], w_ref[2], preferred_element_type=F32)
    o_ref[...] = acc.astype(BF16)


def _merge(ret_o, mlp_o, diff_o, w_branch, layer, z, gate_col0, d, *, bm, bn):
    m = ret_o.shape[0]
    branch = pl.BlockSpec((bm, BRANCH_W), lambda j, i: (i, 0))

    def gate_spec(t):
        off = (gate_col0 + t * d) // bn
        return pl.BlockSpec((bm, bn), lambda j, i: (i, off + j))

    return pl.pallas_call(
        _merge_kernel,
        grid=(d // bn, m // bm),
        in_specs=[branch, branch, branch,
                  pl.BlockSpec((None, 3, BRANCH_W, bn), lambda j, i: (layer, 0, 0, j)),
                  gate_spec(0), gate_spec(1), gate_spec(2)],
        out_specs=pl.BlockSpec((bm, bn), lambda j, i: (i, j)),
        out_shape=jax.ShapeDtypeStruct((m, d), BF16),
        compiler_params=_cparams("arbitrary", "arbitrary"),
        name="merge",
    )(ret_o, mlp_o, diff_o, w_branch, z, z, z)


def _rope_tables(n_tok):
    n_rows = n_tok // GRID_W
    rows = jnp.repeat(jnp.arange(n_rows, dtype=F32), GRID_W)
    cols = jnp.tile(jnp.arange(GRID_W, dtype=F32), n_rows)
    axis_dim = DK // 2
    inv = ROPE_BASE ** (-jnp.arange(0, axis_dim, 2, dtype=F32) / axis_dim)
    ang = jnp.stack([rows[:, None] * inv, cols[:, None] * inv], axis=1)
    cos, sin = jnp.cos(ang), jnp.sin(ang)
    zero = jnp.zeros_like(sin)
    cos_t = jnp.concatenate([cos, cos], axis=-1).reshape(n_tok, DK)
    sin_lo = jnp.concatenate([-sin, zero], axis=-1).reshape(n_tok, DK)
    sin_hi = jnp.concatenate([zero, sin], axis=-1).reshape(n_tok, DK)
    return jnp.stack([cos_t, sin_lo, sin_hi], axis=0)


def _rope(x, cos_t, sin_lo, sin_hi):
    return x * cos_t + pltpu.roll(x, DK - 32, 1) * sin_lo + pltpu.roll(x, 32, 1) * sin_hi


def _log_sigmoid(x):
    return jnp.minimum(x, 0.0) - jnp.log(1.0 + jnp.exp(-jnp.abs(x)))


def _retention_kernel(*refs, n_chunks, hg, rope, has_s0, emit_state, n_aliased):
    it = iter(refs)
    q_ref, k_ref, v_ref, g_ref, logit_ref = next(it), next(it), next(it), next(it), next(it)
    rope_ref = next(it) if rope else None
    s0_ref = next(it) if has_s0 else None
    for _ in range(n_aliased):
        next(it)
    o_ref = next(it)
    st_ref = next(it) if emit_state else None
    ks_scr, sb_scr = next(it), next(it)

    row = lax.broadcasted_iota(jnp.int32, (CHUNK, CHUNK), 0).astype(F32)
    col = lax.broadcasted_iota(jnp.int32, (CHUNK, CHUNK), 1).astype(F32)
    rel = row - col
    row_v = lax.broadcasted_iota(jnp.int32, (CHUNK, DV), 0).astype(F32)
    k_scale = DK ** -0.5

    def rows(c):
        if isinstance(c, int):
            return pl.ds(c * CHUNK, CHUNK)
        return pl.ds(pl.multiple_of(c * CHUNK, CHUNK), CHUNK)

    def maybe_rope(x, r):
        if not rope:
            return x
        return _rope(x, rope_ref[0, r, :], rope_ref[1, r, :], rope_ref[2, r, :])

    def loop(lo, hi, body, init, reverse=False):
        if n_chunks <= 4:
            carry = init
            for c in (range(hi - 1, lo - 1, -1) if reverse else range(lo, hi)):
                carry = body(c, carry)
            return carry
        if reverse:
            return lax.fori_loop(lo, hi, lambda t, cr: body(hi - 1 - t + lo, cr), init, unroll=4)
        return lax.fori_loop(lo, hi, body, init, unroll=4)

    consts = []
    for hh in range(hg):
        lg_f = _log_sigmoid(logit_ref[0, hh])[0:1, :]
        lg_b = _log_sigmoid(logit_ref[1, hh])[0:1, :]
        lg_fv = jnp.concatenate([lg_f, lg_f], axis=1)
        lg_bv = jnp.concatenate([lg_b, lg_b], axis=1)
        consts.append(dict(
            decay=(jnp.where(rel >= 0, jnp.exp(lg_f * jnp.maximum(rel, 0.0)), 0.0)
                   + jnp.where(rel <= 0, jnp.exp(lg_b * jnp.maximum(-rel, 0.0)), 0.0)),
            read_f=jnp.exp(lg_fv * (row_v + 1.0)),
            read_b=jnp.exp(lg_bv * (CHUNK - row_v)),
            write_f=jnp.exp(lg_f * (CHUNK - 1.0 - row)),
            write_b=jnp.exp(lg_b * row),
            cd_f=jnp.exp(lg_fv * float(CHUNK)),
            cd_b=jnp.exp(lg_bv * float(CHUNK)),
        ))
    if has_s0:
        s0_f = tuple(s0_ref[0, hh] for hh in range(hg))
        s0_b = tuple(s0_ref[1, hh] for hh in range(hg))
    else:
        s0_f = s0_b = tuple(jnp.zeros((DK, DV), F32) for _ in range(hg))

    def bwd_body(c, s_b):
        r = rows(c)
        out = []
        for hh, cst in enumerate(consts):
            kq = slice(hh * DK, (hh + 1) * DK)
            vq = slice(hh * DV, (hh + 1) * DV)
            ks = maybe_rope(k_ref[r, kq].astype(F32), r) * k_scale
            ks_scr[r, kq] = ks.astype(BF16)
            sb_scr[hh, c] = s_b[hh].astype(BF16)
            kw = (ks * cst["write_b"]).astype(BF16)
            out.append(s_b[hh] * cst["cd_b"] + lax.dot_general(
                kw, v_ref[r, vq], (((0,), (0,)), ((), ())), preferred_element_type=F32))
        return tuple(out)

    s_b = loop(0, n_chunks, bwd_body, s0_b, reverse=True)

    def fwd_body(c, s_f):
        r = rows(c)
        out = []
        for hh, cst in enumerate(consts):
            kq = slice(hh * DK, (hh + 1) * DK)
            vq = slice(hh * DV, (hh + 1) * DV)
            q = maybe_rope(q_ref[r, kq].astype(F32), r).astype(BF16)
            ks = ks_scr[r, kq]
            v = v_ref[r, vq]
            att = lax.dot_general(q, ks, (((1,), (1,)), ((), ())), preferred_element_type=F32) * cst["decay"]
            o = jnp.dot(att.astype(BF16), v, preferred_element_type=F32)
            o += jnp.dot(q, s_f[hh].astype(BF16), preferred_element_type=F32) * cst["read_f"]
            o += jnp.dot(q, sb_scr[hh, c], preferred_element_type=F32) * cst["read_b"]
            oc = o - jnp.mean(o, axis=-1, keepdims=True)
            on = oc * lax.rsqrt(jnp.mean(oc * oc, axis=-1, keepdims=True) + EPS)
            o_ref[r, vq] = (_silu(g_ref[r, vq].astype(F32)) * on).astype(BF16)
            kw = (ks.astype(F32) * cst["write_f"]).astype(BF16)
            out.append(s_f[hh] * cst["cd_f"] + lax.dot_general(
                kw, v, (((0,), (0,)), ((), ())), preferred_element_type=F32))
        return tuple(out)

    s_f = loop(0, n_chunks, fwd_body, s0_f)
    if emit_state:
        for hh in range(hg):
            st_ref[0, hh] = s_f[hh]
            st_ref[1, hh] = s_b[hh]


def _retention(z, logit_b, layer, *, row0, n_seq, seq, hg, rope_tab=None, state_in=None, state_depth=None,
               out_into=None, state_into=None):
    emit_state = state_depth is not None
    n_chunks = seq // CHUNK
    sb = row0 // seq
    kw, vw = hg * DK, hg * DV
    in_specs = [
        pl.BlockSpec((seq, kw), lambda b, h: (sb + b, h)),
        pl.BlockSpec((seq, kw), lambda b, h: (sb + b, HEADS * DK // kw + h)),
        pl.BlockSpec((seq, vw), lambda b, h: (sb + b, 2 * HEADS * DK // vw + h)),
        pl.BlockSpec((seq, vw), lambda b, h: (sb + b, (2 * HEADS * DK + BRANCH_W) // vw + h)),
        pl.BlockSpec((None, 2, hg, 8, 128), lambda b, h: (layer, 0, h, 0, 0)),
    ]
    args = [z, z, z, z, logit_b]
    if rope_tab is not None:
        in_specs.append(pl.BlockSpec((3, seq, DK), lambda b, h: (0, 0, 0)))
        args.append(rope_tab)
    if state_in is not None:
        in_specs.append(pl.BlockSpec((None, None, 2, hg, DK, DV), lambda b, h: (b, layer, 0, h, 0, 0)))
        args.append(state_in)
    aliases = {}
    for out_idx, arr in ((0, out_into), (1, state_into)):
        if arr is not None:
            aliases[len(args)] = out_idx
            in_specs.append(pl.BlockSpec(memory_space=pl.ANY))
            args.append(arr)
    out_specs = [pl.BlockSpec((seq, vw), lambda b, h: (sb + b, h))]
    out_shape = [jax.ShapeDtypeStruct((z.shape[0], BRANCH_W), BF16)]
    if emit_state:
        out_specs.append(pl.BlockSpec((None, None, 2, hg, DK, DV), lambda b, h: (b, layer, 0, h, 0, 0)))
        out_shape.append(jax.ShapeDtypeStruct((n_seq, state_depth, 2, HEADS, DK, DV), F32))
    return pl.pallas_call(
        functools.partial(_retention_kernel, n_chunks=n_chunks, hg=hg, rope=rope_tab is not None,
                          has_s0=state_in is not None, emit_state=emit_state, n_aliased=len(aliases)),
        grid=(n_seq, HEADS // hg),
        in_specs=in_specs,
        out_specs=out_specs,
        out_shape=out_shape,
        scratch_shapes=[pltpu.VMEM((seq, kw), BF16), pltpu.VMEM((hg, n_chunks, DK, DV), BF16)],
        input_output_aliases=aliases,
        compiler_params=_cparams("arbitrary", "arbitrary"),
        name="retention",
    )(*args)


def _gmlp_kernel(u_ref, v_ref, ng_ref, ws_ref, bs_ref, o_ref, *, n_chunks):
    vn = (_rms(_gelu_tanh(v_ref[...].astype(F32))) * ng_ref[...]).astype(BF16)
    for t in range(n_chunks):
        r = slice(t * CHUNK, (t + 1) * CHUNK)
        for g in range(HEADS):
            cs = slice(g * DV, (g + 1) * DV)
            mixed = jnp.dot(ws_ref[g].astype(BF16), vn[r, cs], preferred_element_type=F32) + bs_ref[g]
            o_ref[r, cs] = (_gelu_tanh(u_ref[r, cs].astype(F32)) * mixed).astype(BF16)


def _gmlp(z, mlp_norm_g3, mlp_ws, bs_b, layer, *, u_col0):
    m = z.shape[0]
    bt = _pick(m, (512, 256, 128))
    ub = u_col0 // BRANCH_W
    return pl.pallas_call(
        functools.partial(_gmlp_kernel, n_chunks=bt // CHUNK),
        grid=(m // bt,),
        in_specs=[
            pl.BlockSpec((bt, BRANCH_W), lambda i: (i, ub)),
            pl.BlockSpec((bt, BRANCH_W), lambda i: (i, ub + 1)),
            pl.BlockSpec((None, 1, BRANCH_W), lambda i: (layer, 0, 0)),
            pl.BlockSpec((None, HEADS, CHUNK, CHUNK), lambda i: (layer, 0, 0, 0)),
            pl.BlockSpec((None, HEADS, CHUNK, DV), lambda i: (layer, 0, 0, 0)),
        ],
        out_specs=pl.BlockSpec((bt, BRANCH_W), lambda i: (i, 0)),
        out_shape=jax.ShapeDtypeStruct((m, BRANCH_W), BF16),
        compiler_params=_cparams("arbitrary"),
        name="gmlp",
    )(z, z, mlp_norm_g3, mlp_ws, bs_b)


Q_SCALE = DK ** -0.5 * LOG2E


def _attn_prep_kernel(q_ref, k_ref, kc_ref, rope_ref, qo_ref, ko_ref, *, cache_blocks):
    r = pl.program_id(1)

    @pl.when(r < cache_blocks)
    def _():
        ko_ref[...] = kc_ref[...].astype(BF16)

    @pl.when(r >= cache_blocks)
    def _():
        cos_t, sin_lo, sin_hi = rope_ref[0], rope_ref[1], rope_ref[2]
        for g in range(BRANCH_W // DK):
            gs = slice(g * DK, (g + 1) * DK)
            ko_ref[:, gs] = _rope(k_ref[:, gs].astype(F32), cos_t, sin_lo, sin_hi).astype(BF16)
            qo_ref[:, gs] = (_rope(q_ref[:, gs].astype(F32), cos_t, sin_lo, sin_hi) * Q_SCALE).astype(BF16)


def _attn_prep(z, q_col0, row0, kv_lat, cache_k4, layer, rope_tab, *, n_seq, seq):
    past = cache_k4.shape[2]
    tr = _pick(math.gcd(past, seq), (512, 256, 128))
    cb, nb = past // tr, seq // tr
    qb0, qc0 = row0 // tr, q_col0 // BRANCH_W

    def new_blk(r):
        return jnp.maximum(r - cb, 0)

    return pl.pallas_call(
        functools.partial(_attn_prep_kernel, cache_blocks=cb),
        grid=(n_seq, cb + nb),
        in_specs=[
            pl.BlockSpec((tr, BRANCH_W), lambda b, r: (qb0 + b * nb + new_blk(r), qc0)),
            pl.BlockSpec((tr, BRANCH_W), lambda b, r: (b * nb + new_blk(r), 0)),
            pl.BlockSpec((None, None, tr, BRANCH_W), lambda b, r: (b, layer, jnp.minimum(r, cb - 1), 0)),
            pl.BlockSpec((3, tr, DK), lambda b, r: (0, new_blk(r), 0)),
        ],
        out_specs=[
            pl.BlockSpec((tr, BRANCH_W), lambda b, r: (b * nb + new_blk(r), 0)),
            pl.BlockSpec((None, tr, BRANCH_W), lambda b, r: (b, r, 0)),
        ],
        out_shape=[jax.ShapeDtypeStruct((n_seq * seq, BRANCH_W), BF16),
                   jax.ShapeDtypeStruct((n_seq, past + seq, BRANCH_W), BF16)],
        compiler_params=_cparams("arbitrary", "arbitrary"),
        name="attn_prep",
    )(z, kv_lat, cache_k4, rope_tab)


def _diff_attn_kernel(*refs, tq, n_new, n_cache, hg, prepped, ahead, lam_init, aliased):
    it = iter(refs)
    q_ref, k_ref, v_ref = next(it), next(it), next(it)
    vc_ref = next(it) if n_cache else None
    lam_ref, sg_ref = next(it), next(it)
    if aliased:
        next(it)
    o_ref = next(it)
    k_scr = None if prepped else next(it)
    v_scr = next(it)
    sub = min(tq, 256)

    @pl.when(pl.program_id(2) == 0)
    def _():
        for hh in range(hg):
            hs = slice(hh * DV, (hh + 1) * DV)
            if not prepped:
                k_scr[hh] = k_ref[:, hs].astype(BF16)
            if n_cache:
                v_scr[hh, 0:n_cache, :] = vc_ref[:, hs].astype(BF16)
            v_scr[hh, n_cache:n_cache + n_new, :] = v_ref[:, hs].astype(BF16)

    lv = lam_ref[...]
    lam = (jnp.exp(jnp.sum(lv[0:1] * lv[1:2], axis=-1, keepdims=True))
           - jnp.exp(jnp.sum(lv[2:3] * lv[3:4], axis=-1, keepdims=True)) + lam_init)
    blocks = [(hh, r0) for hh in range(hg) for r0 in range(0, tq, sub)]

    def qk(hh, r0):
        out = []
        for c in range(2):
            cs = slice(hh * DV + c * DK, hh * DV + (c + 1) * DK)
            if prepped:
                q, k = q_ref[r0:r0 + sub, cs], k_ref[:, cs]
            else:
                q = (q_ref[r0:r0 + sub, cs].astype(F32) * Q_SCALE).astype(BF16)
                k = k_scr[hh, :, c * DK:(c + 1) * DK]
            out.append(lax.dot_general(q, k, (((1,), (1,)), ((), ())), preferred_element_type=F32))
        return out

    def finish(hh, r0, scores):
        pvs, invs = [], []
        for s in scores:
            e = jnp.exp2(s - jnp.max(s, axis=-1, keepdims=True))
            invs.append(1.0 / jnp.sum(e, axis=-1, keepdims=True))
            pvs.append(jnp.dot(e.astype(BF16), v_scr[hh], preferred_element_type=F32))
        o = pvs[0] * invs[0] - pvs[1] * (lam * invs[1])
        o_ref[r0:r0 + sub, hh * DV:(hh + 1) * DV] = (_rms(o) * sg_ref[...] * (1.0 - lam_init)).astype(BF16)

    pending = []
    for blk in blocks:
        pending.append((blk, qk(*blk)))
        if len(pending) > ahead:
            done, scores = pending.pop(0)
            finish(*done, scores)
    for done, scores in pending:
        finish(*done, scores)


def _diff_attn(q_arr, q_spec, k_arr, k_spec, v_arr, v_spec, layer, diff_lambda, subln_g3, lam_init, *,
               out_rows, row0, n_seq, seq, n_keys, hg, tq, prepped, ahead, cache_v=None, out_into=None):
    n_cache = 0 if cache_v is None else cache_v.shape[2]
    nq = seq // tq
    w = hg * DV
    in_specs = [q_spec, k_spec, v_spec]
    args = [q_arr, k_arr, v_arr]
    if n_cache:
        in_specs.append(pl.BlockSpec((None, None, n_cache, w), lambda b, h, i: (b, layer, 0, h)))
        args.append(cache_v)
    in_specs += [pl.BlockSpec((None, 4, DK), lambda b, h, i: (layer, 0, 0)),
                 pl.BlockSpec((None, 1, DV), lambda b, h, i: (layer, 0, 0))]
    args += [diff_lambda, subln_g3]
    aliases = {}
    if out_into is not None:
        aliases = {len(args): 0}
        in_specs.append(pl.BlockSpec(memory_space=pl.ANY))
        args.append(out_into)
    ob0 = row0 // tq
    scratch = [] if prepped else [pltpu.VMEM((hg, n_keys, 2 * DK), BF16)]
    scratch.append(pltpu.VMEM((hg, n_keys, DV), BF16))
    return pl.pallas_call(
        functools.partial(_diff_attn_kernel, tq=tq, n_new=seq, n_cache=n_cache, hg=hg, prepped=prepped, ahead=ahead,
                          lam_init=lam_init, aliased=out_into is not None),
        grid=(n_seq, HEADS // hg, nq),
        in_specs=in_specs,
        out_specs=pl.BlockSpec((tq, w), lambda b, h, i: (ob0 + b * nq + i, h)),
        out_shape=jax.ShapeDtypeStruct((out_rows, BRANCH_W), BF16),
        scratch_shapes=scratch,
        input_output_aliases=aliases,
        compiler_params=_cparams("arbitrary", "arbitrary", "arbitrary"),
        name="diff_attn",
    )(*args)


def kernel(x_prompt, x_sample, cache_k, cache_v, state_ret, c, c_ctx, w_mod, b_mod, norm_g, w_in,
           ret_decay_logit, mlp_norm_g, mlp_ws, mlp_bs, diff_lambda, diff_subln_g, w_branch, w_o,
           w_up, w_down):
    batch, seq, d = x_prompt.shape
    dec_batch, dec_seq, _ = x_sample.shape
    depth = w_in.shape[0]
    hidden = w_down.shape[1]
    past = cache_k.shape[2]
    n_ctx = batch * seq
    n_lat = dec_batch * dec_seq
    tok = _Tokens(n_ctx, dec_batch, dec_seq)
    m = tok.total

    c_mu = 2 * HEADS * DK + 2 * BRANCH_W
    c_dq = c_mu + 2 * BRANCH_W
    c_dk = c_dq + BRANCH_W
    c_dv = c_dk + BRANCH_W
    n_main = c_dk + 3 * d

    bm = _pick(math.gcd(n_ctx, n_lat), (1024, 512, 256, 128))
    assert bm % seq == 0
    bn = _pick(math.gcd(d, BRANCH_W), (1024, 512, 256, 128))
    bm_half = _pick(m, (512, 256, 128))
    n_keys = past + dec_seq
    tq_lat = _pick(dec_seq, (1024, 512, 256, 128))

    w_kv0_b = w_in[0:1, :, c_dk:c_dk + 2 * BRANCH_W].astype(BF16)
    w_in_bs = [None] * depth
    w_up_bs = [None] * depth
    w_branch_b = w_o_b = w_down_b = None
    cond = jnp.zeros((N_COND, d), F32).at[0].set(c_ctx).at[1:1 + dec_batch].set(c)
    b_mod3 = b_mod.reshape(depth, 1, 6 * d)
    logit_b = jnp.broadcast_to(ret_decay_logit[:, :, :, None, None], (depth, 2, HEADS, 8, 128))
    bs_b = jnp.broadcast_to(mlp_bs[:, :, :, None], (depth, HEADS, CHUNK, DV))
    mlp_norm_g3 = mlp_norm_g.reshape(depth, 1, BRANCH_W)
    subln_g3 = diff_subln_g.reshape(depth, 1, DV)
    cache_k4 = cache_k.reshape(dec_batch, depth, past, BRANCH_W)
    cache_v4 = cache_v.reshape(dec_batch, depth, past, BRANCH_W)
    rope_tab = _rope_tables(dec_seq)

    mods = [jnp.transpose(_modulation(cond, w_mod, b_mod3, l), (1, 0, 2)) for l in range(depth)]
    x_parts = [x_prompt.reshape(n_ctx, d), x_sample.reshape(n_lat, d)]
    h, = _token_call(tok, x_parts, d, norm_g=norm_g, layer_b=0, mod_b=mods[0], gn=0, shift=0, scale=1)

    new_k = new_v = new_s = None
    main_split = c_dk // bn
    hg_ctx = HEADS
    w_ctx = hg_ctx * DV
    nq_lat = dec_seq // tq_lat
    for l in range(depth):
        lam_init = 0.8 - 0.6 * math.exp(-0.3 * l)
        main_kw = dict(ncols=n_main, out_dtype=BF16, bm=bm, bn=bn, name="w_in_main",
                       wcol=lambda j: jnp.where(j < main_split, j, j + 2 * BRANCH_W // bn))
        if l == 0:
            side = [(w_branch, 0, None), (w_o, 0, None), (w_up, 0, d)]
            z, w_branch_b, w_o_b, w_up0 = _matmul_cast_ahead(h, w_in, 0, side_srcs=side, **main_kw)
            w_branch_b = w_branch_b.reshape(w_branch.shape)
            w_o_b = w_o_b.reshape(w_o.shape)
            w_up_bs[0] = w_up0.reshape(1, d, 2 * hidden)
            w_kv_b, kv_col0 = w_kv0_b, 0
        else:
            z, w_up_l = _matmul(h, w_in_bs[l], 0, row0=0, nrows=m, side_srcs=[(w_up, l * d, d)], **main_kw)
            w_up_bs[l] = w_up_l.reshape(1, d, 2 * hidden)
            w_kv_b, kv_col0 = w_in_bs[l], c_dk
        new_k = _matmul(h, w_kv_b, 0, row0=0, nrows=n_ctx, ncols=BRANCH_W, out_dtype=F32, bm=bm, bn=bn,
                        wcol=lambda j: kv_col0 // bn + j, name="w_in_dk_ctx", stacked=(batch, depth, seq, l),
                        out_into=new_k)
        new_v = _matmul(h, w_kv_b, 0, row0=0, nrows=n_ctx, ncols=BRANCH_W, out_dtype=F32, bm=bm, bn=bn,
                        wcol=lambda j: (kv_col0 + BRANCH_W) // bn + j, name="w_in_dv_ctx",
                        stacked=(batch, depth, seq, l), out_into=new_v)
        kv_lat = _matmul(h, w_kv_b, 0, row0=n_ctx, nrows=n_lat, ncols=2 * BRANCH_W, out_dtype=BF16, bm=bm,
                         bn=bn, wcol=lambda j: kv_col0 // bn + j, name="w_in_kv_lat")

        ret_o, new_s = _retention(z, logit_b, l, row0=0, n_seq=batch, seq=seq, hg=HEADS, state_depth=depth,
                                  state_into=new_s)
        ret_o, = _retention(z, logit_b, l, row0=n_ctx, n_seq=dec_batch, seq=dec_seq, hg=2,
                            rope_tab=rope_tab, state_in=state_ret, out_into=ret_o)

        mlp_o = _gmlp(z, mlp_norm_g3, mlp_ws, bs_b, l, u_col0=c_mu)

        stacked_spec = pl.BlockSpec((None, None, seq, w_ctx), lambda b, hq, i: (b, l, 0, hq))
        diff_o = _diff_attn(
            z, pl.BlockSpec((seq, w_ctx), lambda b, hq, i: (b, c_dq // w_ctx + hq)),
            new_k, stacked_spec, new_v, stacked_spec, l, diff_lambda, subln_g3, lam_init,
            out_rows=m, row0=0, n_seq=batch, seq=seq, n_keys=seq, hg=hg_ctx, tq=seq, prepped=False, ahead=hg_ctx)
        q_lat, k_all = _attn_prep(z, c_dq, n_ctx, kv_lat, cache_k4, l, rope_tab, n_seq=dec_batch, seq=dec_seq)
        diff_o = _diff_attn(
            q_lat, pl.BlockSpec((tq_lat, DV), lambda b, hq, i: (b * nq_lat + i, hq)),
            k_all, pl.BlockSpec((None, n_keys, DV), lambda b, hq, i: (b, 0, hq)),
            kv_lat, pl.BlockSpec((dec_seq, DV), lambda b, hq, i: (b, HEADS + hq)),
            l, diff_lambda, subln_g3, lam_init, out_rows=m, row0=n_ctx, n_seq=dec_batch, seq=dec_seq,
            n_keys=n_keys, hg=1, tq=tq_lat, prepped=True, ahead=0, cache_v=cache_v4, out_into=diff_o)

        merged = _merge(ret_o, mlp_o, diff_o, w_branch_b, l, z, c_dk, d, bm=bm_half, bn=bn)
        y = _matmul(merged, w_o_b, l, row0=0, nrows=m, ncols=d, out_dtype=BF16, bm=bm, bn=bn,
                    wcol=lambda j: j, name="w_o")
        x, h2 = _token_call(tok, x_parts, d, y=y, norm_g=norm_g, layer_a=l, mod_a=mods[l], layer_b=l,
                            mod_b=mods[l], gy=1, gate=2, gn=2, shift=3, scale=4)
        x_parts = [x]

        up_kw = dict(bm=_pick(m, (2 * bm, bm)), bn=_pick(hidden, (256, 128)))
        down_kw = dict(row0=0, nrows=m, ncols=d, out_dtype=BF16, bm=bm_half, bn=_pick(d, (512, 256, 128)),
                       wcol=lambda j: j, name="w_down")
        if l == 0:
            act, w_down_b = _swiglu_up(h2, w_up_bs[0], 0, hidden, side_srcs=[(w_down, 0, None)], **up_kw)
            w_down_b = w_down_b.reshape(w_down.shape)
        if l == 0 and depth > 1:
            f, *rest = _matmul(act, w_down_b, 0, side_srcs=[(w_in, i * d, d) for i in range(1, depth)], **down_kw)
            w_in_bs[1:] = [r.reshape(1, d, w_in.shape[2]) for r in rest]
        elif l == 0:
            f = _matmul(act, w_down_b, 0, **down_kw)
        else:
            act = _swiglu_up(h2, w_up_bs[l], 0, hidden, **up_kw)
            f = _matmul(act, w_down_b, l, **down_kw)
        if l + 1 < depth:
            x, h = _token_call(tok, x_parts, d, y=f, norm_g=norm_g, layer_a=l, mod_a=mods[l], layer_b=l + 1,
                               mod_b=mods[l + 1], gy=3, gate=5, gn=0, shift=0, scale=1)
            x_parts = [x]
        else:
            y_ctx, y_lat = _token_call(tok, x_parts, d, y=f, norm_g=norm_g, layer_a=l, mod_a=mods[l],
                                       split_out=True, gy=3, gate=5)

    return (y_ctx.reshape(batch, seq, d), y_lat.reshape(dec_batch, dec_seq, d),
            new_k.reshape(batch, depth, seq, HEADS, 2, DK), new_v.reshape(batch, depth, seq, HEADS, DV), new_s)
```

```python
import functools
import math

import jax
import jax.numpy as jnp
from jax import lax
from jax.experimental import pallas as pl
from jax.experimental.pallas import tpu as pltpu

F32 = jnp.float32
BF16 = jnp.bfloat16

HEADS = 8
DK = 128
DV = 256
CHUNK = 128
BRANCH_W = HEADS * DV
GRID_W = 64
ROPE_BASE = 10000.0
EPS = 1e-6
N_COND = 16
MIB = 1 << 20
V7X_VMEM_MIB = 64
VMEM_LIMIT_MIB = V7X_VMEM_MIB - 8
LOG2E = 1.4426950408889634


def _pick(n, prefs):
    for p in prefs:
        if n % p == 0:
            return p
    raise ValueError(f"no block size in {prefs} divides {n}")


def _cparams(*dims):
    return pltpu.CompilerParams(dimension_semantics=dims, vmem_limit_bytes=VMEM_LIMIT_MIB * MIB)


def _silu(x):
    return x * jax.nn.sigmoid(x)


def _gelu_tanh(x):
    return 0.5 * x * (1.0 + jnp.tanh(math.sqrt(2.0 / math.pi) * (x + 0.044715 * (x * x * x))))


def _rms(x):
    return x * lax.rsqrt(jnp.mean(x * x, axis=-1, keepdims=True) + EPS)


def _mod_kernel(c_ref, w_ref, b_ref, o_ref):
    s = _silu(c_ref[...]).astype(BF16)
    o_ref[...] = jnp.dot(s, w_ref[...].astype(BF16), preferred_element_type=F32) + b_ref[...]


def _modulation(cond, w_mod, b_mod3, layer):
    n_cond, d = cond.shape
    bn = _pick(d, (512, 256, 128))
    per_seg = d // bn
    return pl.pallas_call(
        _mod_kernel,
        grid=(6 * per_seg,),
        in_specs=[
            pl.BlockSpec((n_cond, d), lambda j: (0, 0)),
            pl.BlockSpec((None, d, bn), lambda j: (layer, 0, j)),
            pl.BlockSpec((None, 1, bn), lambda j: (layer, 0, j)),
        ],
        out_specs=pl.BlockSpec((None, n_cond, bn), lambda j: (j // per_seg, 0, j % per_seg)),
        out_shape=jax.ShapeDtypeStruct((6, n_cond, d), F32),
        compiler_params=_cparams("arbitrary"),
        name="modulation",
    )(cond, w_mod, b_mod3)


class _Tokens:
    def __init__(self, n_ctx, n_dec_seq, dec_seq):
        self.n_ctx = n_ctx
        self.n_dec_seq = n_dec_seq
        self.dec_seq = dec_seq
        self.total = n_ctx + n_dec_seq * dec_seq
        self.br = _pick(math.gcd(n_ctx, dec_seq), (256, 128))
        self.ctx_blocks = n_ctx // self.br

    def cond_index(self, i):
        per = self.dec_seq // self.br
        return jnp.where(i < self.ctx_blocks, 0, 1 + (i - self.ctx_blocks) // per)

    def split_specs(self, d):
        nc = self.ctx_blocks
        return [pl.BlockSpec((self.br, d), lambda i: (jnp.minimum(i, nc - 1), 0)),
                pl.BlockSpec((self.br, d), lambda i: (jnp.maximum(i - nc, 0), 0))]


def _row_spec(tok, d):
    return pl.BlockSpec((tok.br, d), lambda i: (i, 0))


def _ng_spec(layer, d):
    return pl.BlockSpec((None, 4, d), lambda i: (layer, 0, 0))


def _mod_spec(tok, d):
    return pl.BlockSpec((None, 6, d), lambda i: (tok.cond_index(i), 0, 0))


def _token_kernel(*refs, ctx_blocks, split_in, split_out, has_y, has_h, gy, gate, gn, shift, scale):
    it = iter(refs)
    x_refs = [next(it), next(it)] if split_in else [next(it)]
    if has_y:
        y_ref, nga_ref, moda_ref = next(it), next(it), next(it)
    if has_h:
        ngb_ref, modb_ref = next(it), next(it)
    if has_y:
        xo_refs = [next(it), next(it)] if split_out else [next(it)]
    if has_h:
        h_ref = next(it)
    in_ctx = pl.program_id(0) < ctx_blocks

    x = jnp.where(in_ctx, x_refs[0][...], x_refs[1][...]) if split_in else x_refs[0][...]
    if has_y:
        x = x + moda_ref[gate:gate + 1, :] * (_rms(y_ref[...].astype(F32)) * nga_ref[gy:gy + 1, :])
        if split_out:
            @pl.when(in_ctx)
            def _():
                xo_refs[0][...] = x

            @pl.when(jnp.logical_not(in_ctx))
            def _():
                xo_refs[1][...] = x
        else:
            xo_refs[0][...] = x
    if has_h:
        hn = _rms(x) * ngb_ref[gn:gn + 1, :]
        h_ref[...] = (hn * (1.0 + modb_ref[scale:scale + 1, :]) + modb_ref[shift:shift + 1, :]).astype(BF16)


def _token_call(tok, x_parts, d, *, y=None, norm_g=None, layer_a=None, mod_a=None, layer_b=None, mod_b=None,
                split_out=False, gy=0, gate=0, gn=0, shift=0, scale=0):
    split_in = len(x_parts) == 2
    has_y, has_h = y is not None, mod_b is not None
    in_specs = tok.split_specs(d) if split_in else [_row_spec(tok, d)]
    args = list(x_parts)
    if has_y:
        in_specs += [_row_spec(tok, d), _ng_spec(layer_a, d), _mod_spec(tok, d)]
        args += [y, norm_g, mod_a]
    if has_h:
        in_specs += [_ng_spec(layer_b, d), _mod_spec(tok, d)]
        args += [norm_g, mod_b]
    out_specs, out_shape = [], []
    if has_y:
        if split_out:
            out_specs += tok.split_specs(d)
            out_shape += [jax.ShapeDtypeStruct((tok.n_ctx, d), F32),
                          jax.ShapeDtypeStruct((tok.total - tok.n_ctx, d), F32)]
        else:
            out_specs.append(_row_spec(tok, d))
            out_shape.append(jax.ShapeDtypeStruct((tok.total, d), F32))
    if has_h:
        out_specs.append(_row_spec(tok, d))
        out_shape.append(jax.ShapeDtypeStruct((tok.total, d), BF16))
    return pl.pallas_call(
        functools.partial(_token_kernel, ctx_blocks=tok.ctx_blocks, split_in=split_in, split_out=split_out,
                          has_y=has_y, has_h=has_h, gy=gy, gate=gate, gn=gn, shift=shift, scale=scale),
        grid=(tok.total // tok.br,),
        in_specs=in_specs,
        out_specs=out_specs,
        out_shape=out_shape,
        compiler_params=_cparams("arbitrary"),
        name="token_norm",
    )(*args)


class _SideCast:
    def __init__(self, src, steps, first_row=0, n_rows=None):
        cols = src.shape[-1]
        flat = src.reshape(-1, cols)
        n_rows = flat.shape[0] - first_row if n_rows is None else n_rows
        self.rows = next(r for r in (16, 32, 64, 96, 128, 192, 256, 384, 512, 768, 1024)
                         if n_rows % r == 0 and first_row % r == 0 and n_rows // r <= steps)
        self.cols = cols
        self.count = n_rows // self.rows
        self.first = first_row // self.rows
        self.src3 = flat.reshape(-1, self.rows, cols)

    def specs(self, step_of):
        blk = (None, self.rows, self.cols)

        def chunk(*ids):
            return jnp.minimum(step_of(*ids), self.count - 1)

        return (pl.BlockSpec(blk, lambda *ids: (self.first + chunk(*ids), 0, 0)),
                pl.BlockSpec(blk, lambda *ids: (chunk(*ids), 0, 0)),
                jax.ShapeDtypeStruct((self.count, self.rows, self.cols), BF16))


def _side_cast_step(step, sides):
    for in_ref, out_ref, count in sides:
        @pl.when(step < count)
        def _():
            out_ref[...] = in_ref[...].astype(BF16)


def _mm_kernel(*refs, nk, nm, cast, aliased, acc_scratch, side_counts):
    it = iter(refs)
    a_ref, w_ref = next(it), next(it)
    if aliased:
        next(it)
    side_in = [next(it) for _ in side_counts]
    o_ref = next(it)
    side_out = [next(it) for _ in side_counts]
    wb_ref = next(it) if cast else None
    acc_ref = next(it) if acc_scratch else None
    step = (pl.program_id(0) * nm + pl.program_id(1)) * nk + pl.program_id(2)
    _side_cast_step(step, list(zip(side_in, side_out, side_counts)))
    if cast:
        @pl.when(pl.program_id(1) == 0)
        def _():
            wb_ref[...] = w_ref[...].astype(BF16)
        w = wb_ref[...]
    else:
        w = w_ref[...]
    p = jnp.dot(a_ref[...], w, preferred_element_type=F32)
    if nk == 1:
        o_ref[...] = p.astype(o_ref.dtype).reshape(o_ref.shape)
        return
    k = pl.program_id(2)
    tgt = acc_ref if acc_scratch else o_ref

    @pl.when(k == 0)
    def _():
        tgt[...] = p

    @pl.when(k > 0)
    def _():
        tgt[...] += p

    if acc_scratch:
        @pl.when(k == nk - 1)
        def _():
            o_ref[...] = acc_ref[...].astype(o_ref.dtype)


def _matmul(a, w, layer, *, row0, nrows, wcol, ncols, out_dtype, bm, bn, nk=1, name, stacked=None, out_into=None,
            side_srcs=()):
    kdim = a.shape[1]
    bk = kdim // nk
    cast = w.dtype != BF16
    assert nk == 1 or not cast
    acc_scratch = nk > 1 and out_dtype != F32
    r0 = row0 // bm
    nm = nrows // bm
    grid = (ncols // bn, nm, nk)
    sides = [_SideCast(src, grid[0] * nm * nk, first, n) for src, first, n in side_srcs]
    in_specs = [
        pl.BlockSpec((bm, bk), lambda j, i, k: (r0 + i, k)),
        pl.BlockSpec((None, bk, bn), lambda j, i, k: (layer, k, wcol(j))),
    ]
    args = [a, w]
    aliases = {}
    if out_into is not None:
        aliases = {len(args): 0}
        in_specs.append(pl.BlockSpec(memory_space=pl.ANY))
        args.append(out_into)
    if stacked is None:
        out_specs = [pl.BlockSpec((bm, bn), lambda j, i, k: (i, j))]
        out_shape = [jax.ShapeDtypeStruct((nrows, ncols), out_dtype)]
    else:
        n_seq, depth, seq, out_layer = stacked
        out_specs = [pl.BlockSpec((bm // seq, None, seq, bn), lambda j, i, k: (i, out_layer, 0, j))]
        out_shape = [jax.ShapeDtypeStruct((n_seq, depth, seq, ncols), out_dtype)]
    for sc in sides:
        i_spec, o_spec, o_shape = sc.specs(lambda j, i, k: (j * nm + i) * nk + k)
        in_specs.append(i_spec)
        args.append(sc.src3)
        out_specs.append(o_spec)
        out_shape.append(o_shape)
    scratch = ([pltpu.VMEM((bk, bn), BF16)] if cast else []) + ([pltpu.VMEM((bm, bn), F32)] if acc_scratch else [])
    outs = pl.pallas_call(
        functools.partial(_mm_kernel, nk=nk, nm=nm, cast=cast, aliased=out_into is not None,
                          acc_scratch=acc_scratch, side_counts=tuple(sc.count for sc in sides)),
        grid=grid,
        in_specs=in_specs,
        out_specs=out_specs,
        out_shape=out_shape,
        scratch_shapes=scratch,
        input_output_aliases=aliases,
        compiler_params=_cparams("arbitrary", "arbitrary", "arbitrary"),
        name=name,
    )(*args)
    return outs[0] if not sides else outs


def _mm_cast_ahead_kernel(*refs, nb, nm, n_slabs, ks, side_counts):
    it = iter(refs)
    a_ref, w_ref = next(it), next(it)
    side_in = [next(it) for _ in side_counts]
    o_ref = next(it)
    side_out = [next(it) for _ in side_counts]
    w_scr = next(it)
    j, i = pl.program_id(0), pl.program_id(1)
    _side_cast_step(j * nm + i, list(zip(side_in, side_out, side_counts)))

    @pl.when(jnp.logical_and(j < nb, i < n_slabs))
    def _():
        w_scr[j % 2, pl.ds(pl.multiple_of(i * ks, ks), ks), :] = w_ref[...].astype(BF16)

    @pl.when(j >= 1)
    def _():
        o_ref[...] = jnp.dot(a_ref[...], w_scr[(j - 1) % 2], preferred_element_type=F32).astype(o_ref.dtype)


def _matmul_cast_ahead(a, w, layer, *, wcol, ncols, out_dtype, bm, bn, name, side_srcs=()):
    m, kdim = a.shape
    nb, nm = ncols // bn, m // bm
    ks = next(r for r in range(16, kdim + 1, 16) if kdim % r == 0 and kdim // r <= nm)
    n_slabs = kdim // ks
    sides = [_SideCast(src, (nb + 1) * nm, first, n) for src, first, n in side_srcs]

    def row_blk(j, i):
        return jnp.where(j == 0, 0, i)

    in_specs = [
        pl.BlockSpec((bm, kdim), lambda j, i: (row_blk(j, i), 0)),
        pl.BlockSpec((None, ks, bn), lambda j, i: (layer, jnp.where(j == nb, n_slabs - 1, jnp.minimum(i, n_slabs - 1)),
                                                   wcol(jnp.minimum(j, nb - 1)))),
    ]
    args = [a, w]
    out_specs = [pl.BlockSpec((bm, bn), lambda j, i: (row_blk(j, i), jnp.maximum(j - 1, 0)))]
    out_shape = [jax.ShapeDtypeStruct((m, ncols), out_dtype)]
    for sc in sides:
        i_spec, o_spec, o_shape = sc.specs(lambda j, i: j * nm + i)
        in_specs.append(i_spec)
        args.append(sc.src3)
        out_specs.append(o_spec)
        out_shape.append(o_shape)
    outs = pl.pallas_call(
        functools.partial(_mm_cast_ahead_kernel, nb=nb, nm=nm, n_slabs=n_slabs, ks=ks,
                          side_counts=tuple(sc.count for sc in sides)),
        grid=(nb + 1, nm),
        in_specs=in_specs,
        out_specs=out_specs,
        out_shape=out_shape,
        scratch_shapes=[pltpu.VMEM((2, kdim, bn), BF16)],
        compiler_params=_cparams("arbitrary", "arbitrary"),
        name=name,
    )(*args)
    return outs[0] if not sides else outs


def _swiglu_kernel(*refs, half, side_counts):
    it = iter(refs)
    a_ref, wa_ref, wb_ref = next(it), next(it), next(it)
    side_in = [next(it) for _ in side_counts]
    o_ref = next(it)
    side_out = [next(it) for _ in side_counts]
    _side_cast_step(pl.program_id(0) * half + pl.program_id(1), list(zip(side_in, side_out, side_counts)))
    x = a_ref[...]
    ga = jnp.dot(x, wa_ref[...], preferred_element_type=F32)
    gb = jnp.dot(x, wb_ref[...], preferred_element_type=F32)
    o_ref[...] = (_silu(ga) * gb).astype(BF16)


def _swiglu_up(h, w_up_b, layer, hidden, *, bm, bn, side_srcs=()):
    m, d = h.shape
    half = hidden // bn
    nm = m // bm
    sides = [_SideCast(src, half * nm, first, n) for src, first, n in side_srcs]
    in_specs = [
        pl.BlockSpec((bm, d), lambda i, j: (i, 0)),
        pl.BlockSpec((None, d, bn), lambda i, j: (layer, 0, j)),
        pl.BlockSpec((None, d, bn), lambda i, j: (layer, 0, half + j)),
    ]
    args = [h, w_up_b, w_up_b]
    out_specs = [pl.BlockSpec((bm, bn), lambda i, j: (i, j))]
    out_shape = [jax.ShapeDtypeStruct((m, hidden), BF16)]
    for sc in sides:
        i_spec, o_spec, o_shape = sc.specs(lambda i, j: i * half + j)
        in_specs.append(i_spec)
        args.append(sc.src3)
        out_specs.append(o_spec)
        out_shape.append(o_shape)
    outs = pl.pallas_call(
        functools.partial(_swiglu_kernel, half=half, side_counts=tuple(sc.count for sc in sides)),
        grid=(nm, half),
        in_specs=in_specs,
        out_specs=out_specs,
        out_shape=out_shape,
        compiler_params=_cparams("arbitrary", "arbitrary"),
        name="swiglu_up",
    )(*args)
    return outs[0] if not sides else outs


def _merge_kernel(r_ref, m_ref, d_ref, w_ref, g0_ref, g1_ref, g2_ref, o_ref):
    acc = jax.nn.sigmoid(g0_ref[...].astype(F32)) * jnp.dot(r_ref[...], w_ref[0], preferred_element_type=F32)
    acc += jax.nn.sigmoid(g1_ref[...].astype(F32)) * jnp.dot(m_ref[...], w_ref[1], preferred_element_type=F32)
    acc += jax.nn.sigmoid(g2_ref[...].astype(F32)) * jnp.dot(d_ref[...], w_ref[2], preferred_element_type=F32)
    o_ref[...] = acc.astype(BF16)


def _merge(ret_o, mlp_o, diff_o, w_branch, layer, z, gate_col0, d, *, bm, bn):
    m = ret_o.shape[0]
    branch = pl.BlockSpec((bm, BRANCH_W), lambda j, i: (i, 0))

    def gate_spec(t):
        off = (gate_col0 + t * d) // bn
        return pl.BlockSpec((bm, bn), lambda j, i: (i, off + j))

    return pl.pallas_call(
        _merge_kernel,
        grid=(d // bn, m // bm),
        in_specs=[branch, branch, branch,
                  pl.BlockSpec((None, 3, BRANCH_W, bn), lambda j, i: (layer, 0, 0, j)),
                  gate_spec(0), gate_spec(1), gate_spec(2)],
        out_specs=pl.BlockSpec((bm, bn), lambda j, i: (i, j)),
        out_shape=jax.ShapeDtypeStruct((m, d), BF16),
        compiler_params=_cparams("arbitrary", "arbitrary"),
        name="merge",
    )(ret_o, mlp_o, diff_o, w_branch, z, z, z)


def _rope_tables(n_tok):
    n_rows = n_tok // GRID_W
    rows = jnp.repeat(jnp.arange(n_rows, dtype=F32), GRID_W)
    cols = jnp.tile(jnp.arange(GRID_W, dtype=F32), n_rows)
    axis_dim = DK // 2
    inv = ROPE_BASE ** (-jnp.arange(0, axis_dim, 2, dtype=F32) / axis_dim)
    ang = jnp.stack([rows[:, None] * inv, cols[:, None] * inv], axis=1)
    cos, sin = jnp.cos(ang), jnp.sin(ang)
    zero = jnp.zeros_like(sin)
    cos_t = jnp.concatenate([cos, cos], axis=-1).reshape(n_tok, DK)
    sin_lo = jnp.concatenate([-sin, zero], axis=-1).reshape(n_tok, DK)
    sin_hi = jnp.concatenate([zero, sin], axis=-1).reshape(n_tok, DK)
    return jnp.stack([cos_t, sin_lo, sin_hi], axis=0)


def _rope(x, cos_t, sin_lo, sin_hi):
    return x * cos_t + pltpu.roll(x, DK - 32, 1) * sin_lo + pltpu.roll(x, 32, 1) * sin_hi


def _log_sigmoid(x):
    return jnp.minimum(x, 0.0) - jnp.log(1.0 + jnp.exp(-jnp.abs(x)))


def _retention_kernel(*refs, n_chunks, hg, rope, has_s0, emit_state, n_aliased, hoist_tables):
    it = iter(refs)
    q_ref, k_ref, v_ref, g_ref, logit_ref = next(it), next(it), next(it), next(it), next(it)
    rope_ref = next(it) if rope else None
    s0_ref = next(it) if has_s0 else None
    for _ in range(n_aliased):
        next(it)
    o_ref = next(it)
    st_ref = next(it) if emit_state else None
    ks_scr, sb_scr = next(it), next(it)
    tabs = (next(it), next(it), next(it)) if hoist_tables else None

    row = lax.broadcasted_iota(jnp.int32, (CHUNK, CHUNK), 0).astype(F32)
    col = lax.broadcasted_iota(jnp.int32, (CHUNK, CHUNK), 1).astype(F32)
    rel = row - col
    row_v = lax.broadcasted_iota(jnp.int32, (CHUNK, DV), 0).astype(F32)
    k_scale = DK ** -0.5

    def rows(c):
        if isinstance(c, int):
            return pl.ds(c * CHUNK, CHUNK)
        return pl.ds(pl.multiple_of(c * CHUNK, CHUNK), CHUNK)

    def maybe_rope(x, r):
        if not rope:
            return x
        return _rope(x, rope_ref[0, r, :], rope_ref[1, r, :], rope_ref[2, r, :])

    def loop(lo, hi, body, init, reverse=False):
        if n_chunks <= 4:
            carry = init
            for c in (range(hi - 1, lo - 1, -1) if reverse else range(lo, hi)):
                carry = body(c, carry)
            return carry
        if reverse:
            return lax.fori_loop(lo, hi, lambda t, cr: body(hi - 1 - t + lo, cr), init, unroll=8)
        return lax.fori_loop(lo, hi, body, init, unroll=8)

    def make_consts():
        consts = []
        for hh in range(hg):
            lg_f = _log_sigmoid(logit_ref[0, hh])[0:1, :]
            lg_b = _log_sigmoid(logit_ref[1, hh])[0:1, :]
            lg_fv = jnp.concatenate([lg_f, lg_f], axis=1)
            lg_bv = jnp.concatenate([lg_b, lg_b], axis=1)
            consts.append(dict(
                decay=(jnp.where(rel >= 0, jnp.exp(lg_f * jnp.maximum(rel, 0.0)), 0.0)
                       + jnp.where(rel <= 0, jnp.exp(lg_b * jnp.maximum(-rel, 0.0)), 0.0)),
                read_f=jnp.exp(lg_fv * (row_v + 1.0)),
                read_b=jnp.exp(lg_bv * (CHUNK - row_v)),
                write_f=jnp.exp(lg_f * (CHUNK - 1.0 - row)),
                write_b=jnp.exp(lg_b * row),
                cd_f=jnp.exp(lg_fv * float(CHUNK)),
                cd_b=jnp.exp(lg_bv * float(CHUNK)),
            ))
        return consts

    if hoist_tables:
        t128, t256, tcd = tabs

        @pl.when(pl.program_id(0) == 0)
        def _():
            for hh, cst in enumerate(make_consts()):
                t128[hh, 0], t128[hh, 1], t128[hh, 2] = cst["decay"], cst["write_f"], cst["write_b"]
                t256[hh, 0], t256[hh, 1] = cst["read_f"], cst["read_b"]
                tcd[hh, 0] = jnp.broadcast_to(cst["cd_f"], (8, DV))
                tcd[hh, 1] = jnp.broadcast_to(cst["cd_b"], (8, DV))

        consts = [dict(decay=t128[hh, 0], write_f=t128[hh, 1], write_b=t128[hh, 2], read_f=t256[hh, 0],
                       read_b=t256[hh, 1], cd_f=tcd[hh, 0][0:1, :], cd_b=tcd[hh, 1][0:1, :]) for hh in range(hg)]
    else:
        consts = make_consts()
    if has_s0:
        s0_f = tuple(s0_ref[0, hh] for hh in range(hg))
        s0_b = tuple(s0_ref[1, hh] for hh in range(hg))
    else:
        s0_f = s0_b = tuple(jnp.zeros((DK, DV), F32) for _ in range(hg))

    def bwd_body(c, s_b):
        r = rows(c)
        out = []
        for hh, cst in enumerate(consts):
            kq = slice(hh * DK, (hh + 1) * DK)
            vq = slice(hh * DV, (hh + 1) * DV)
            ks = maybe_rope(k_ref[r, kq].astype(F32), r) * k_scale
            ks_scr[r, kq] = ks.astype(BF16)
            sb_scr[hh, c] = s_b[hh].astype(BF16)
            kw = (ks * cst["write_b"]).astype(BF16)
            out.append(s_b[hh] * cst["cd_b"] + lax.dot_general(
                kw, v_ref[r, vq], (((0,), (0,)), ((), ())), preferred_element_type=F32))
        return tuple(out)

    s_b = loop(0, n_chunks, bwd_body, s0_b, reverse=True)

    def fwd_body(c, s_f):
        r = rows(c)
        out = []
        for hh, cst in enumerate(consts):
            kq = slice(hh * DK, (hh + 1) * DK)
            vq = slice(hh * DV, (hh + 1) * DV)
            q = maybe_rope(q_ref[r, kq].astype(F32), r).astype(BF16)
            ks = ks_scr[r, kq]
            v = v_ref[r, vq]
            att = lax.dot_general(q, ks, (((1,), (1,)), ((), ())), preferred_element_type=F32) * cst["decay"]
            o = jnp.dot(att.astype(BF16), v, preferred_element_type=F32)
            o += jnp.dot(q, s_f[hh].astype(BF16), preferred_element_type=F32) * cst["read_f"]
            o += jnp.dot(q, sb_scr[hh, c], preferred_element_type=F32) * cst["read_b"]
            oc = o - jnp.mean(o, axis=-1, keepdims=True)
            on = oc * lax.rsqrt(jnp.mean(oc * oc, axis=-1, keepdims=True) + EPS)
            o_ref[r, vq] = (_silu(g_ref[r, vq].astype(F32)) * on).astype(BF16)
            kw = (ks.astype(F32) * cst["write_f"]).astype(BF16)
            out.append(s_f[hh] * cst["cd_f"] + lax.dot_general(
                kw, v, (((0,), (0,)), ((), ())), preferred_element_type=F32))
        return tuple(out)

    s_f = loop(0, n_chunks, fwd_body, s0_f)
    if emit_state:
        for hh in range(hg):
            st_ref[0, hh] = s_f[hh]
            st_ref[1, hh] = s_b[hh]


def _retention(z, logit_b, layer, *, row0, n_seq, seq, hg, rope_tab=None, state_in=None, state_depth=None,
               out_into=None, state_into=None):
    emit_state = state_depth is not None
    hoist = hg == HEADS and n_seq > 1
    n_chunks = seq // CHUNK
    sb = row0 // seq
    kw, vw = hg * DK, hg * DV
    in_specs = [
        pl.BlockSpec((seq, kw), lambda b, h: (sb + b, h)),
        pl.BlockSpec((seq, kw), lambda b, h: (sb + b, HEADS * DK // kw + h)),
        pl.BlockSpec((seq, vw), lambda b, h: (sb + b, 2 * HEADS * DK // vw + h)),
        pl.BlockSpec((seq, vw), lambda b, h: (sb + b, (2 * HEADS * DK + BRANCH_W) // vw + h)),
        pl.BlockSpec((None, 2, hg, 8, 128), lambda b, h: (layer, 0, h, 0, 0)),
    ]
    args = [z, z, z, z, logit_b]
    if rope_tab is not None:
        in_specs.append(pl.BlockSpec((3, seq, DK), lambda b, h: (0, 0, 0)))
        args.append(rope_tab)
    if state_in is not None:
        in_specs.append(pl.BlockSpec((None, None, 2, hg, DK, DV), lambda b, h: (b, layer, 0, h, 0, 0)))
        args.append(state_in)
    aliases = {}
    for out_idx, arr in ((0, out_into), (1, state_into)):
        if arr is not None:
            aliases[len(args)] = out_idx
            in_specs.append(pl.BlockSpec(memory_space=pl.ANY))
            args.append(arr)
    out_specs = [pl.BlockSpec((seq, vw), lambda b, h: (sb + b, h))]
    out_shape = [jax.ShapeDtypeStruct((z.shape[0], BRANCH_W), BF16)]
    if emit_state:
        out_specs.append(pl.BlockSpec((None, None, 2, hg, DK, DV), lambda b, h: (b, layer, 0, h, 0, 0)))
        out_shape.append(jax.ShapeDtypeStruct((n_seq, state_depth, 2, HEADS, DK, DV), F32))
    return pl.pallas_call(
        functools.partial(_retention_kernel, n_chunks=n_chunks, hg=hg, rope=rope_tab is not None,
                          has_s0=state_in is not None, emit_state=emit_state, n_aliased=len(aliases), hoist_tables=hoist),
        grid=(n_seq, HEADS // hg),
        in_specs=in_specs,
        out_specs=out_specs,
        out_shape=out_shape,
        scratch_shapes=[pltpu.VMEM((seq, kw), BF16), pltpu.VMEM((hg, n_chunks, DK, DV), BF16)] + (
            [pltpu.VMEM((hg, 3, CHUNK, CHUNK), F32), pltpu.VMEM((hg, 2, CHUNK, DV), F32),
             pltpu.VMEM((hg, 2, 8, DV), F32)] if hoist else []),
        input_output_aliases=aliases,
        compiler_params=_cparams("arbitrary", "arbitrary"),
        name="retention",
    )(*args)


def _gmlp_kernel(u_ref, v_ref, ng_ref, ws_ref, bs_ref, o_ref, *, n_chunks):
    vn = (_rms(_gelu_tanh(v_ref[...].astype(F32))) * ng_ref[...]).astype(BF16)
    for t in range(n_chunks):
        r = slice(t * CHUNK, (t + 1) * CHUNK)
        for g in range(HEADS):
            cs = slice(g * DV, (g + 1) * DV)
            mixed = jnp.dot(ws_ref[g].astype(BF16), vn[r, cs], preferred_element_type=F32) + bs_ref[g]
            o_ref[r, cs] = (_gelu_tanh(u_ref[r, cs].astype(F32)) * mixed).astype(BF16)


def _gmlp(z, mlp_norm_g3, mlp_ws, bs_b, layer, *, u_col0):
    m = z.shape[0]
    bt = _pick(m, (512, 256, 128))
    ub = u_col0 // BRANCH_W
    return pl.pallas_call(
        functools.partial(_gmlp_kernel, n_chunks=bt // CHUNK),
        grid=(m // bt,),
        in_specs=[
            pl.BlockSpec((bt, BRANCH_W), lambda i: (i, ub)),
            pl.BlockSpec((bt, BRANCH_W), lambda i: (i, ub + 1)),
            pl.BlockSpec((None, 1, BRANCH_W), lambda i: (layer, 0, 0)),
            pl.BlockSpec((None, HEADS, CHUNK, CHUNK), lambda i: (layer, 0, 0, 0)),
            pl.BlockSpec((None, HEADS, CHUNK, DV), lambda i: (layer, 0, 0, 0)),
        ],
        out_specs=pl.BlockSpec((bt, BRANCH_W), lambda i: (i, 0)),
        out_shape=jax.ShapeDtypeStruct((m, BRANCH_W), BF16),
        compiler_params=_cparams("arbitrary"),
        name="gmlp",
    )(z, z, mlp_norm_g3, mlp_ws, bs_b)


Q_SCALE = DK ** -0.5 * LOG2E


def _attn_prep_kernel(q_ref, k_ref, kc_ref, rope_ref, qo_ref, ko_ref, *, cache_blocks):
    r = pl.program_id(1)

    @pl.when(r < cache_blocks)
    def _():
        ko_ref[...] = kc_ref[...].astype(BF16)

    @pl.when(r >= cache_blocks)
    def _():
        cos_t, sin_lo, sin_hi = rope_ref[0], rope_ref[1], rope_ref[2]
        for g in range(BRANCH_W // DK):
            gs = slice(g * DK, (g + 1) * DK)
            ko_ref[:, gs] = _rope(k_ref[:, gs].astype(F32), cos_t, sin_lo, sin_hi).astype(BF16)
            qo_ref[:, gs] = (_rope(q_ref[:, gs].astype(F32), cos_t, sin_lo, sin_hi) * Q_SCALE).astype(BF16)


def _attn_prep(z, q_col0, row0, kv_lat, cache_k4, layer, rope_tab, *, n_seq, seq):
    past = cache_k4.shape[2]
    tr = _pick(math.gcd(past, seq), (512, 256, 128))
    cb, nb = past // tr, seq // tr
    qb0, qc0 = row0 // tr, q_col0 // BRANCH_W

    def new_blk(r):
        return jnp.maximum(r - cb, 0)

    return pl.pallas_call(
        functools.partial(_attn_prep_kernel, cache_blocks=cb),
        grid=(n_seq, cb + nb),
        in_specs=[
            pl.BlockSpec((tr, BRANCH_W), lambda b, r: (qb0 + b * nb + new_blk(r), qc0)),
            pl.BlockSpec((tr, BRANCH_W), lambda b, r: (b * nb + new_blk(r), 0)),
            pl.BlockSpec((None, None, tr, BRANCH_W), lambda b, r: (b, layer, jnp.minimum(r, cb - 1), 0)),
            pl.BlockSpec((3, tr, DK), lambda b, r: (0, new_blk(r), 0)),
        ],
        out_specs=[
            pl.BlockSpec((tr, BRANCH_W), lambda b, r: (b * nb + new_blk(r), 0)),
            pl.BlockSpec((None, tr, BRANCH_W), lambda b, r: (b, r, 0)),
        ],
        out_shape=[jax.ShapeDtypeStruct((n_seq * seq, BRANCH_W), BF16),
                   jax.ShapeDtypeStruct((n_seq, past + seq, BRANCH_W), BF16)],
        compiler_params=_cparams("arbitrary", "arbitrary"),
        name="attn_prep",
    )(z, kv_lat, cache_k4, rope_tab)


def _diff_attn_kernel(*refs, tq, n_new, n_cache, hg, prepped, ahead, lam_init, aliased):
    it = iter(refs)
    q_ref, k_ref, v_ref = next(it), next(it), next(it)
    vc_ref = next(it) if n_cache else None
    lam_ref, sg_ref = next(it), next(it)
    if aliased:
        next(it)
    o_ref = next(it)
    k_scr = None if prepped else next(it)
    v_scr = next(it)
    sub = min(tq, 256)

    @pl.when(pl.program_id(2) == 0)
    def _():
        for hh in range(hg):
            hs = slice(hh * DV, (hh + 1) * DV)
            if not prepped:
                k_scr[hh] = k_ref[:, hs].astype(BF16)
            if n_cache:
                v_scr[hh, 0:n_cache, :] = vc_ref[:, hs].astype(BF16)
            v_scr[hh, n_cache:n_cache + n_new, :] = v_ref[:, hs].astype(BF16)

    lv = lam_ref[...]
    lam = (jnp.exp(jnp.sum(lv[0:1] * lv[1:2], axis=-1, keepdims=True))
           - jnp.exp(jnp.sum(lv[2:3] * lv[3:4], axis=-1, keepdims=True)) + lam_init)
    blocks = [(hh, r0) for hh in range(hg) for r0 in range(0, tq, sub)]

    def qk(hh, r0):
        out = []
        for c in range(2):
            cs = slice(hh * DV + c * DK, hh * DV + (c + 1) * DK)
            if prepped:
                q, k = q_ref[r0:r0 + sub, cs], k_ref[:, cs]
            else:
                q = (q_ref[r0:r0 + sub, cs].astype(F32) * Q_SCALE).astype(BF16)
                k = k_scr[hh, :, c * DK:(c + 1) * DK]
            out.append(lax.dot_general(q, k, (((1,), (1,)), ((), ())), preferred_element_type=F32))
        return out

    def finish(hh, r0, scores):
        pvs, invs = [], []
        for s in scores:
            e = jnp.exp2(s - jnp.max(s, axis=-1, keepdims=True))
            invs.append(1.0 / jnp.sum(e, axis=-1, keepdims=True))
            pvs.append(jnp.dot(e.astype(BF16), v_scr[hh], preferred_element_type=F32))
        o = pvs[0] * invs[0] - pvs[1] * (lam * invs[1])
        o_ref[r0:r0 + sub, hh * DV:(hh + 1) * DV] = (_rms(o) * sg_ref[...] * (1.0 - lam_init)).astype(BF16)

    pending = []
    for blk in blocks:
        pending.append((blk, qk(*blk)))
        if len(pending) > ahead:
            done, scores = pending.pop(0)
            finish(*done, scores)
    for done, scores in pending:
        finish(*done, scores)


def _diff_attn(q_arr, q_spec, k_arr, k_spec, v_arr, v_spec, layer, diff_lambda, subln_g3, lam_init, *,
               out_rows, row0, n_seq, seq, n_keys, hg, tq, prepped, ahead, cache_v=None, out_into=None):
    n_cache = 0 if cache_v is None else cache_v.shape[2]
    nq = seq // tq
    w = hg * DV
    in_specs = [q_spec, k_spec, v_spec]
    args = [q_arr, k_arr, v_arr]
    if n_cache:
        in_specs.append(pl.BlockSpec((None, None, n_cache, w), lambda b, h, i: (b, layer, 0, h)))
        args.append(cache_v)
    in_specs += [pl.BlockSpec((None, 4, DK), lambda b, h, i: (layer, 0, 0)),
                 pl.BlockSpec((None, 1, DV), lambda b, h, i: (layer, 0, 0))]
    args += [diff_lambda, subln_g3]
    aliases = {}
    if out_into is not None:
        aliases = {len(args): 0}
        in_specs.append(pl.BlockSpec(memory_space=pl.ANY))
        args.append(out_into)
    ob0 = row0 // tq
    scratch = [] if prepped else [pltpu.VMEM((hg, n_keys, 2 * DK), BF16)]
    scratch.append(pltpu.VMEM((hg, n_keys, DV), BF16))
    return pl.pallas_call(
        functools.partial(_diff_attn_kernel, tq=tq, n_new=seq, n_cache=n_cache, hg=hg, prepped=prepped, ahead=ahead,
                          lam_init=lam_init, aliased=out_into is not None),
        grid=(n_seq, HEADS // hg, nq),
        in_specs=in_specs,
        out_specs=pl.BlockSpec((tq, w), lambda b, h, i: (ob0 + b * nq + i, h)),
        out_shape=jax.ShapeDtypeStruct((out_rows, BRANCH_W), BF16),
        scratch_shapes=scratch,
        input_output_aliases=aliases,
        compiler_params=_cparams("arbitrary", "arbitrary", "arbitrary"),
        name="diff_attn",
    )(*args)


def kernel(x_prompt, x_sample, cache_k, cache_v, state_ret, c, c_ctx, w_mod, b_mod, norm_g, w_in,
           ret_decay_logit, mlp_norm_g, mlp_ws, mlp_bs, diff_lambda, diff_subln_g, w_branch, w_o,
           w_up, w_down):
    batch, seq, d = x_prompt.shape
    dec_batch, dec_seq, _ = x_sample.shape
    depth = w_in.shape[0]
    hidden = w_down.shape[1]
    past = cache_k.shape[2]
    n_ctx = batch * seq
    n_lat = dec_batch * dec_seq
    tok = _Tokens(n_ctx, dec_batch, dec_seq)
    m = tok.total

    c_mu = 2 * HEADS * DK + 2 * BRANCH_W
    c_dq = c_mu + 2 * BRANCH_W
    c_dk = c_dq + BRANCH_W
    c_dv = c_dk + BRANCH_W
    n_main = c_dk + 3 * d

    bm = _pick(math.gcd(n_ctx, n_lat), (1024, 512, 256, 128))
    assert bm % seq == 0
    bn = _pick(math.gcd(d, BRANCH_W), (1024, 512, 256, 128))
    bm_half = _pick(m, (512, 256, 128))
    n_keys = past + dec_seq
    tq_lat = _pick(dec_seq, (1024, 512, 256, 128))

    w_kv0_b = w_in[0:1, :, c_dk:c_dk + 2 * BRANCH_W].astype(BF16)
    w_in_bs = [None] * depth
    w_up_bs = [None] * depth
    w_branch_b = w_o_b = w_down_b = None
    cond = jnp.zeros((N_COND, d), F32).at[0].set(c_ctx).at[1:1 + dec_batch].set(c)
    b_mod3 = b_mod.reshape(depth, 1, 6 * d)
    logit_b = jnp.broadcast_to(ret_decay_logit[:, :, :, None, None], (depth, 2, HEADS, 8, 128))
    bs_b = jnp.broadcast_to(mlp_bs[:, :, :, None], (depth, HEADS, CHUNK, DV))
    mlp_norm_g3 = mlp_norm_g.reshape(depth, 1, BRANCH_W)
    subln_g3 = diff_subln_g.reshape(depth, 1, DV)
    cache_k4 = cache_k.reshape(dec_batch, depth, past, BRANCH_W)
    cache_v4 = cache_v.reshape(dec_batch, depth, past, BRANCH_W)
    rope_tab = _rope_tables(dec_seq)

    mods = [jnp.transpose(_modulation(cond, w_mod, b_mod3, l), (1, 0, 2)) for l in range(depth)]
    x_parts = [x_prompt.reshape(n_ctx, d), x_sample.reshape(n_lat, d)]
    h, = _token_call(tok, x_parts, d, norm_g=norm_g, layer_b=0, mod_b=mods[0], gn=0, shift=0, scale=1)

    new_k = new_v = new_s = None
    main_split = c_dk // bn
    hg_ctx = HEADS
    w_ctx = hg_ctx * DV
    nq_lat = dec_seq // tq_lat
    for l in range(depth):
        lam_init = 0.8 - 0.6 * math.exp(-0.3 * l)
        main_kw = dict(ncols=n_main, out_dtype=BF16, bm=bm, bn=bn, name="w_in_main",
                       wcol=lambda j: jnp.where(j < main_split, j, j + 2 * BRANCH_W // bn))
        if l == 0:
            side = [(w_branch, 0, None), (w_o, 0, None), (w_up, 0, d)]
            z, w_branch_b, w_o_b, w_up0 = _matmul_cast_ahead(h, w_in, 0, side_srcs=side, **main_kw)
            w_branch_b = w_branch_b.reshape(w_branch.shape)
            w_o_b = w_o_b.reshape(w_o.shape)
            w_up_bs[0] = w_up0.reshape(1, d, 2 * hidden)
            w_kv_b, kv_col0 = w_kv0_b, 0
        else:
            z, w_up_l = _matmul(h, w_in_bs[l], 0, row0=0, nrows=m, side_srcs=[(w_up, l * d, d)], **main_kw)
            w_up_bs[l] = w_up_l.reshape(1, d, 2 * hidden)
            w_kv_b, kv_col0 = w_in_bs[l], c_dk
        new_k = _matmul(h, w_kv_b, 0, row0=0, nrows=n_ctx, ncols=BRANCH_W, out_dtype=F32, bm=bm, bn=bn,
                        wcol=lambda j: kv_col0 // bn + j, name="w_in_dk_ctx", stacked=(batch, depth, seq, l),
                        out_into=new_k)
        new_v = _matmul(h, w_kv_b, 0, row0=0, nrows=n_ctx, ncols=BRANCH_W, out_dtype=F32, bm=bm, bn=bn,
                        wcol=lambda j: (kv_col0 + BRANCH_W) // bn + j, name="w_in_dv_ctx",
                        stacked=(batch, depth, seq, l), out_into=new_v)
        kv_lat = _matmul(h, w_kv_b, 0, row0=n_ctx, nrows=n_lat, ncols=2 * BRANCH_W, out_dtype=BF16, bm=bm,
                         bn=bn, wcol=lambda j: kv_col0 // bn + j, name="w_in_kv_lat")

        ret_o, new_s = _retention(z, logit_b, l, row0=0, n_seq=batch, seq=seq, hg=HEADS, state_depth=depth,
                                  state_into=new_s)
        ret_o, = _retention(z, logit_b, l, row0=n_ctx, n_seq=dec_batch, seq=dec_seq, hg=2,
                            rope_tab=rope_tab, state_in=state_ret, out_into=ret_o)

        mlp_o = _gmlp(z, mlp_norm_g3, mlp_ws, bs_b, l, u_col0=c_mu)

        stacked_spec = pl.BlockSpec((None, None, seq, w_ctx), lambda b, hq, i: (b, l, 0, hq))
        diff_o = _diff_attn(
            z, pl.BlockSpec((seq, w_ctx), lambda b, hq, i: (b, c_dq // w_ctx + hq)),
            new_k, stacked_spec, new_v, stacked_spec, l, diff_lambda, subln_g3, lam_init,
            out_rows=m, row0=0, n_seq=batch, seq=seq, n_keys=seq, hg=hg_ctx, tq=seq, prepped=False, ahead=hg_ctx)
        q_lat, k_all = _attn_prep(z, c_dq, n_ctx, kv_lat, cache_k4, l, rope_tab, n_seq=dec_batch, seq=dec_seq)
        diff_o = _diff_attn(
            q_lat, pl.BlockSpec((tq_lat, DV), lambda b, hq, i: (b * nq_lat + i, hq)),
            k_all, pl.BlockSpec((None, n_keys, DV), lambda b, hq, i: (b, 0, hq)),
            kv_lat, pl.BlockSpec((dec_seq, DV), lambda b, hq, i: (b, HEADS + hq)),
            l, diff_lambda, subln_g3, lam_init, out_rows=m, row0=n_ctx, n_seq=dec_batch, seq=dec_seq,
            n_keys=n_keys, hg=1, tq=tq_lat, prepped=True, ahead=0, cache_v=cache_v4, out_into=diff_o)

        merged = _merge(ret_o, mlp_o, diff_o, w_branch_b, l, z, c_dk, d, bm=bm_half, bn=bn)
        y = _matmul(merged, w_o_b, l, row0=0, nrows=m, ncols=d, out_dtype=BF16, bm=bm, bn=bn,
                    wcol=lambda j: j, name="w_o")
        x, h2 = _token_call(tok, x_parts, d, y=y, norm_g=norm_g, layer_a=l, mod_a=mods[l], layer_b=l,
                            mod_b=mods[l], gy=1, gate=2, gn=2, shift=3, scale=4)
        x_parts = [x]

        up_kw = dict(bm=_pick(m, (2 * bm, bm)), bn=_pick(hidden, (256, 128)))
        down_kw = dict(row0=0, nrows=m, ncols=d, out_dtype=BF16, bm=bm_half, bn=_pick(d, (512, 256, 128)),
                       wcol=lambda j: j, name="w_down")
        if l == 0:
            act, w_down_b = _swiglu_up(h2, w_up_bs[0], 0, hidden, side_srcs=[(w_down, 0, None)], **up_kw)
            w_down_b = w_down_b.reshape(w_down.shape)
        if l == 0 and depth > 1:
            f, *rest = _matmul(act, w_down_b, 0, side_srcs=[(w_in, i * d, d) for i in range(1, depth)], **down_kw)
            w_in_bs[1:] = [r.reshape(1, d, w_in.shape[2]) for r in rest]
        elif l == 0:
            f = _matmul(act, w_down_b, 0, **down_kw)
        else:
            act = _swiglu_up(h2, w_up_bs[l], 0, hidden, **up_kw)
            f = _matmul(act, w_down_b, l, **down_kw)
        if l + 1 < depth:
            x, h = _token_call(tok, x_parts, d, y=f, norm_g=norm_g, layer_a=l, mod_a=mods[l], layer_b=l + 1,
                               mod_b=mods[l + 1], gy=3, gate=5, gn=0, shift=0, scale=1)
            x_parts = [x]
        else:
            y_ctx, y_lat = _token_call(tok, x_parts, d, y=f, norm_g=norm_g, layer_a=l, mod_a=mods[l],
                                       split_out=True, gy=3, gate=5)

    return (y_ctx.reshape(batch, seq, d), y_lat.reshape(dec_batch, dec_seq, d),
            new_k.reshape(batch, depth, seq, HEADS, 2, DK), new_v.reshape(batch, depth, seq, HEADS, DV), new_s)
```
